```python
import math
import jax, jax.numpy as jnp
from jax import lax
import numpy as np

D_MODEL = 1024
BATCH = 1
SEQ = 16384
DEPTH = 4

N_MIXERS = 4
D_FF = 2816
EPS = 1e-6
S5_GROUP = 16
S5_GROUPS = D_MODEL // S5_GROUP
S5_STATE = 64
DT_MIN = 1e-3
DT_MAX = 1e-1
CONV_W = 31
GM_CHUNK = 128
GM_E = 2 * D_MODEL
GM_HEADS = 8
HEAD_DIM = 64
AT_HEADS = D_MODEL // HEAD_DIM
PATTERNS = ((128, 1), (512, 4), (2048, 16))
N_PATTERNS = len(PATTERNS)
BLOCK = 128
NUM_BUCKETS = 32
MAX_DISTANCE = 2048

kernel_name = "hybrid_interleaved_s5_conv_gmlp_dilated_attn"


def rmsnorm(x, g):
    xf = x.astype(jnp.float32)
    y = xf * lax.rsqrt(jnp.mean(xf * xf, axis=-1, keepdims=True) + EPS)
    return (y * g.astype(jnp.float32)).astype(x.dtype)


def layernorm(x, g, b):
    xf = x.astype(jnp.float32)
    mu = jnp.mean(xf, axis=-1, keepdims=True)
    var = jnp.mean(jnp.square(xf - mu), axis=-1, keepdims=True)
    y = (xf - mu) * lax.rsqrt(var + EPS)
    return (y * g.astype(jnp.float32) + b.astype(jnp.float32)).astype(x.dtype)


def swiglu(h, w1, w3, w2):
    return (jax.nn.silu(h @ w1) * (h @ w3)) @ w2


def _complex_scan_op(left, right):
    a1r, a1i, b1r, b1i = left
    a2r, a2i, b2r, b2i = right
    return (a2r * a1r - a2i * a1i,
            a2r * a1i + a2i * a1r,
            a2r * b1r - a2i * b1i + b2r,
            a2r * b1i + a2i * b1r + b2i)


def s5_mixer(h, w_in, a_re, a_im, log_dt, b_re, b_im, c_re, c_im, d_skip, w_glu, b_glu, w_out):
    bsz, seq, _ = h.shape
    u = (h @ w_in).astype(jnp.float32).reshape(bsz, seq, S5_GROUPS, S5_GROUP)
    ar = a_re.astype(jnp.float32)
    ai = a_im.astype(jnp.float32)
    dt = jnp.exp(log_dt.astype(jnp.float32))[:, None]
    mag = jnp.exp(dt * ar)
    abar_re = mag * jnp.cos(dt * ai)
    abar_im = mag * jnp.sin(dt * ai)
    den = ar * ar + ai * ai
    nr = abar_re - 1.0
    f_re = (nr * ar + abar_im * ai) / den
    f_im = (abar_im * ar - nr * ai) / den
    br = b_re.astype(jnp.float32)
    bi = b_im.astype(jnp.float32)
    bb_re = f_re[..., None] * br - f_im[..., None] * bi
    bb_im = f_re[..., None] * bi + f_im[..., None] * br
    bu_re = jnp.einsum('blgh,gph->blgp', u, bb_re)
    bu_im = jnp.einsum('blgh,gph->blgp', u, bb_im)
    elems = (jnp.broadcast_to(abar_re, bu_re.shape), jnp.broadcast_to(abar_im, bu_re.shape), bu_re, bu_im)
    _, _, st_re, st_im = lax.associative_scan(_complex_scan_op, elems, axis=1)
    y = (jnp.einsum('blgp,ghp->blgh', st_re, c_re.astype(jnp.float32))
         - jnp.einsum('blgp,ghp->blgh', st_im, c_im.astype(jnp.float32))
         + d_skip.astype(jnp.float32).reshape(S5_GROUPS, S5_GROUP) * u)
    y = jax.nn.gelu(y.reshape(bsz, seq, D_MODEL)).astype(h.dtype)
    z = y * jax.nn.sigmoid(y @ w_glu + b_glu)
    return z @ w_out


def conv_mixer(h, w_in, b_in, dw, dw_b, ln_g, ln_b, w_out, b_out):
    z = h @ w_in + b_in
    a, g = jnp.split(z, 2, axis=-1)
    z = a * jax.nn.sigmoid(g)
    z = lax.conv_general_dilated(z, dw[:, None, :].astype(z.dtype), window_strides=(1,),
                                 padding=[(CONV_W - 1, 0)],
                                 dimension_numbers=('NWC', 'WIO', 'NWC'),
                                 feature_group_count=D_MODEL) + dw_b
    z = jax.nn.silu(layernorm(z, ln_g, ln_b))
    return z @ w_out + b_out


def gmlp_mixer(h, w_in, b_in, ln_g, ln_b, w_s, b_s, w_out, b_out):
    bsz, seq, _ = h.shape
    z = jax.nn.gelu(h @ w_in + b_in)
    u, v = jnp.split(z, 2, axis=-1)
    v = layernorm(v, ln_g, ln_b)
    vc = v.reshape(bsz, seq // GM_CHUNK, GM_CHUNK, GM_HEADS, GM_E // GM_HEADS)
    causal = jnp.tril(jnp.ones((GM_CHUNK, GM_CHUNK), jnp.float32))
    s = jnp.einsum('hts,bnshc->bnthc', w_s * causal, vc) + b_s.T[None, None, :, :, None]
    s = s.reshape(bsz, seq, GM_E).astype(u.dtype)
    return (u * s) @ w_out + b_out


def t5_bucket(dist):
    max_exact = NUM_BUCKETS // 2
    distf = jnp.maximum(dist, 1).astype(jnp.float32)
    large = max_exact + (jnp.log(distf / max_exact) / math.log(MAX_DISTANCE / max_exact)
                         * (NUM_BUCKETS - max_exact)).astype(jnp.int32)
    large = jnp.minimum(large, NUM_BUCKETS - 1)
    return jnp.where(dist < max_exact, dist, large)


def _band_delta():
    return (jnp.arange(BLOCK)[:, None] + BLOCK) - jnp.arange(2 * BLOCK)[None, :]


def rel_bias_block(table, dilation):
    dist = jnp.maximum(_band_delta(), 0) * dilation
    return table.astype(jnp.float32)[t5_bucket(dist)].transpose(2, 0, 1)


def dilated_window_attention(q, k, v, bias, window, dilation):
    bsz, seq, heads, hd = q.shape
    span = BLOCK * dilation
    seq_p = -(-seq // span) * span
    n_sub = seq_p // dilation
    n_blk = n_sub // BLOCK

    def to_blocks(t):
        t = jnp.pad(t.astype(jnp.float32), ((0, 0), (0, seq_p - seq), (0, 0), (0, 0)))
        t = t.reshape(bsz, n_sub, dilation, heads, hd).transpose(0, 2, 1, 3, 4)
        return t.reshape(bsz, dilation, n_blk, BLOCK, heads, hd)

    def with_prev(t):
        prev = jnp.pad(t, ((0, 0), (0, 0), (1, 0), (0, 0), (0, 0), (0, 0)))[:, :, :-1]
        return jnp.concatenate([prev, t], axis=3)

    qb = to_blocks(q)
    kk = with_prev(to_blocks(k))
    vv = with_prev(to_blocks(v))
    logits = jnp.einsum('brnqhd,brnkhd->brnhqk', qb, kk) * (hd ** -0.5) + bias[None, None, None]
    delta = _band_delta()
    in_band = (delta >= 0) & (delta <= window // dilation)
    not_before_start = (jnp.arange(n_blk)[:, None, None] > 0) | (jnp.arange(2 * BLOCK) >= BLOCK)[None, None, :]
    mask = in_band[None] & not_before_start
    logits = jnp.where(mask[None, None, :, None], logits, -jnp.inf)
    m = jnp.max(logits, axis=-1, keepdims=True)
    p = jnp.exp(logits - m)
    den = jnp.sum(p, axis=-1)
    o = jnp.einsum('brnhqk,brnkhd->brnqhd', p, vv) / jnp.swapaxes(den, -1, -2)[..., None]
    lse = jnp.swapaxes(m[..., 0] + jnp.log(den), -1, -2)
    o = o.reshape(bsz, dilation, n_sub, heads, hd).transpose(0, 2, 1, 3, 4).reshape(bsz, seq_p, heads, hd)[:, :seq]
    lse = lse.reshape(bsz, dilation, n_sub, heads).transpose(0, 2, 1, 3).reshape(bsz, seq_p, heads)[:, :seq]
    return o, lse


def attention_mixer(h, w_qkv, w_out, rel_bias):
    bsz, seq, _ = h.shape
    qkv = (h @ w_qkv).reshape(bsz, seq, N_PATTERNS, 3, AT_HEADS, HEAD_DIM)
    outs, lses = [], []
    for g, (window, dilation) in enumerate(PATTERNS):
        bias = rel_bias_block(rel_bias[:, g * AT_HEADS:(g + 1) * AT_HEADS], dilation)
        o, lse = dilated_window_attention(qkv[:, :, g, 0], qkv[:, :, g, 1], qkv[:, :, g, 2], bias, window, dilation)
        outs.append(o)
        lses.append(lse)
    wts = jax.nn.softmax(jnp.stack(lses, axis=0), axis=0)
    o = jnp.sum(wts[..., None] * jnp.stack(outs, axis=0), axis=0)
    return o.reshape(bsz, seq, AT_HEADS * HEAD_DIM).astype(h.dtype) @ w_out


def _count(kind):
    return len(range(kind, DEPTH, N_MIXERS))


def setup_inputs(seed: int = 0) -> dict:
    key = jax.random.key(seed)
    ks = iter(jax.random.split(key, 48))

    def nrm(shape, scale):
        return scale * jax.random.normal(next(ks), shape, jnp.float32)

    na, nb, nc, nd = _count(0), _count(1), _count(2), _count(3)
    D, F, G, P, HG, E = D_MODEL, D_FF, S5_GROUPS, S5_STATE, S5_GROUP, GM_E
    return {
        "x": nrm((BATCH, SEQ, D), 1.0),
        "norm_pre": 1.0 + nrm((DEPTH, 3, D), 0.05),
        "norm_post": 1.0 + nrm((DEPTH, 3, D), 0.05),
        "ffn_w1": nrm((DEPTH, 2, D, F), D ** -0.5),
        "ffn_w3": nrm((DEPTH, 2, D, F), D ** -0.5),
        "ffn_w2": nrm((DEPTH, 2, F, D), F ** -0.5),
        "rel_bias": nrm((NUM_BUCKETS, N_PATTERNS * AT_HEADS), 0.5),
        "s5_w_in": nrm((na, D, D), D ** -0.5),
        "s5_a_re": -0.5 + nrm((na, G, P), 0.01),
        "s5_a_im": jnp.pi * jnp.arange(P, dtype=jnp.float32) + nrm((na, G, P), 0.01),
        "s5_log_dt": jax.random.uniform(next(ks), (na, G), jnp.float32, math.log(DT_MIN), math.log(DT_MAX)),
        "s5_b_re": nrm((na, G, P, HG), (2 * HG) ** -0.5),
        "s5_b_im": nrm((na, G, P, HG), (2 * HG) ** -0.5),
        "s5_c_re": nrm((na, G, HG, P), (2 * P) ** -0.5),
        "s5_c_im": nrm((na, G, HG, P), (2 * P) ** -0.5),
        "s5_d": nrm((na, D), 1.0),
        "s5_w_glu": nrm((na, D, D), D ** -0.5),
        "s5_b_glu": nrm((na, D), 0.01),
        "s5_w_out": nrm((na, D, D), D ** -0.5),
        "cv_w_in": nrm((nb, D, 2 * D), D ** -0.5),
        "cv_b_in": nrm((nb, 2 * D), 0.01),
        "cv_dw": nrm((nb, CONV_W, D), CONV_W ** -0.5),
        "cv_dw_b": nrm((nb, D), 0.01),
        "cv_ln_g": 1.0 + nrm((nb, D), 0.05),
        "cv_ln_b": nrm((nb, D), 0.01),
        "cv_w_out": nrm((nb, D, D), D ** -0.5),
        "cv_b_out": nrm((nb, D), 0.01),
        "gm_w_in": nrm((nc, D, 2 * E), D ** -0.5),
        "gm_b_in": nrm((nc, 2 * E), 0.01),
        "gm_ln_g": 1.0 + nrm((nc, E), 0.05),
        "gm_ln_b": nrm((nc, E), 0.01),
        "gm_w_s": nrm((nc, GM_HEADS, GM_CHUNK, GM_CHUNK), GM_CHUNK ** -0.5),
        "gm_b_s": 1.0 + nrm((nc, GM_HEADS, GM_CHUNK), 0.01),
        "gm_w_out": nrm((nc, E, D), E ** -0.5),
        "gm_b_out": nrm((nc, D), 0.01),
        "at_w_qkv": nrm((nd, D, N_PATTERNS * 3 * AT_HEADS * HEAD_DIM), D ** -0.5),
        "at_w_out": nrm((nd, AT_HEADS * HEAD_DIM, D), (AT_HEADS * HEAD_DIM) ** -0.5),
    }


def reference(x, norm_pre, norm_post, ffn_w1, ffn_w3, ffn_w2, rel_bias,
              s5_w_in, s5_a_re, s5_a_im, s5_log_dt, s5_b_re, s5_b_im, s5_c_re, s5_c_im,
              s5_d, s5_w_glu, s5_b_glu, s5_w_out,
              cv_w_in, cv_b_in, cv_dw, cv_dw_b, cv_ln_g, cv_ln_b, cv_w_out, cv_b_out,
              gm_w_in, gm_b_in, gm_ln_g, gm_ln_b, gm_w_s, gm_b_s, gm_w_out, gm_b_out,
              at_w_qkv, at_w_out):
    for i in range(DEPTH):
        kind, j = i % N_MIXERS, i // N_MIXERS
        h = swiglu(rmsnorm(x, norm_pre[i, 0]), ffn_w1[i, 0], ffn_w3[i, 0], ffn_w2[i, 0])
        x = x + 0.5 * rmsnorm(h, norm_post[i, 0])
        h = rmsnorm(x, norm_pre[i, 1])
        if kind == 0:
            h = s5_mixer(h, s5_w_in[j], s5_a_re[j], s5_a_im[j], s5_log_dt[j], s5_b_re[j], s5_b_im[j],
                         s5_c_re[j], s5_c_im[j], s5_d[j], s5_w_glu[j], s5_b_glu[j], s5_w_out[j])
        elif kind == 1:
            h = conv_mixer(h, cv_w_in[j], cv_b_in[j], cv_dw[j], cv_dw_b[j], cv_ln_g[j], cv_ln_b[j],
                           cv_w_out[j], cv_b_out[j])
        elif kind == 2:
            h = gmlp_mixer(h, gm_w_in[j], gm_b_in[j], gm_ln_g[j], gm_ln_b[j], gm_w_s[j], gm_b_s[j],
                           gm_w_out[j], gm_b_out[j])
        else:
            h = attention_mixer(h, at_w_qkv[j], at_w_out[j], rel_bias)
        x = x + rmsnorm(h, norm_post[i, 1])
        h = swiglu(rmsnorm(x, norm_pre[i, 2]), ffn_w1[i, 1], ffn_w3[i, 1], ffn_w2[i, 1])
        x = x + 0.5 * rmsnorm(h, norm_post[i, 2])
    return x
```

```python
import functools
import math

import numpy as np
import jax
import jax.numpy as jnp
from jax import lax
from jax.experimental import pallas as pl
from jax.experimental.pallas import tpu as pltpu

F32 = jnp.float32
BF16 = jnp.bfloat16

EPS = 1e-6
S5_GROUP = 16
S5_STATE = 64
S5_GROUPS_PER_TILE = 8
CONV_W = 31
CONV_HALO = 32
GM_CHUNK = 128
GM_HEADS = 8
HEAD_DIM = 64
PATTERNS = ((128, 1), (512, 4), (2048, 16))
BLOCK = 128
NUM_BUCKETS = 32
MAX_DISTANCE = 2048
NEG_BIG = -1e30

VMEM_LIMIT_BYTES = 56 * 1024 * 1024


def _params(*sem):
    return pltpu.CompilerParams(dimension_semantics=sem, vmem_limit_bytes=VMEM_LIMIT_BYTES)


def _rms(x, g):
    return x * lax.rsqrt(jnp.mean(x * x, axis=-1, keepdims=True) + EPS) * g


def _layernorm(x, g, b):
    mu = jnp.mean(x, axis=-1, keepdims=True)
    xc = x - mu
    var = jnp.mean(xc * xc, axis=-1, keepdims=True)
    return xc * lax.rsqrt(var + EPS) * g + b


def _dot(a, b):
    return jnp.dot(a, b, preferred_element_type=F32)


def _row(v):
    return v.reshape(1, -1)


def _ffn_body(x_ref, gpre_ref, gpost_ref, w1_ref, w3_ref, w2_ref, o_ref, xn_scr, acc_scr):
    j = pl.program_id(1)

    @pl.when(j == 0)
    def _():
        xn_scr[...] = _rms(x_ref[...], gpre_ref[...]).astype(BF16)
        acc_scr[...] = jnp.zeros_like(acc_scr)

    xn = xn_scr[...]
    h1 = _dot(xn, w1_ref[...])
    h3 = _dot(xn, w3_ref[...])
    g = (h1 * jax.nn.sigmoid(h1)) * h3
    acc_scr[...] += _dot(g.astype(BF16), w2_ref[...])

    @pl.when(j == pl.num_programs(1) - 1)
    def _():
        o_ref[...] = x_ref[...] + 0.5 * _rms(acc_scr[...], gpost_ref[...])


def _ffn(x, gpre, gpost, w1, w3, w2, tm=512, tf=1408):
    L, D = x.shape
    F = w1.shape[1]
    return pl.pallas_call(
        _ffn_body,
        grid=(L // tm, F // tf),
        in_specs=[
            pl.BlockSpec((tm, D), lambda i, j: (i, 0)),
            pl.BlockSpec((1, D), lambda i, j: (0, 0)),
            pl.BlockSpec((1, D), lambda i, j: (0, 0)),
            pl.BlockSpec((D, tf), lambda i, j: (0, j)),
            pl.BlockSpec((D, tf), lambda i, j: (0, j)),
            pl.BlockSpec((tf, D), lambda i, j: (j, 0)),
        ],
        out_specs=pl.BlockSpec((tm, D), lambda i, j: (i, 0)),
        out_shape=jax.ShapeDtypeStruct((L, D), F32),
        scratch_shapes=[pltpu.VMEM((tm, D), BF16), pltpu.VMEM((tm, D), F32)],
        compiler_params=_params("parallel", "arbitrary"),
        name="ffn",
    )(x, _row(gpre), _row(gpost), w1.astype(BF16), w3.astype(BF16), w2.astype(BF16))


def _s5_body(x_ref, gpre_ref, gpost_ref, win_ref, are_ref, aim_ref, ldt_ref, bdre_ref, bdim_ref,
             cdre_ref, cdim_ref, d_ref, wglu_ref, bglu_ref, wout_ref, o_ref,
             abar_re, abar_im, bb_re, bb_im, st_re, st_im, bu_re, bu_im, y_scr, *, lane_block):
    tt = x_ref.shape[0]
    n_tiles = bdre_ref.shape[0]
    uw = bdre_ref.shape[1]
    sw = bdre_ref.shape[2]
    n_state = n_tiles * sw

    @pl.when(pl.program_id(0) == 0)
    def _():
        ar = are_ref[...]
        ai = aim_ref[...]
        dt = jnp.exp(ldt_ref[...])
        mag = jnp.exp(dt * ar)
        abr = mag * jnp.cos(dt * ai)
        abi = mag * jnp.sin(dt * ai)
        den = ar * ar + ai * ai
        nr = abr - 1.0
        fre = (nr * ar + abi * ai) / den
        fim = (abi * ar - nr * ai) / den
        abar_re[...] = abr
        abar_im[...] = abi
        for k in range(n_tiles):
            fr = fre[:, k * sw:(k + 1) * sw]
            fi = fim[:, k * sw:(k + 1) * sw]
            bb_re[k] = (fr * bdre_ref[k] - fi * bdim_ref[k]).astype(BF16)
            bb_im[k] = (fr * bdim_ref[k] + fi * bdre_ref[k]).astype(BF16)
        st_re[...] = jnp.zeros_like(st_re)
        st_im[...] = jnp.zeros_like(st_im)

    x = x_ref[...]
    xn = _rms(x, gpre_ref[...]).astype(BF16)
    u = _dot(xn, win_ref[...])
    ub = u.astype(BF16)
    for k in range(n_tiles):
        uk = ub[:, k * uw:(k + 1) * uw]
        bu_re[:, k * sw:(k + 1) * sw] = _dot(uk, bb_re[k])
        bu_im[:, k * sw:(k + 1) * sw] = _dot(uk, bb_im[k])

    for lb in range(n_state // lane_block):
        sl = slice(lb * lane_block, (lb + 1) * lane_block)
        a_r = abar_re[:, sl]
        a_i = abar_im[:, sl]

        def step(t, carry, sl=sl, a_r=a_r, a_i=a_i):
            sr, si = carry
            nr = a_r * sr - a_i * si + bu_re[pl.ds(t, 1), sl]
            ni = a_r * si + a_i * sr + bu_im[pl.ds(t, 1), sl]
            bu_re[pl.ds(t, 1), sl] = nr
            bu_im[pl.ds(t, 1), sl] = ni
            return nr, ni

        sr, si = lax.fori_loop(0, tt, step, (st_re[:, sl], st_im[:, sl]), unroll=8)
        st_re[:, sl] = sr
        st_im[:, sl] = si

    for k in range(n_tiles):
        sre = bu_re[:, k * sw:(k + 1) * sw].astype(BF16)
        sim = bu_im[:, k * sw:(k + 1) * sw].astype(BF16)
        y_scr[:, k * uw:(k + 1) * uw] = _dot(sre, cdre_ref[k]) - _dot(sim, cdim_ref[k])
    y = jax.nn.gelu(y_scr[...] + d_ref[...] * u)
    z = y * jax.nn.sigmoid(_dot(y.astype(BF16), wglu_ref[...]) + bglu_ref[...])
    h = _dot(z.astype(BF16), wout_ref[...])
    o_ref[...] = x + _rms(h, gpost_ref[...])


def _s5_block_diag(b, c):
    G, P, HG = b.shape
    gt = S5_GROUPS_PER_TILE
    eye = jnp.eye(gt, dtype=b.dtype)
    bt = b.reshape(G // gt, gt, P, HG).transpose(0, 1, 3, 2)
    bd = jnp.einsum('kghp,gj->kghjp', bt, eye).reshape(G // gt, gt * HG, gt * P)
    ct = c.reshape(G // gt, gt, HG, P)
    cd = jnp.einsum('kghp,gj->kjpgh', ct, eye).reshape(G // gt, gt * P, gt * HG)
    return bd, cd


def _s5_mixer(x, gpre, gpost, w_in, a_re, a_im, log_dt, b_re, b_im, c_re, c_im, d_skip, w_glu, b_glu,
              w_out, tt=256, lane_block=1024):
    L, D = x.shape
    G, P = a_re.shape
    n_state = G * P
    bd_re, cd_re = _s5_block_diag(b_re, c_re)
    bd_im, cd_im = _s5_block_diag(b_im, c_im)
    n_tiles, uw, sw = bd_re.shape
    full2 = lambda shape: pl.BlockSpec(shape, lambda i: (0, 0))
    full3 = lambda shape: pl.BlockSpec(shape, lambda i: (0, 0, 0))
    return pl.pallas_call(
        functools.partial(_s5_body, lane_block=lane_block),
        grid=(L // tt,),
        in_specs=[
            pl.BlockSpec((tt, D), lambda i: (i, 0)),
            full2((1, D)), full2((1, D)), full2((D, D)),
            full2((1, n_state)), full2((1, n_state)), full2((1, n_state)),
            full3((n_tiles, uw, sw)), full3((n_tiles, uw, sw)),
            full3((n_tiles, sw, uw)), full3((n_tiles, sw, uw)),
            full2((1, D)), full2((D, D)), full2((1, D)), full2((D, D)),
        ],
        out_specs=pl.BlockSpec((tt, D), lambda i: (i, 0)),
        out_shape=jax.ShapeDtypeStruct((L, D), F32),
        scratch_shapes=[
            pltpu.VMEM((1, n_state), F32), pltpu.VMEM((1, n_state), F32),
            pltpu.VMEM((n_tiles, uw, sw), BF16), pltpu.VMEM((n_tiles, uw, sw), BF16),
            pltpu.VMEM((1, n_state), F32), pltpu.VMEM((1, n_state), F32),
            pltpu.VMEM((tt, n_state), F32), pltpu.VMEM((tt, n_state), F32),
            pltpu.VMEM((tt, D), F32),
        ],
        compiler_params=_params("arbitrary"),
        name="s5_mixer",
    )(x, _row(gpre), _row(gpost), w_in.astype(BF16),
      _row(a_re), _row(a_im), _row(jnp.repeat(log_dt, P)),
      bd_re, bd_im, cd_re.astype(BF16), cd_im.astype(BF16),
      _row(d_skip), w_glu.astype(BF16), _row(b_glu), w_out.astype(BF16))


def _conv_body(x_ref, gpre_ref, gpost_ref, wa_ref, wg_ref, ba_ref, bg_ref, dw_ref, dwb_ref,
               lng_ref, lnb_ref, wout_ref, bout_ref, o_ref, ext, conv_scr):
    tm, D = x_ref.shape
    halo = ext.shape[0] - tm
    n_taps = dw_ref.shape[0]

    @pl.when(pl.program_id(0) == 0)
    def _():
        ext[0:halo, :] = jnp.zeros((halo, D), F32)

    x = x_ref[...]
    xn = _rms(x, gpre_ref[...]).astype(BF16)
    a = _dot(xn, wa_ref[...]) + ba_ref[...]
    g = _dot(xn, wg_ref[...]) + bg_ref[...]
    ext[halo:halo + tm, :] = a * jax.nn.sigmoid(g)

    lane = 128
    for c in range(D // lane):
        cs = slice(c * lane, (c + 1) * lane)
        acc = jnp.zeros((tm, lane), F32)
        for k in range(n_taps):
            acc = acc + dw_ref[k:k + 1, cs] * ext[halo - (n_taps - 1) + k:halo - (n_taps - 1) + k + tm, cs]
        conv_scr[:, cs] = acc
    ext[0:halo, :] = ext[tm:tm + halo, :]

    y = _layernorm(conv_scr[...] + dwb_ref[...], lng_ref[...], lnb_ref[...])
    y = y * jax.nn.sigmoid(y)
    h = _dot(y.astype(BF16), wout_ref[...]) + bout_ref[...]
    o_ref[...] = x + _rms(h, gpost_ref[...])


def _conv_mixer(x, gpre, gpost, w_in, b_in, dw, dw_b, ln_g, ln_b, w_out, b_out, tm=256):
    L, D = x.shape
    full2 = lambda shape: pl.BlockSpec(shape, lambda i: (0, 0))
    return pl.pallas_call(
        _conv_body,
        grid=(L // tm,),
        in_specs=[
            pl.BlockSpec((tm, D), lambda i: (i, 0)),
            full2((1, D)), full2((1, D)), full2((D, D)), full2((D, D)), full2((1, D)), full2((1, D)),
            full2(dw.shape), full2((1, D)), full2((1, D)), full2((1, D)), full2((D, D)), full2((1, D)),
        ],
        out_specs=pl.BlockSpec((tm, D), lambda i: (i, 0)),
        out_shape=jax.ShapeDtypeStruct((L, D), F32),
        scratch_shapes=[pltpu.VMEM((tm + CONV_HALO, D), F32), pltpu.VMEM((tm, D), F32)],
        compiler_params=_params("arbitrary"),
        name="conv_mixer",
    )(x, _row(gpre), _row(gpost), w_in[:, :D].astype(BF16), w_in[:, D:].astype(BF16),
      _row(b_in[:D]), _row(b_in[D:]), dw, _row(dw_b), _row(ln_g), _row(ln_b),
      w_out.astype(BF16), _row(b_out))


def _gmlp_body(x_ref, gpre_ref, gpost_ref, win_ref, bin_ref, lng_ref, lnb_ref, ws_ref, bs_ref,
               wout_ref, bout_ref, o_ref, su_scr):
    tm = x_ref.shape[0]
    E = lng_ref.shape[1]
    n_heads, chunk, _ = ws_ref.shape
    hw = E // n_heads
    x = x_ref[...]
    xn = _rms(x, gpre_ref[...]).astype(BF16)
    z = jax.nn.gelu(_dot(xn, win_ref[...]) + bin_ref[...])
    u = z[:, :E]
    vb = _layernorm(z[:, E:], lng_ref[...], lnb_ref[...]).astype(BF16)
    causal = (lax.broadcasted_iota(jnp.int32, (chunk, chunk), 1)
              <= lax.broadcasted_iota(jnp.int32, (chunk, chunk), 0))
    for h in range(n_heads):
        wm = jnp.where(causal, ws_ref[h], 0.0).astype(BF16)
        bias = bs_ref[:, h:h + 1]
        for c in range(tm // chunk):
            rs = slice(c * chunk, (c + 1) * chunk)
            hs = slice(h * hw, (h + 1) * hw)
            s = _dot(wm, vb[rs, hs]) + bias
            su_scr[rs, hs] = (u[rs, hs] * s).astype(BF16)
    hout = _dot(su_scr[...], wout_ref[...]) + bout_ref[...]
    o_ref[...] = x + _rms(hout, gpost_ref[...])


def _gmlp_mixer(x, gpre, gpost, w_in, b_in, ln_g, ln_b, w_s, b_s, w_out, b_out, tm=256):
    L, D = x.shape
    E = ln_g.shape[0]
    full2 = lambda shape: pl.BlockSpec(shape, lambda i: (0, 0))
    return pl.pallas_call(
        _gmlp_body,
        grid=(L // tm,),
        in_specs=[
            pl.BlockSpec((tm, D), lambda i: (i, 0)),
            full2((1, D)), full2((1, D)), full2((D, 2 * E)), full2((1, 2 * E)), full2((1, E)), full2((1, E)),
            pl.BlockSpec(w_s.shape, lambda i: (0, 0, 0)), full2((GM_CHUNK, GM_HEADS)),
            full2((E, D)), full2((1, D)),
        ],
        out_specs=pl.BlockSpec((tm, D), lambda i: (i, 0)),
        out_shape=jax.ShapeDtypeStruct((L, D), F32),
        scratch_shapes=[pltpu.VMEM((tm, E), BF16)],
        compiler_params=_params("parallel"),
        name="gmlp_mixer",
    )(x, _row(gpre), _row(gpost), w_in.astype(BF16), _row(b_in), _row(ln_g), _row(ln_b),
      w_s, b_s.T, w_out.astype(BF16), _row(b_out))


def _nmm_body(x_ref, g_ref, w_ref, o_ref, xn_scr):
    @pl.when(pl.program_id(1) == 0)
    def _():
        xn_scr[...] = _rms(x_ref[...], g_ref[...]).astype(BF16)

    o_ref[...] = _dot(xn_scr[...], w_ref[...]).astype(o_ref.dtype)


def _norm_matmul(x, g, w, out_dtype, tm=1024, tn=1536):
    L, D = x.shape
    N = w.shape[1]
    return pl.pallas_call(
        _nmm_body,
        grid=(L // tm, N // tn),
        in_specs=[
            pl.BlockSpec((tm, D), lambda i, j: (i, 0)),
            pl.BlockSpec((1, D), lambda i, j: (0, 0)),
            pl.BlockSpec((D, tn), lambda i, j: (0, j)),
        ],
        out_specs=pl.BlockSpec((tm, tn), lambda i, j: (i, j)),
        out_shape=jax.ShapeDtypeStruct((L, N), out_dtype),
        scratch_shapes=[pltpu.VMEM((tm, D), BF16)],
        compiler_params=_params("parallel", "arbitrary"),
        name="norm_matmul",
    )(x, _row(g), w.astype(BF16))


def _t5_buckets(dilation):
    delta = (np.arange(BLOCK)[:, None] + BLOCK) - np.arange(2 * BLOCK)[None, :]
    dist = np.maximum(delta, 0) * dilation
    max_exact = NUM_BUCKETS // 2
    distf = np.maximum(dist, 1).astype(np.float32)
    large = max_exact + (np.log(distf / np.float32(max_exact)) / np.float32(math.log(MAX_DISTANCE / max_exact))
                         * np.float32(NUM_BUCKETS - max_exact)).astype(np.int32)
    large = np.minimum(large, NUM_BUCKETS - 1)
    return np.where(dist < max_exact, dist, large).astype(np.int32)


def _attn_body(tab_ref, bkt_ref, q_ref, kp_ref, kc_ref, vp_ref, vc_ref, o_ref, lse_ref, bias_scr, *, scale):
    blk = q_ref.shape[0]
    n_heads = bias_scr.shape[0]
    hd = q_ref.shape[1] // n_heads
    n = pl.program_id(1)

    @pl.when((pl.program_id(0) == 0) & (n == 0))
    def _():
        bkt = bkt_ref[...]
        for h in range(n_heads):
            acc = jnp.zeros(bkt.shape, F32)
            for b in range(tab_ref.shape[0]):
                acc = jnp.where(bkt == b, tab_ref[b, h], acc)
            bias_scr[h] = acc

    row = lax.broadcasted_iota(jnp.int32, (blk, blk), 0)
    col = lax.broadcasted_iota(jnp.int32, (blk, blk), 1)
    mask_c = col <= row
    mask_p = (col >= row) & (n > 0)
    dims = (((1,), (1,)), ((), ()))
    for h in range(n_heads):
        hs = slice(h * hd, (h + 1) * hd)
        qh = q_ref[:, hs]
        lp = lax.dot_general(qh, kp_ref[:, hs], dims, preferred_element_type=F32) * scale + bias_scr[h, :, 0:blk]
        lc = lax.dot_general(qh, kc_ref[:, hs], dims, preferred_element_type=F32) * scale + bias_scr[h, :, blk:2 * blk]
        lp = jnp.where(mask_p, lp, NEG_BIG)
        lc = jnp.where(mask_c, lc, NEG_BIG)
        m = jnp.maximum(jnp.max(lp, axis=-1, keepdims=True), jnp.max(lc, axis=-1, keepdims=True))
        pp = jnp.exp(lp - m)
        pc = jnp.exp(lc - m)
        den = jnp.sum(pp, axis=-1, keepdims=True) + jnp.sum(pc, axis=-1, keepdims=True)
        oh = _dot(pp.astype(BF16), vp_ref[:, hs]) + _dot(pc.astype(BF16), vc_ref[:, hs])
        o_ref[:, hs] = oh / den
        lse_ref[:, hs] = jnp.broadcast_to(m + jnp.log(den), (blk, hd))


def _dilated_attention(qkv, table, g, dilation):
    L, W = qkv.shape
    D = W // (3 * len(PATTERNS))
    n_heads = D // HEAD_DIM
    n_blk = L // dilation // BLOCK
    wb = W // D
    view = qkv.reshape(L // dilation, dilation * W)
    col = lambda c: (lambda r, n: (n, r * wb + 3 * g + c))
    colp = lambda c: (lambda r, n: (jnp.maximum(n - 1, 0), r * wb + 3 * g + c))
    blk = lambda imap: pl.BlockSpec((BLOCK, D), imap)
    o, lse = pl.pallas_call(
        functools.partial(_attn_body, scale=HEAD_DIM ** -0.5),
        grid=(dilation, n_blk),
        in_specs=[
            pl.BlockSpec(memory_space=pltpu.SMEM),
            pl.BlockSpec((BLOCK, 2 * BLOCK), lambda r, n: (0, 0)),
            blk(col(0)), blk(colp(1)), blk(col(1)), blk(colp(2)), blk(col(2)),
        ],
        out_specs=[pl.BlockSpec((BLOCK, D), lambda r, n: (n, r))] * 2,
        out_shape=[jax.ShapeDtypeStruct((L // dilation, dilation * D), F32)] * 2,
        scratch_shapes=[pltpu.VMEM((n_heads, BLOCK, 2 * BLOCK), F32)],
        compiler_params=_params("arbitrary", "arbitrary"),
        name=f"dilated_attn_d{dilation}",
    )(table, jnp.asarray(_t5_buckets(dilation)), view, view, view, view, view)
    return o.reshape(L, D), lse.reshape(L, D)


def _attn_out_body(x_ref, gpost_ref, o0_ref, o1_ref, o2_ref, l0_ref, l1_ref, l2_ref, wout_ref, out_ref):
    l0, l1, l2 = l0_ref[...], l1_ref[...], l2_ref[...]
    m = jnp.maximum(jnp.maximum(l0, l1), l2)
    e0, e1, e2 = jnp.exp(l0 - m), jnp.exp(l1 - m), jnp.exp(l2 - m)
    o = (e0 * o0_ref[...] + e1 * o1_ref[...] + e2 * o2_ref[...]) / (e0 + e1 + e2)
    h = _dot(o.astype(BF16), wout_ref[...])
    out_ref[...] = x_ref[...] + _rms(h, gpost_ref[...])


def _attention_mixer(x, gpre, gpost, w_qkv, w_out, rel_bias, tm=512):
    L, D = x.shape
    n_heads = D // HEAD_DIM
    qkv = _norm_matmul(x, gpre, w_qkv, BF16)
    outs, lses = [], []
    for g, (window, dilation) in enumerate(PATTERNS):
        assert window // dilation == BLOCK and L % (BLOCK * dilation) == 0
        o, lse = _dilated_attention(qkv, rel_bias[:, g * n_heads:(g + 1) * n_heads], g, dilation)
        outs.append(o)
        lses.append(lse)
    tile = pl.BlockSpec((tm, D), lambda i: (i, 0))
    return pl.pallas_call(
        _attn_out_body,
        grid=(L // tm,),
        in_specs=[tile, pl.BlockSpec((1, D), lambda i: (0, 0))] + [tile] * 6
                 + [pl.BlockSpec((D, D), lambda i: (0, 0))],
        out_specs=tile,
        out_shape=jax.ShapeDtypeStruct((L, D), F32),
        compiler_params=_params("parallel"),
        name="attn_out",
    )(x, _row(gpost), *outs, *lses, w_out.astype(BF16))


def kernel(x, norm_pre, norm_post, ffn_w1, ffn_w3, ffn_w2, rel_bias, s5_w_in, s5_a_re, s5_a_im, s5_log_dt, s5_b_re, s5_b_im, s5_c_re, s5_c_im, s5_d, s5_w_glu, s5_b_glu, s5_w_out, cv_w_in, cv_b_in, cv_dw, cv_dw_b, cv_ln_g, cv_ln_b, cv_w_out, cv_b_out, gm_w_in, gm_b_in, gm_ln_g, gm_ln_b, gm_w_s, gm_b_s, gm_w_out, gm_b_out, at_w_qkv, at_w_out):
    bsz, seq, d_model = x.shape
    assert bsz == 1, "sequence mixers carry state along the row axis of one sequence"
    depth = norm_pre.shape[0]
    n_mixers = 4
    h = x.reshape(seq, d_model)
    for i in range(depth):
        kind, j = i % n_mixers, i // n_mixers
        h = _ffn(h, norm_pre[i, 0], norm_post[i, 0], ffn_w1[i, 0], ffn_w3[i, 0], ffn_w2[i, 0])
        if kind == 0:
            h = _s5_mixer(h, norm_pre[i, 1], norm_post[i, 1], s5_w_in[j], s5_a_re[j], s5_a_im[j], s5_log_dt[j],
                          s5_b_re[j], s5_b_im[j], s5_c_re[j], s5_c_im[j], s5_d[j], s5_w_glu[j], s5_b_glu[j],
                          s5_w_out[j])
        elif kind == 1:
            h = _conv_mixer(h, norm_pre[i, 1], norm_post[i, 1], cv_w_in[j], cv_b_in[j], cv_dw[j], cv_dw_b[j],
                            cv_ln_g[j], cv_ln_b[j], cv_w_out[j], cv_b_out[j])
        elif kind == 2:
            h = _gmlp_mixer(h, norm_pre[i, 1], norm_post[i, 1], gm_w_in[j], gm_b_in[j], gm_ln_g[j], gm_ln_b[j],
                            gm_w_s[j], gm_b_s[j], gm_w_out[j], gm_b_out[j])
        else:
            h = _attention_mixer(h, norm_pre[i, 1], norm_post[i, 1], at_w_qkv[j], at_w_out[j], rel_bias)
        h = _ffn(h, norm_pre[i, 2], norm_post[i, 2], ffn_w1[i, 1], ffn_w3[i, 1], ffn_w2[i, 1])
    return h.reshape(bsz, seq, d_model)
```

```python
import functools
import math

import numpy as np
import jax
import jax.numpy as jnp
from jax import lax
from jax.experimental import pallas as pl
from jax.experimental.pallas import tpu as pltpu

F32 = jnp.float32
BF16 = jnp.bfloat16

EPS = 1e-6
LANES = 128
S5_GROUP = 16
S5_STATE = 64
S5_GROUPS_PER_TILE = 8
CONV_W = 31
CONV_HALO = 32
GM_CHUNK = 128
GM_HEADS = 8
HEAD_DIM = 64
PATTERNS = ((128, 1), (512, 4), (2048, 16))
BLOCK = 128
NUM_BUCKETS = 32
MAX_DISTANCE = 2048
NEG_BIG = -1e30

VMEM_LIMIT_BYTES = 56 * 1024 * 1024


def _params(*sem):
    return pltpu.CompilerParams(dimension_semantics=sem, vmem_limit_bytes=VMEM_LIMIT_BYTES)


def _rms(x, g):
    return x * lax.rsqrt(jnp.mean(x * x, axis=-1, keepdims=True) + EPS) * g


def _layernorm(x, g, b):
    mu = jnp.mean(x, axis=-1, keepdims=True)
    xc = x - mu
    var = jnp.mean(xc * xc, axis=-1, keepdims=True)
    return xc * lax.rsqrt(var + EPS) * g + b


def _dot(a, b):
    return jnp.dot(a, b, preferred_element_type=F32)


def _row(v):
    return v.reshape(1, -1)


def _ffn_body(x_ref, gpre_ref, gpost_ref, w1_ref, w3_ref, w2_ref, o_ref, xn_scr, acc_scr):
    j = pl.program_id(1)

    @pl.when(j == 0)
    def _():
        xn_scr[...] = _rms(x_ref[...], gpre_ref[...]).astype(BF16)
        acc_scr[...] = jnp.zeros_like(acc_scr)

    xn = xn_scr[...]
    h1 = _dot(xn, w1_ref[...])
    h3 = _dot(xn, w3_ref[...])
    g = (h1 * jax.nn.sigmoid(h1)) * h3
    acc_scr[...] += _dot(g.astype(BF16), w2_ref[...])

    @pl.when(j == pl.num_programs(1) - 1)
    def _():
        o_ref[...] = x_ref[...] + 0.5 * _rms(acc_scr[...], gpost_ref[...])


def _ffn(x, gpre, gpost, w1, w3, w2, tm=512, tf=1408):
    L, D = x.shape
    F = w1.shape[1]
    return pl.pallas_call(
        _ffn_body,
        grid=(L // tm, F // tf),
        in_specs=[
            pl.BlockSpec((tm, D), lambda i, j: (i, 0)),
            pl.BlockSpec((1, D), lambda i, j: (0, 0)),
            pl.BlockSpec((1, D), lambda i, j: (0, 0)),
            pl.BlockSpec((D, tf), lambda i, j: (0, j)),
            pl.BlockSpec((D, tf), lambda i, j: (0, j)),
            pl.BlockSpec((tf, D), lambda i, j: (j, 0)),
        ],
        out_specs=pl.BlockSpec((tm, D), lambda i, j: (i, 0)),
        out_shape=jax.ShapeDtypeStruct((L, D), F32),
        scratch_shapes=[pltpu.VMEM((tm, D), BF16), pltpu.VMEM((tm, D), F32)],
        compiler_params=_params("parallel", "arbitrary"),
        name="ffn",
    )(x, _row(gpre), _row(gpost), w1.astype(BF16), w3.astype(BF16), w2.astype(BF16))


def _s5_body(x_ref, gpre_ref, gpost_ref, win_ref, are_ref, aim_ref, ldt_ref, bdre_ref, bdim_ref,
             cdre_ref, cdim_ref, d_ref, wglu_ref, bglu_ref, wout_ref, o_ref,
             abar_re, abar_im, bb_re, bb_im, st_re, st_im, bu_re, bu_im, y_scr, *, lane_block):
    tt = x_ref.shape[0]
    n_tiles = bdre_ref.shape[0]
    uw = bdre_ref.shape[1]
    sw = bdre_ref.shape[2]
    n_state = n_tiles * sw

    @pl.when(pl.program_id(0) == 0)
    def _():
        ar = are_ref[...]
        ai = aim_ref[...]
        dt = jnp.exp(ldt_ref[...])
        mag = jnp.exp(dt * ar)
        abr = mag * jnp.cos(dt * ai)
        abi = mag * jnp.sin(dt * ai)
        den = ar * ar + ai * ai
        nr = abr - 1.0
        fre = (nr * ar + abi * ai) / den
        fim = (abi * ar - nr * ai) / den
        abar_re[...] = abr
        abar_im[...] = abi
        for k in range(n_tiles):
            fr = fre[:, k * sw:(k + 1) * sw]
            fi = fim[:, k * sw:(k + 1) * sw]
            bb_re[k] = (fr * bdre_ref[k] - fi * bdim_ref[k]).astype(BF16)
            bb_im[k] = (fr * bdim_ref[k] + fi * bdre_ref[k]).astype(BF16)
        st_re[...] = jnp.zeros_like(st_re)
        st_im[...] = jnp.zeros_like(st_im)

    x = x_ref[...]
    xn = _rms(x, gpre_ref[...]).astype(BF16)
    u = _dot(xn, win_ref[...])
    ub = u.astype(BF16)
    for k in range(n_tiles):
        uk = ub[:, k * uw:(k + 1) * uw]
        bu_re[:, k * sw:(k + 1) * sw] = _dot(uk, bb_re[k])
        bu_im[:, k * sw:(k + 1) * sw] = _dot(uk, bb_im[k])

    for lb in range(n_state // lane_block):
        sl = slice(lb * lane_block, (lb + 1) * lane_block)
        a_r = abar_re[:, sl]
        a_i = abar_im[:, sl]

        def step(t, carry, sl=sl, a_r=a_r, a_i=a_i):
            sr, si = carry
            nr = a_r * sr - a_i * si + bu_re[pl.ds(t, 1), sl]
            ni = a_r * si + a_i * sr + bu_im[pl.ds(t, 1), sl]
            bu_re[pl.ds(t, 1), sl] = nr
            bu_im[pl.ds(t, 1), sl] = ni
            return nr, ni

        sr, si = lax.fori_loop(0, tt, step, (st_re[:, sl], st_im[:, sl]), unroll=8)
        st_re[:, sl] = sr
        st_im[:, sl] = si

    for k in range(n_tiles):
        sre = bu_re[:, k * sw:(k + 1) * sw].astype(BF16)
        sim = bu_im[:, k * sw:(k + 1) * sw].astype(BF16)
        y_scr[:, k * uw:(k + 1) * uw] = _dot(sre, cdre_ref[k]) - _dot(sim, cdim_ref[k])
    y = jax.nn.gelu(y_scr[...] + d_ref[...] * u)
    z = y * jax.nn.sigmoid(_dot(y.astype(BF16), wglu_ref[...]) + bglu_ref[...])
    h = _dot(z.astype(BF16), wout_ref[...])
    o_ref[...] = x + _rms(h, gpost_ref[...])


def _s5_block_diag(b, c):
    G, P, HG = b.shape
    gt = S5_GROUPS_PER_TILE
    eye = jnp.eye(gt, dtype=b.dtype)
    bt = b.reshape(G // gt, gt, P, HG).transpose(0, 1, 3, 2)
    bd = jnp.einsum('kghp,gj->kghjp', bt, eye).reshape(G // gt, gt * HG, gt * P)
    ct = c.reshape(G // gt, gt, HG, P)
    cd = jnp.einsum('kghp,gj->kjpgh', ct, eye).reshape(G // gt, gt * P, gt * HG)
    return bd, cd


def _s5_mixer(x, gpre, gpost, w_in, a_re, a_im, log_dt, b_re, b_im, c_re, c_im, d_skip, w_glu, b_glu,
              w_out, tt=256, lane_block=1024):
    L, D = x.shape
    G, P = a_re.shape
    n_state = G * P
    bd_re, cd_re = _s5_block_diag(b_re, c_re)
    bd_im, cd_im = _s5_block_diag(b_im, c_im)
    n_tiles, uw, sw = bd_re.shape
    full2 = lambda shape: pl.BlockSpec(shape, lambda i: (0, 0))
    full3 = lambda shape: pl.BlockSpec(shape, lambda i: (0, 0, 0))
    return pl.pallas_call(
        functools.partial(_s5_body, lane_block=lane_block),
        grid=(L // tt,),
        in_specs=[
            pl.BlockSpec((tt, D), lambda i: (i, 0)),
            full2((1, D)), full2((1, D)), full2((D, D)),
            full2((1, n_state)), full2((1, n_state)), full2((1, n_state)),
            full3((n_tiles, uw, sw)), full3((n_tiles, uw, sw)),
            full3((n_tiles, sw, uw)), full3((n_tiles, sw, uw)),
            full2((1, D)), full2((D, D)), full2((1, D)), full2((D, D)),
        ],
        out_specs=pl.BlockSpec((tt, D), lambda i: (i, 0)),
        out_shape=jax.ShapeDtypeStruct((L, D), F32),
        scratch_shapes=[
            pltpu.VMEM((1, n_state), F32), pltpu.VMEM((1, n_state), F32),
            pltpu.VMEM((n_tiles, uw, sw), BF16), pltpu.VMEM((n_tiles, uw, sw), BF16),
            pltpu.VMEM((1, n_state), F32), pltpu.VMEM((1, n_state), F32),
            pltpu.VMEM((tt, n_state), F32), pltpu.VMEM((tt, n_state), F32),
            pltpu.VMEM((tt, D), F32),
        ],
        compiler_params=_params("arbitrary"),
        name="s5_mixer",
    )(x, _row(gpre), _row(gpost), w_in.astype(BF16),
      _row(a_re), _row(a_im), _row(jnp.repeat(log_dt, P)),
      bd_re, bd_im, cd_re.astype(BF16), cd_im.astype(BF16),
      _row(d_skip), w_glu.astype(BF16), _row(b_glu), w_out.astype(BF16))


def _conv_body(x_ref, gpre_ref, gpost_ref, wa_ref, wg_ref, ba_ref, bg_ref, dw_ref, dwb_ref,
               lng_ref, lnb_ref, wout_ref, bout_ref, o_ref, ext, conv_scr):
    tm, D = x_ref.shape
    halo = ext.shape[0] - tm
    n_taps = dw_ref.shape[0]

    @pl.when(pl.program_id(0) == 0)
    def _():
        ext[0:halo, :] = jnp.zeros((halo, D), F32)

    x = x_ref[...]
    xn = _rms(x, gpre_ref[...]).astype(BF16)
    a = _dot(xn, wa_ref[...]) + ba_ref[...]
    g = _dot(xn, wg_ref[...]) + bg_ref[...]
    ext[halo:halo + tm, :] = a * jax.nn.sigmoid(g)

    lane = 128
    for c in range(D // lane):
        cs = slice(c * lane, (c + 1) * lane)
        acc = jnp.zeros((tm, lane), F32)
        for k in range(n_taps):
            acc = acc + dw_ref[k:k + 1, cs] * ext[halo - (n_taps - 1) + k:halo - (n_taps - 1) + k + tm, cs]
        conv_scr[:, cs] = acc
    ext[0:halo, :] = ext[tm:tm + halo, :]

    y = _layernorm(conv_scr[...] + dwb_ref[...], lng_ref[...], lnb_ref[...])
    y = y * jax.nn.sigmoid(y)
    h = _dot(y.astype(BF16), wout_ref[...]) + bout_ref[...]
    o_ref[...] = x + _rms(h, gpost_ref[...])


def _conv_mixer(x, gpre, gpost, w_in, b_in, dw, dw_b, ln_g, ln_b, w_out, b_out, tm=256):
    L, D = x.shape
    full2 = lambda shape: pl.BlockSpec(shape, lambda i: (0, 0))
    return pl.pallas_call(
        _conv_body,
        grid=(L // tm,),
        in_specs=[
            pl.BlockSpec((tm, D), lambda i: (i, 0)),
            full2((1, D)), full2((1, D)), full2((D, D)), full2((D, D)), full2((1, D)), full2((1, D)),
            full2(dw.shape), full2((1, D)), full2((1, D)), full2((1, D)), full2((D, D)), full2((1, D)),
        ],
        out_specs=pl.BlockSpec((tm, D), lambda i: (i, 0)),
        out_shape=jax.ShapeDtypeStruct((L, D), F32),
        scratch_shapes=[pltpu.VMEM((tm + CONV_HALO, D), F32), pltpu.VMEM((tm, D), F32)],
        compiler_params=_params("arbitrary"),
        name="conv_mixer",
    )(x, _row(gpre), _row(gpost), w_in[:, :D].astype(BF16), w_in[:, D:].astype(BF16),
      _row(b_in[:D]), _row(b_in[D:]), dw, _row(dw_b), _row(ln_g), _row(ln_b),
      w_out.astype(BF16), _row(b_out))


def _gmlp_body(x_ref, gpre_ref, gpost_ref, win_ref, bin_ref, lng_ref, lnb_ref, ws_ref, bs_ref,
               wout_ref, bout_ref, o_ref, su_scr):
    tm = x_ref.shape[0]
    E = lng_ref.shape[1]
    n_heads, chunk, _ = ws_ref.shape
    hw = E // n_heads
    x = x_ref[...]
    xn = _rms(x, gpre_ref[...]).astype(BF16)
    z = jax.nn.gelu(_dot(xn, win_ref[...]) + bin_ref[...])
    u = z[:, :E]
    vb = _layernorm(z[:, E:], lng_ref[...], lnb_ref[...]).astype(BF16)
    causal = (lax.broadcasted_iota(jnp.int32, (chunk, chunk), 1)
              <= lax.broadcasted_iota(jnp.int32, (chunk, chunk), 0))
    for h in range(n_heads):
        wm = jnp.where(causal, ws_ref[h], 0.0).astype(BF16)
        bias = bs_ref[:, h:h + 1]
        for c in range(tm // chunk):
            rs = slice(c * chunk, (c + 1) * chunk)
            hs = slice(h * hw, (h + 1) * hw)
            s = _dot(wm, vb[rs, hs]) + bias
            su_scr[rs, hs] = (u[rs, hs] * s).astype(BF16)
    hout = _dot(su_scr[...], wout_ref[...]) + bout_ref[...]
    o_ref[...] = x + _rms(hout, gpost_ref[...])


def _gmlp_mixer(x, gpre, gpost, w_in, b_in, ln_g, ln_b, w_s, b_s, w_out, b_out, tm=256):
    L, D = x.shape
    E = ln_g.shape[0]
    full2 = lambda shape: pl.BlockSpec(shape, lambda i: (0, 0))
    return pl.pallas_call(
        _gmlp_body,
        grid=(L // tm,),
        in_specs=[
            pl.BlockSpec((tm, D), lambda i: (i, 0)),
            full2((1, D)), full2((1, D)), full2((D, 2 * E)), full2((1, 2 * E)), full2((1, E)), full2((1, E)),
            pl.BlockSpec(w_s.shape, lambda i: (0, 0, 0)), full2((GM_CHUNK, GM_HEADS)),
            full2((E, D)), full2((1, D)),
        ],
        out_specs=pl.BlockSpec((tm, D), lambda i: (i, 0)),
        out_shape=jax.ShapeDtypeStruct((L, D), F32),
        scratch_shapes=[pltpu.VMEM((tm, E), BF16)],
        compiler_params=_params("parallel"),
        name="gmlp_mixer",
    )(x, _row(gpre), _row(gpost), w_in.astype(BF16), _row(b_in), _row(ln_g), _row(ln_b),
      w_s, b_s.T, w_out.astype(BF16), _row(b_out))


def _qkv_body(x_ref, g_ref, w_ref, o_ref, xt_scr, xs_scr):
    d, rows, _ = o_ref.shape
    if d == 1:
        x = x_ref[...]
    else:
        for c in range(xt_scr.shape[0]):
            cs = slice(c * LANES, (c + 1) * LANES)
            xt_scr[c] = x_ref[:, cs]
            for r in range(d):
                xs_scr[r * rows:(r + 1) * rows, cs] = xt_scr[c, pl.ds(r, rows, stride=d), :]
        x = xs_scr[...]
    res = _dot(_rms(x, g_ref[...]).astype(BF16), w_ref[...])
    for r in range(d):
        o_ref[r] = res[r * rows:(r + 1) * rows, :].astype(o_ref.dtype)


def _qkv_proj(x, g, w, dilation, tm=512):
    L, D = x.shape
    N = w.shape[1]
    return pl.pallas_call(
        _qkv_body,
        grid=(L // tm,),
        in_specs=[
            pl.BlockSpec((tm, D), lambda i: (i, 0)),
            pl.BlockSpec((1, D), lambda i: (0, 0)),
            pl.BlockSpec((D, N), lambda i: (0, 0)),
        ],
        out_specs=pl.BlockSpec((dilation, tm // dilation, N), lambda i: (0, i, 0)),
        out_shape=jax.ShapeDtypeStruct((dilation, L // dilation, N), BF16),
        scratch_shapes=[pltpu.VMEM((D // LANES, tm, LANES), F32), pltpu.VMEM((tm, D), F32)],
        compiler_params=_params("parallel"),
        name=f"qkv_proj_d{dilation}",
    )(x, _row(g), w.astype(BF16))


def _t5_buckets(dilation):
    delta = (np.arange(BLOCK)[:, None] + BLOCK) - np.arange(2 * BLOCK)[None, :]
    dist = np.maximum(delta, 0) * dilation
    max_exact = NUM_BUCKETS // 2
    distf = np.maximum(dist, 1).astype(np.float32)
    large = max_exact + (np.log(distf / np.float32(max_exact)) / np.float32(math.log(MAX_DISTANCE / max_exact))
                         * np.float32(NUM_BUCKETS - max_exact)).astype(np.int32)
    large = np.minimum(large, NUM_BUCKETS - 1)
    return np.where(dist < max_exact, dist, large).astype(np.int32)


def _attn_body(tab_ref, bkt_ref, q_ref, kc_ref, vc_ref, o_ref, lse_ref, bias_scr, k2, v2, *, scale):
    blk = q_ref.shape[0]
    n_heads = bias_scr.shape[1]
    hd = q_ref.shape[1] // n_heads
    n = pl.program_id(1)

    @pl.when((pl.program_id(0) == 0) & (n == 0))
    def _():
        bkt = bkt_ref[...]
        qi = lax.broadcasted_iota(jnp.int32, bkt.shape, 0)
        ki = lax.broadcasted_iota(jnp.int32, bkt.shape, 1)
        delta = qi + blk - ki
        in_band = (delta >= 0) & (delta <= blk)
        for h in range(n_heads):
            acc = jnp.zeros(bkt.shape, F32)
            for b in range(tab_ref.shape[0]):
                acc = jnp.where(bkt == b, tab_ref[b, h], acc)
            acc = jnp.where(in_band, acc, NEG_BIG)
            bias_scr[1, h] = acc
            bias_scr[0, h] = jnp.where(ki >= blk, acc, NEG_BIG)

    @pl.when(n == 0)
    def _():
        k2[0:blk, :] = jnp.zeros((blk, k2.shape[1]), k2.dtype)
        v2[0:blk, :] = jnp.zeros((blk, v2.shape[1]), v2.dtype)

    k2[blk:2 * blk, :] = kc_ref[...]
    v2[blk:2 * blk, :] = vc_ref[...]
    slot = jnp.minimum(n, 1)
    dims = (((1,), (1,)), ((), ()))
    for h in range(n_heads):
        hs = slice(h * hd, (h + 1) * hd)
        qh = q_ref[:, hs] * scale
        s = lax.dot_general(qh, k2[:, hs], dims, preferred_element_type=F32) + bias_scr[slot, h]
        m = jnp.max(s, axis=-1, keepdims=True)
        p = jnp.exp(s - m)
        den = jnp.sum(p, axis=-1, keepdims=True)
        o_ref[:, hs] = _dot(p.astype(BF16), v2[:, hs]) / den
        lse_ref[:, hs] = jnp.broadcast_to(m + jnp.log(den), (blk, hd))
    k2[0:blk, :] = k2[blk:2 * blk, :]
    v2[0:blk, :] = v2[blk:2 * blk, :]


def _dilated_attention(qkv, table, dilation):
    d, rows, W = qkv.shape
    D = W // 3
    n_heads = D // HEAD_DIM
    blk = lambda c: pl.BlockSpec((None, BLOCK, D), lambda r, n: (r, n, c))
    return pl.pallas_call(
        functools.partial(_attn_body, scale=HEAD_DIM ** -0.5),
        grid=(d, rows // BLOCK),
        in_specs=[
            pl.BlockSpec(memory_space=pltpu.SMEM),
            pl.BlockSpec((BLOCK, 2 * BLOCK), lambda r, n: (0, 0)),
            blk(0), blk(1), blk(2),
        ],
        out_specs=[blk(0)] * 2,
        out_shape=[jax.ShapeDtypeStruct((d, rows, D), F32)] * 2,
        scratch_shapes=[pltpu.VMEM((2, n_heads, BLOCK, 2 * BLOCK), F32),
                        pltpu.VMEM((2 * BLOCK, D), BF16), pltpu.VMEM((2 * BLOCK, D), BF16)],
        compiler_params=_params("arbitrary", "arbitrary"),
        name=f"dilated_attn_d{dilation}",
    )(table, jnp.asarray(_t5_buckets(dilation)), qkv, qkv, qkv)


def _attn_out_body(x_ref, gpost_ref, o0_ref, o1_ref, o2_ref, l0_ref, l1_ref, l2_ref, wout_ref, out_ref, *scr):
    def natural(ref, s):
        d, rows, _ = ref.shape
        if d == 1:
            return ref[0]
        for c in range(s.shape[0]):
            for r in range(d):
                s[c, pl.ds(r, rows, stride=d), :] = ref[r, :, c * LANES:(c + 1) * LANES]
        return jnp.concatenate([s[c] for c in range(s.shape[0])], axis=-1)

    l0, l1, l2 = natural(l0_ref, None), natural(l1_ref, scr[0]), natural(l2_ref, scr[1])
    m = jnp.maximum(jnp.maximum(l0, l1), l2)
    e0, e1, e2 = jnp.exp(l0 - m), jnp.exp(l1 - m), jnp.exp(l2 - m)
    o = (e0 * natural(o0_ref, None) + e1 * natural(o1_ref, scr[2]) + e2 * natural(o2_ref, scr[3])) / (e0 + e1 + e2)
    h = _dot(o.astype(BF16), wout_ref[...])
    out_ref[...] = x_ref[...] + _rms(h, gpost_ref[...])


def _attention_mixer(x, gpre, gpost, w_qkv, w_out, rel_bias, tm=512):
    L, D = x.shape
    n_heads = D // HEAD_DIM
    outs, lses = [], []
    for g, (window, dilation) in enumerate(PATTERNS):
        assert window // dilation == BLOCK and L % (BLOCK * dilation) == 0
        qkv = _qkv_proj(x, gpre, w_qkv[:, 3 * D * g:3 * D * (g + 1)], dilation)
        o, lse = _dilated_attention(qkv, rel_bias[:, g * n_heads:(g + 1) * n_heads], dilation)
        outs.append(o)
        lses.append(lse)
    tile = pl.BlockSpec((tm, D), lambda i: (i, 0))
    grouped = [pl.BlockSpec((d, tm // d, D), lambda i: (0, i, 0)) for _, d in PATTERNS]
    return pl.pallas_call(
        _attn_out_body,
        grid=(L // tm,),
        in_specs=[tile, pl.BlockSpec((1, D), lambda i: (0, 0))] + grouped * 2
                 + [pl.BlockSpec((D, D), lambda i: (0, 0))],
        out_specs=tile,
        out_shape=jax.ShapeDtypeStruct((L, D), F32),
        scratch_shapes=[pltpu.VMEM((D // LANES, tm, LANES), F32)] * 4,
        compiler_params=_params("parallel"),
        name="attn_out",
    )(x, _row(gpost), *outs, *lses, w_out.astype(BF16))


def kernel(x, norm_pre, norm_post, ffn_w1, ffn_w3, ffn_w2, rel_bias, s5_w_in, s5_a_re, s5_a_im, s5_log_dt, s5_b_re, s5_b_im, s5_c_re, s5_c_im, s5_d, s5_w_glu, s5_b_glu, s5_w_out, cv_w_in, cv_b_in, cv_dw, cv_dw_b, cv_ln_g, cv_ln_b, cv_w_out, cv_b_out, gm_w_in, gm_b_in, gm_ln_g, gm_ln_b, gm_w_s, gm_b_s, gm_w_out, gm_b_out, at_w_qkv, at_w_out):
    bsz, seq, d_model = x.shape
    assert bsz == 1, "sequence mixers carry state along the row axis of one sequence"
    depth = norm_pre.shape[0]
    n_mixers = 4
    h = x.reshape(seq, d_model)
    for i in range(depth):
        kind, j = i % n_mixers, i // n_mixers
        h = _ffn(h, norm_pre[i, 0], norm_post[i, 0], ffn_w1[i, 0], ffn_w3[i, 0], ffn_w2[i, 0])
        if kind == 0:
            h = _s5_mixer(h, norm_pre[i, 1], norm_post[i, 1], s5_w_in[j], s5_a_re[j], s5_a_im[j], s5_log_dt[j],
                          s5_b_re[j], s5_b_im[j], s5_c_re[j], s5_c_im[j], s5_d[j], s5_w_glu[j], s5_b_glu[j],
                          s5_w_out[j])
        elif kind == 1:
            h = _conv_mixer(h, norm_pre[i, 1], norm_post[i, 1], cv_w_in[j], cv_b_in[j], cv_dw[j], cv_dw_b[j],
                            cv_ln_g[j], cv_ln_b[j], cv_w_out[j], cv_b_out[j])
        elif kind == 2:
            h = _gmlp_mixer(h, norm_pre[i, 1], norm_post[i, 1], gm_w_in[j], gm_b_in[j], gm_ln_g[j], gm_ln_b[j],
                            gm_w_s[j], gm_b_s[j], gm_w_out[j], gm_b_out[j])
        else:
            h = _attention_mixer(h, norm_pre[i, 1], norm_post[i, 1], at_w_qkv[j], at_w_out[j], rel_bias)
        h = _ffn(h, norm_pre[i, 2], norm_post[i, 2], ffn_w1[i, 1], ffn_w3[i, 1], ffn_w2[i, 1])
    return h.reshape(bsz, seq, d_model)
```

```python
import functools
import math

import numpy as np
import jax
import jax.numpy as jnp
from jax import lax
from jax.experimental import pallas as pl
from jax.experimental.pallas import tpu as pltpu

F32 = jnp.float32
BF16 = jnp.bfloat16

EPS = 1e-6
LANES = 128
SUBLANES = 8
S5_GROUP = 16
S5_STATE = 64
S5_GROUPS_PER_TILE = 8
CONV_W = 31
CONV_HALO = 32
GM_CHUNK = 128
GM_HEADS = 8
HEAD_DIM = 64
PATTERNS = ((128, 1), (512, 4), (2048, 16))
BLOCK = 128
NUM_BUCKETS = 32
MAX_DISTANCE = 2048
NEG_BIG = -1e30

VMEM_LIMIT_BYTES = 56 * 1024 * 1024


def _params(*sem):
    return pltpu.CompilerParams(dimension_semantics=sem, vmem_limit_bytes=VMEM_LIMIT_BYTES)


def _rms(x, g):
    return x * lax.rsqrt(jnp.mean(x * x, axis=-1, keepdims=True) + EPS) * g


def _layernorm(x, g, b):
    mu = jnp.mean(x, axis=-1, keepdims=True)
    xc = x - mu
    var = jnp.mean(xc * xc, axis=-1, keepdims=True)
    return xc * lax.rsqrt(var + EPS) * g + b


def _dot(a, b):
    return jnp.dot(a, b, preferred_element_type=F32)


def _row(v):
    return v.reshape(1, -1)


def _ffn_body(x_ref, gpre_ref, gpost_ref, w1_ref, w3_ref, w2_ref, o_ref, xn_scr, acc_scr):
    j = pl.program_id(1)

    @pl.when(j == 0)
    def _():
        xn_scr[...] = _rms(x_ref[...], gpre_ref[...]).astype(BF16)
        acc_scr[...] = jnp.zeros_like(acc_scr)

    xn = xn_scr[...]
    h1 = _dot(xn, w1_ref[...])
    h3 = _dot(xn, w3_ref[...])
    g = (h1 * jax.nn.sigmoid(h1)) * h3
    acc_scr[...] += _dot(g.astype(BF16), w2_ref[...])

    @pl.when(j == pl.num_programs(1) - 1)
    def _():
        o_ref[...] = x_ref[...] + 0.5 * _rms(acc_scr[...], gpost_ref[...])


def _ffn(x, gpre, gpost, w1, w3, w2, tm=512, tf=1408):
    L, D = x.shape
    F = w1.shape[1]
    return pl.pallas_call(
        _ffn_body,
        grid=(L // tm, F // tf),
        in_specs=[
            pl.BlockSpec((tm, D), lambda i, j: (i, 0)),
            pl.BlockSpec((1, D), lambda i, j: (0, 0)),
            pl.BlockSpec((1, D), lambda i, j: (0, 0)),
            pl.BlockSpec((D, tf), lambda i, j: (0, j)),
            pl.BlockSpec((D, tf), lambda i, j: (0, j)),
            pl.BlockSpec((tf, D), lambda i, j: (j, 0)),
        ],
        out_specs=pl.BlockSpec((tm, D), lambda i, j: (i, 0)),
        out_shape=jax.ShapeDtypeStruct((L, D), F32),
        scratch_shapes=[pltpu.VMEM((tm, D), BF16), pltpu.VMEM((tm, D), F32)],
        compiler_params=_params("parallel", "arbitrary"),
        name="ffn",
    )(x, _row(gpre), _row(gpost), w1.astype(BF16), w3.astype(BF16), w2.astype(BF16))


def _s5_body(x_ref, gpre_ref, gpost_ref, win_ref, are_ref, aim_ref, ldt_ref, bdre_ref, bdim_ref,
             cdre_ref, cdim_ref, d_ref, wglu_ref, bglu_ref, wout_ref, o_ref,
             abar_re, abar_im, bb_re, bb_im, st_re, st_im, bu_re, bu_im, y_scr, *, lane_block):
    tt = x_ref.shape[0]
    n_tiles = bdre_ref.shape[0]
    uw = bdre_ref.shape[1]
    sw = bdre_ref.shape[2]
    n_state = n_tiles * sw

    @pl.when(pl.program_id(0) == 0)
    def _():
        ar = are_ref[...]
        ai = aim_ref[...]
        dt = jnp.exp(ldt_ref[...])
        mag = jnp.exp(dt * ar)
        abr = mag * jnp.cos(dt * ai)
        abi = mag * jnp.sin(dt * ai)
        den = ar * ar + ai * ai
        nr = abr - 1.0
        fre = (nr * ar + abi * ai) / den
        fim = (abi * ar - nr * ai) / den
        abar_re[...] = abr
        abar_im[...] = abi
        for k in range(n_tiles):
            fr = fre[:, k * sw:(k + 1) * sw]
            fi = fim[:, k * sw:(k + 1) * sw]
            bb_re[k] = (fr * bdre_ref[k] - fi * bdim_ref[k]).astype(BF16)
            bb_im[k] = (fr * bdim_ref[k] + fi * bdre_ref[k]).astype(BF16)
        st_re[...] = jnp.zeros_like(st_re)
        st_im[...] = jnp.zeros_like(st_im)

    x = x_ref[...]
    xn = _rms(x, gpre_ref[...]).astype(BF16)
    u = _dot(xn, win_ref[...])
    ub = u.astype(BF16)
    for k in range(n_tiles):
        uk = ub[:, k * uw:(k + 1) * uw]
        bu_re[:, k * sw:(k + 1) * sw] = _dot(uk, bb_re[k])
        bu_im[:, k * sw:(k + 1) * sw] = _dot(uk, bb_im[k])

    for lb in range(n_state // lane_block):
        sl = slice(lb * lane_block, (lb + 1) * lane_block)
        a_r = abar_re[:, sl]
        a_i = abar_im[:, sl]

        def step(t, carry, sl=sl, a_r=a_r, a_i=a_i):
            sr, si = carry
            nr = a_r * sr - a_i * si + bu_re[pl.ds(t, 1), sl]
            ni = a_r * si + a_i * sr + bu_im[pl.ds(t, 1), sl]
            bu_re[pl.ds(t, 1), sl] = nr
            bu_im[pl.ds(t, 1), sl] = ni
            return nr, ni

        sr, si = lax.fori_loop(0, tt, step, (st_re[:, sl], st_im[:, sl]), unroll=8)
        st_re[:, sl] = sr
        st_im[:, sl] = si

    for k in range(n_tiles):
        sre = bu_re[:, k * sw:(k + 1) * sw].astype(BF16)
        sim = bu_im[:, k * sw:(k + 1) * sw].astype(BF16)
        y_scr[:, k * uw:(k + 1) * uw] = _dot(sre, cdre_ref[k]) - _dot(sim, cdim_ref[k])
    y = jax.nn.gelu(y_scr[...] + d_ref[...] * u)
    z = y * jax.nn.sigmoid(_dot(y.astype(BF16), wglu_ref[...]) + bglu_ref[...])
    h = _dot(z.astype(BF16), wout_ref[...])
    o_ref[...] = x + _rms(h, gpost_ref[...])


def _s5_block_diag(b, c):
    G, P, HG = b.shape
    gt = S5_GROUPS_PER_TILE
    eye = jnp.eye(gt, dtype=b.dtype)
    bt = b.reshape(G // gt, gt, P, HG).transpose(0, 1, 3, 2)
    bd = jnp.einsum('kghp,gj->kghjp', bt, eye).reshape(G // gt, gt * HG, gt * P)
    ct = c.reshape(G // gt, gt, HG, P)
    cd = jnp.einsum('kghp,gj->kjpgh', ct, eye).reshape(G // gt, gt * P, gt * HG)
    return bd, cd


def _s5_mixer(x, gpre, gpost, w_in, a_re, a_im, log_dt, b_re, b_im, c_re, c_im, d_skip, w_glu, b_glu,
              w_out, tt=256, lane_block=1024):
    L, D = x.shape
    G, P = a_re.shape
    n_state = G * P
    bd_re, cd_re = _s5_block_diag(b_re, c_re)
    bd_im, cd_im = _s5_block_diag(b_im, c_im)
    n_tiles, uw, sw = bd_re.shape
    full2 = lambda shape: pl.BlockSpec(shape, lambda i: (0, 0))
    full3 = lambda shape: pl.BlockSpec(shape, lambda i: (0, 0, 0))
    return pl.pallas_call(
        functools.partial(_s5_body, lane_block=lane_block),
        grid=(L // tt,),
        in_specs=[
            pl.BlockSpec((tt, D), lambda i: (i, 0)),
            full2((1, D)), full2((1, D)), full2((D, D)),
            full2((1, n_state)), full2((1, n_state)), full2((1, n_state)),
            full3((n_tiles, uw, sw)), full3((n_tiles, uw, sw)),
            full3((n_tiles, sw, uw)), full3((n_tiles, sw, uw)),
            full2((1, D)), full2((D, D)), full2((1, D)), full2((D, D)),
        ],
        out_specs=pl.BlockSpec((tt, D), lambda i: (i, 0)),
        out_shape=jax.ShapeDtypeStruct((L, D), F32),
        scratch_shapes=[
            pltpu.VMEM((1, n_state), F32), pltpu.VMEM((1, n_state), F32),
            pltpu.VMEM((n_tiles, uw, sw), BF16), pltpu.VMEM((n_tiles, uw, sw), BF16),
            pltpu.VMEM((1, n_state), F32), pltpu.VMEM((1, n_state), F32),
            pltpu.VMEM((tt, n_state), F32), pltpu.VMEM((tt, n_state), F32),
            pltpu.VMEM((tt, D), F32),
        ],
        compiler_params=_params("arbitrary"),
        name="s5_mixer",
    )(x, _row(gpre), _row(gpost), w_in.astype(BF16),
      _row(a_re), _row(a_im), _row(jnp.repeat(log_dt, P)),
      bd_re, bd_im, cd_re.astype(BF16), cd_im.astype(BF16),
      _row(d_skip), w_glu.astype(BF16), _row(b_glu), w_out.astype(BF16))


def _conv_body(x_ref, gpre_ref, gpost_ref, wa_ref, wg_ref, ba_ref, bg_ref, dw_ref, dwb_ref,
               lng_ref, lnb_ref, wout_ref, bout_ref, o_ref, ext, conv_scr):
    tm, D = x_ref.shape
    halo = ext.shape[0] - tm
    n_taps = dw_ref.shape[0]

    @pl.when(pl.program_id(0) == 0)
    def _():
        ext[0:halo, :] = jnp.zeros((halo, D), F32)

    x = x_ref[...]
    xn = _rms(x, gpre_ref[...]).astype(BF16)
    a = _dot(xn, wa_ref[...]) + ba_ref[...]
    g = _dot(xn, wg_ref[...]) + bg_ref[...]
    ext[halo:halo + tm, :] = a * jax.nn.sigmoid(g)

    base = halo - (n_taps - 1)
    for c in range(D // LANES):
        cs = slice(c * LANES, (c + 1) * LANES)
        acc = None
        for s in range(SUBLANES):
            rows = tm if s == 0 else tm + SUBLANES
            q = None
            for a in range((base + n_taps - 1) // SUBLANES + 1):
                k = SUBLANES * a + s - base
                if 0 <= k < n_taps:
                    term = dw_ref[k:k + 1, cs] * ext[SUBLANES * a:SUBLANES * a + rows, cs]
                    q = term if q is None else q + term
            part = q if s == 0 else q[s:s + tm]
            acc = part if acc is None else acc + part
        conv_scr[:, cs] = acc
    ext[0:halo, :] = ext[tm:tm + halo, :]

    y = _layernorm(conv_scr[...] + dwb_ref[...], lng_ref[...], lnb_ref[...])
    y = y * jax.nn.sigmoid(y)
    h = _dot(y.astype(BF16), wout_ref[...]) + bout_ref[...]
    o_ref[...] = x + _rms(h, gpost_ref[...])


def _conv_mixer(x, gpre, gpost, w_in, b_in, dw, dw_b, ln_g, ln_b, w_out, b_out, tm=256):
    L, D = x.shape
    full2 = lambda shape: pl.BlockSpec(shape, lambda i: (0, 0))
    return pl.pallas_call(
        _conv_body,
        grid=(L // tm,),
        in_specs=[
            pl.BlockSpec((tm, D), lambda i: (i, 0)),
            full2((1, D)), full2((1, D)), full2((D, D)), full2((D, D)), full2((1, D)), full2((1, D)),
            full2(dw.shape), full2((1, D)), full2((1, D)), full2((1, D)), full2((D, D)), full2((1, D)),
        ],
        out_specs=pl.BlockSpec((tm, D), lambda i: (i, 0)),
        out_shape=jax.ShapeDtypeStruct((L, D), F32),
        scratch_shapes=[pltpu.VMEM((tm + CONV_HALO, D), F32), pltpu.VMEM((tm, D), F32)],
        compiler_params=_params("arbitrary"),
        name="conv_mixer",
    )(x, _row(gpre), _row(gpost), w_in[:, :D].astype(BF16), w_in[:, D:].astype(BF16),
      _row(b_in[:D]), _row(b_in[D:]), dw, _row(dw_b), _row(ln_g), _row(ln_b),
      w_out.astype(BF16), _row(b_out))


def _gmlp_body(x_ref, gpre_ref, gpost_ref, win_ref, bin_ref, lng_ref, lnb_ref, ws_ref, bs_ref,
               wout_ref, bout_ref, o_ref, su_scr):
    tm = x_ref.shape[0]
    E = lng_ref.shape[1]
    n_heads, chunk, _ = ws_ref.shape
    hw = E // n_heads
    x = x_ref[...]
    xn = _rms(x, gpre_ref[...]).astype(BF16)
    z = jax.nn.gelu(_dot(xn, win_ref[...]) + bin_ref[...])
    u = z[:, :E]
    vb = _layernorm(z[:, E:], lng_ref[...], lnb_ref[...]).astype(BF16)
    causal = (lax.broadcasted_iota(jnp.int32, (chunk, chunk), 1)
              <= lax.broadcasted_iota(jnp.int32, (chunk, chunk), 0))
    for h in range(n_heads):
        wm = jnp.where(causal, ws_ref[h], 0.0).astype(BF16)
        bias = bs_ref[:, h:h + 1]
        for c in range(tm // chunk):
            rs = slice(c * chunk, (c + 1) * chunk)
            hs = slice(h * hw, (h + 1) * hw)
            s = _dot(wm, vb[rs, hs]) + bias
            su_scr[rs, hs] = (u[rs, hs] * s).astype(BF16)
    hout = _dot(su_scr[...], wout_ref[...]) + bout_ref[...]
    o_ref[...] = x + _rms(hout, gpost_ref[...])


def _gmlp_mixer(x, gpre, gpost, w_in, b_in, ln_g, ln_b, w_s, b_s, w_out, b_out, tm=256):
    L, D = x.shape
    E = ln_g.shape[0]
    full2 = lambda shape: pl.BlockSpec(shape, lambda i: (0, 0))
    return pl.pallas_call(
        _gmlp_body,
        grid=(L // tm,),
        in_specs=[
            pl.BlockSpec((tm, D), lambda i: (i, 0)),
            full2((1, D)), full2((1, D)), full2((D, 2 * E)), full2((1, 2 * E)), full2((1, E)), full2((1, E)),
            pl.BlockSpec(w_s.shape, lambda i: (0, 0, 0)), full2((GM_CHUNK, GM_HEADS)),
            full2((E, D)), full2((1, D)),
        ],
        out_specs=pl.BlockSpec((tm, D), lambda i: (i, 0)),
        out_shape=jax.ShapeDtypeStruct((L, D), F32),
        scratch_shapes=[pltpu.VMEM((tm, E), BF16)],
        compiler_params=_params("parallel"),
        name="gmlp_mixer",
    )(x, _row(gpre), _row(gpost), w_in.astype(BF16), _row(b_in), _row(ln_g), _row(ln_b),
      w_s, b_s.T, w_out.astype(BF16), _row(b_out))


def _qkv_body(x_ref, g_ref, w_ref, o_ref, xt_scr, xs_scr):
    d, rows, _ = o_ref.shape
    if d == 1:
        x = x_ref[...]
    else:
        for c in range(xt_scr.shape[0]):
            cs = slice(c * LANES, (c + 1) * LANES)
            xt_scr[c] = x_ref[:, cs]
            for r in range(d):
                xs_scr[r * rows:(r + 1) * rows, cs] = xt_scr[c, pl.ds(r, rows, stride=d), :]
        x = xs_scr[...]
    res = _dot(_rms(x, g_ref[...]).astype(BF16), w_ref[...])
    for r in range(d):
        o_ref[r] = res[r * rows:(r + 1) * rows, :].astype(o_ref.dtype)


def _qkv_proj(x, g, w, dilation, tm=512):
    L, D = x.shape
    N = w.shape[1]
    return pl.pallas_call(
        _qkv_body,
        grid=(L // tm,),
        in_specs=[
            pl.BlockSpec((tm, D), lambda i: (i, 0)),
            pl.BlockSpec((1, D), lambda i: (0, 0)),
            pl.BlockSpec((D, N), lambda i: (0, 0)),
        ],
        out_specs=pl.BlockSpec((dilation, tm // dilation, N), lambda i: (0, i, 0)),
        out_shape=jax.ShapeDtypeStruct((dilation, L // dilation, N), BF16),
        scratch_shapes=[pltpu.VMEM((D // LANES, tm, LANES), F32), pltpu.VMEM((tm, D), F32)],
        compiler_params=_params("parallel"),
        name=f"qkv_proj_d{dilation}",
    )(x, _row(g), w.astype(BF16))


def _t5_buckets(dilation):
    delta = (np.arange(BLOCK)[:, None] + BLOCK) - np.arange(2 * BLOCK)[None, :]
    dist = np.maximum(delta, 0) * dilation
    max_exact = NUM_BUCKETS // 2
    distf = np.maximum(dist, 1).astype(np.float32)
    large = max_exact + (np.log(distf / np.float32(max_exact)) / np.float32(math.log(MAX_DISTANCE / max_exact))
                         * np.float32(NUM_BUCKETS - max_exact)).astype(np.int32)
    large = np.minimum(large, NUM_BUCKETS - 1)
    return np.where(dist < max_exact, dist, large).astype(np.int32)


def _attn_body(tab_ref, bkt_ref, q_ref, kc_ref, vc_ref, o_ref, lse_ref, bias_scr, k2, v2, s_scr, p_scr, *, scale):
    blk = q_ref.shape[0]
    n_heads = bias_scr.shape[1]
    hd = q_ref.shape[1] // n_heads
    n = pl.program_id(1)

    @pl.when((pl.program_id(0) == 0) & (n == 0))
    def _():
        bkt = bkt_ref[...]
        qi = lax.broadcasted_iota(jnp.int32, bkt.shape, 0)
        ki = lax.broadcasted_iota(jnp.int32, bkt.shape, 1)
        delta = qi + blk - ki
        in_band = (delta >= 0) & (delta <= blk)
        for h in range(n_heads):
            acc = jnp.zeros(bkt.shape, F32)
            for b in range(tab_ref.shape[0]):
                acc = jnp.where(bkt == b, tab_ref[b, h], acc)
            acc = jnp.where(in_band, acc, NEG_BIG)
            bias_scr[1, h] = acc
            bias_scr[0, h] = jnp.where(ki >= blk, acc, NEG_BIG)

    @pl.when(n == 0)
    def _():
        k2[0:blk, :] = jnp.zeros((blk, k2.shape[1]), k2.dtype)
        v2[0:blk, :] = jnp.zeros((blk, v2.shape[1]), v2.dtype)

    k2[blk:2 * blk, :] = kc_ref[...]
    v2[blk:2 * blk, :] = vc_ref[...]
    slot = jnp.minimum(n, 1)
    dims = (((1,), (1,)), ((), ()))
    per_tile = LANES // hd
    lane = lax.broadcasted_iota(jnp.int32, (1, LANES), 1)
    own = [(lane >= i * hd) & (lane < (i + 1) * hd) for i in range(per_tile)]
    n_tiles = n_heads // per_tile

    for t in range(n_tiles):
        cs = slice(t * LANES, (t + 1) * LANES)
        qt = q_ref[:, cs] * scale
        kt = k2[:, cs]
        for i in range(per_tile):
            h = t * per_tile + i
            qh = jnp.where(own[i], qt, jnp.zeros_like(qt))
            s_scr[h] = lax.dot_general(qh, kt, dims, preferred_element_type=F32) + bias_scr[slot, h]

    for t in range(n_tiles):
        lse_t = jnp.zeros((blk, LANES), F32)
        for i in range(per_tile):
            h = t * per_tile + i
            s = s_scr[h]
            m = jnp.max(s, axis=-1, keepdims=True)
            e = jnp.exp(s - m)
            den = jnp.sum(e, axis=-1, keepdims=True)
            p_scr[h] = (e * (1.0 / den)).astype(BF16)
            lse_t = jnp.where(own[i], m + jnp.log(den), lse_t)
        lse_ref[:, t * LANES:(t + 1) * LANES] = lse_t

    for t in range(n_tiles):
        cs = slice(t * LANES, (t + 1) * LANES)
        vt = v2[:, cs]
        acc = jnp.zeros((blk, LANES), F32)
        for i in range(per_tile):
            acc = acc + _dot(p_scr[t * per_tile + i], jnp.where(own[i], vt, jnp.zeros_like(vt)))
        o_ref[:, cs] = acc
    k2[0:blk, :] = k2[blk:2 * blk, :]
    v2[0:blk, :] = v2[blk:2 * blk, :]


def _dilated_attention(qkv, table, dilation):
    d, rows, W = qkv.shape
    D = W // 3
    n_heads = D // HEAD_DIM
    blk = lambda c: pl.BlockSpec((None, BLOCK, D), lambda r, n: (r, n, c))
    return pl.pallas_call(
        functools.partial(_attn_body, scale=HEAD_DIM ** -0.5),
        grid=(d, rows // BLOCK),
        in_specs=[
            pl.BlockSpec(memory_space=pltpu.SMEM),
            pl.BlockSpec((BLOCK, 2 * BLOCK), lambda r, n: (0, 0)),
            blk(0), blk(1), blk(2),
        ],
        out_specs=[blk(0)] * 2,
        out_shape=[jax.ShapeDtypeStruct((d, rows, D), F32)] * 2,
        scratch_shapes=[pltpu.VMEM((2, n_heads, BLOCK, 2 * BLOCK), F32),
                        pltpu.VMEM((2 * BLOCK, D), BF16), pltpu.VMEM((2 * BLOCK, D), BF16),
                        pltpu.VMEM((n_heads, BLOCK, 2 * BLOCK), F32),
                        pltpu.VMEM((n_heads, BLOCK, 2 * BLOCK), BF16)],
        compiler_params=_params("arbitrary", "arbitrary"),
        name=f"dilated_attn_d{dilation}",
    )(table, jnp.asarray(_t5_buckets(dilation)), qkv, qkv, qkv)


def _attn_out_body(x_ref, gpost_ref, o0_ref, o1_ref, o2_ref, l0_ref, l1_ref, l2_ref, wout_ref, out_ref, *scr):
    def natural(ref, s):
        d, rows, _ = ref.shape
        if d == 1:
            return ref[0]
        for c in range(s.shape[0]):
            for r in range(d):
                s[c, pl.ds(r, rows, stride=d), :] = ref[r, :, c * LANES:(c + 1) * LANES]
        return jnp.concatenate([s[c] for c in range(s.shape[0])], axis=-1)

    l0, l1, l2 = natural(l0_ref, None), natural(l1_ref, scr[0]), natural(l2_ref, scr[1])
    m = jnp.maximum(jnp.maximum(l0, l1), l2)
    e0, e1, e2 = jnp.exp(l0 - m), jnp.exp(l1 - m), jnp.exp(l2 - m)
    o = (e0 * natural(o0_ref, None) + e1 * natural(o1_ref, scr[2]) + e2 * natural(o2_ref, scr[3])) / (e0 + e1 + e2)
    h = _dot(o.astype(BF16), wout_ref[...])
    out_ref[...] = x_ref[...] + _rms(h, gpost_ref[...])


def _attention_mixer(x, gpre, gpost, w_qkv, w_out, rel_bias, tm=512):
    L, D = x.shape
    n_heads = D // HEAD_DIM
    outs, lses = [], []
    for g, (window, dilation) in enumerate(PATTERNS):
        assert window // dilation == BLOCK and L % (BLOCK * dilation) == 0
        qkv = _qkv_proj(x, gpre, w_qkv[:, 3 * D * g:3 * D * (g + 1)], dilation)
        o, lse = _dilated_attention(qkv, rel_bias[:, g * n_heads:(g + 1) * n_heads], dilation)
        outs.append(o)
        lses.append(lse)
    tile = pl.BlockSpec((tm, D), lambda i: (i, 0))
    grouped = [pl.BlockSpec((d, tm // d, D), lambda i: (0, i, 0)) for _, d in PATTERNS]
    return pl.pallas_call(
        _attn_out_body,
        grid=(L // tm,),
        in_specs=[tile, pl.BlockSpec((1, D), lambda i: (0, 0))] + grouped * 2
                 + [pl.BlockSpec((D, D), lambda i: (0, 0))],
        out_specs=tile,
        out_shape=jax.ShapeDtypeStruct((L, D), F32),
        scratch_shapes=[pltpu.VMEM((D // LANES, tm, LANES), F32)] * 4,
        compiler_params=_params("parallel"),
        name="attn_out",
    )(x, _row(gpost), *outs, *lses, w_out.astype(BF16))


def kernel(x, norm_pre, norm_post, ffn_w1, ffn_w3, ffn_w2, rel_bias, s5_w_in, s5_a_re, s5_a_im, s5_log_dt, s5_b_re, s5_b_im, s5_c_re, s5_c_im, s5_d, s5_w_glu, s5_b_glu, s5_w_out, cv_w_in, cv_b_in, cv_dw, cv_dw_b, cv_ln_g, cv_ln_b, cv_w_out, cv_b_out, gm_w_in, gm_b_in, gm_ln_g, gm_ln_b, gm_w_s, gm_b_s, gm_w_out, gm_b_out, at_w_qkv, at_w_out):
    bsz, seq, d_model = x.shape
    assert bsz == 1, "sequence mixers carry state along the row axis of one sequence"
    depth = norm_pre.shape[0]
    n_mixers = 4
    h = x.reshape(seq, d_model)
    for i in range(depth):
        kind, j = i % n_mixers, i // n_mixers
        h = _ffn(h, norm_pre[i, 0], norm_post[i, 0], ffn_w1[i, 0], ffn_w3[i, 0], ffn_w2[i, 0])
        if kind == 0:
            h = _s5_mixer(h, norm_pre[i, 1], norm_post[i, 1], s5_w_in[j], s5_a_re[j], s5_a_im[j], s5_log_dt[j],
                          s5_b_re[j], s5_b_im[j], s5_c_re[j], s5_c_im[j], s5_d[j], s5_w_glu[j], s5_b_glu[j],
                          s5_w_out[j])
        elif kind == 1:
            h = _conv_mixer(h, norm_pre[i, 1], norm_post[i, 1], cv_w_in[j], cv_b_in[j], cv_dw[j], cv_dw_b[j],
                            cv_ln_g[j], cv_ln_b[j], cv_w_out[j], cv_b_out[j])
        elif kind == 2:
            h = _gmlp_mixer(h, norm_pre[i, 1], norm_post[i, 1], gm_w_in[j], gm_b_in[j], gm_ln_g[j], gm_ln_b[j],
                            gm_w_s[j], gm_b_s[j], gm_w_out[j], gm_b_out[j])
        else:
            h = _attention_mixer(h, norm_pre[i, 1], norm_post[i, 1], at_w_qkv[j], at_w_out[j], rel_bias)
        h = _ffn(h, norm_pre[i, 2], norm_post[i, 2], ffn_w1[i, 1], ffn_w3[i, 1], ffn_w2[i, 1])
    return h.reshape(bsz, seq, d_model)
```

```python
import functools
import math

import numpy as np
import jax
import jax.numpy as jnp
from jax import lax
from jax.experimental import pallas as pl
from jax.experimental.pallas import tpu as pltpu

F32 = jnp.float32
BF16 = jnp.bfloat16

EPS = 1e-6
LANES = 128
SUBLANES = 8
S5_GROUP = 16
S5_STATE = 64
S5_GROUPS_PER_TILE = 8
CONV_W = 31
CONV_HALO = 32
GM_CHUNK = 128
GM_HEADS = 8
HEAD_DIM = 64
PATTERNS = ((128, 1), (512, 4), (2048, 16))
BLOCK = 128
NUM_BUCKETS = 32
MAX_DISTANCE = 2048
NEG_BIG = -1e30

VMEM_LIMIT_BYTES = 56 * 1024 * 1024


def _params(*sem):
    return pltpu.CompilerParams(dimension_semantics=sem, vmem_limit_bytes=VMEM_LIMIT_BYTES)


def _rms(x, g):
    return x * lax.rsqrt(jnp.mean(x * x, axis=-1, keepdims=True) + EPS) * g


def _layernorm(x, g, b):
    mu = jnp.mean(x, axis=-1, keepdims=True)
    xc = x - mu
    var = jnp.mean(xc * xc, axis=-1, keepdims=True)
    return xc * lax.rsqrt(var + EPS) * g + b


def _dot(a, b):
    return jnp.dot(a, b, preferred_element_type=F32)


def _row(v):
    return v.reshape(1, -1)


def _ffn_body(x_ref, gpre_ref, gpost_ref, w1_ref, w3_ref, w2_ref, o_ref, g_scr, *, fc):
    n_sub, sub, F = g_scr.shape
    for s in range(n_sub):
        rs = slice(s * sub, (s + 1) * sub)
        xn = _rms(x_ref[rs, :], gpre_ref[...]).astype(BF16)
        for c in range(F // fc):
            cs = slice(c * fc, (c + 1) * fc)
            h1 = _dot(xn, w1_ref[:, cs])
            h3 = _dot(xn, w3_ref[:, cs])
            g_scr[s, :, cs] = ((h1 * jax.nn.sigmoid(h1)) * h3).astype(BF16)
        h = _dot(g_scr[s], w2_ref[...])
        o_ref[rs, :] = x_ref[rs, :] + 0.5 * _rms(h, gpost_ref[...])


def _resident(shape):
    return pl.BlockSpec(shape, lambda i: (0,) * len(shape), pipeline_mode=pl.Buffered(1))


def _ffn(x, gpre, gpost, w1, w3, w2, tm=1024, sub=512, fc=256):
    L, D = x.shape
    F = w1.shape[1]
    return pl.pallas_call(
        functools.partial(_ffn_body, fc=fc),
        grid=(L // tm,),
        in_specs=[
            pl.BlockSpec((tm, D), lambda i: (i, 0)),
            _resident((1, D)), _resident((1, D)),
            _resident((D, F)), _resident((D, F)), _resident((F, D)),
        ],
        out_specs=pl.BlockSpec((tm, D), lambda i: (i, 0)),
        out_shape=jax.ShapeDtypeStruct((L, D), F32),
        scratch_shapes=[pltpu.VMEM((tm // sub, sub, F), BF16)],
        compiler_params=_params("parallel"),
        name="ffn",
    )(x, _row(gpre), _row(gpost), w1.astype(BF16), w3.astype(BF16), w2.astype(BF16))


def _s5_body(x_ref, gpre_ref, gpost_ref, win_ref, are_ref, aim_ref, ldt_ref, bdre_ref, bdim_ref,
             cdre_ref, cdim_ref, d_ref, wglu_ref, bglu_ref, wout_ref, o_ref,
             abar_re, abar_im, bb_re, bb_im, st_re, st_im, bu_re, bu_im, y_scr, *, lane_block):
    tt = x_ref.shape[0]
    n_tiles = bdre_ref.shape[0]
    uw = bdre_ref.shape[1]
    sw = bdre_ref.shape[2]
    n_state = n_tiles * sw

    @pl.when(pl.program_id(0) == 0)
    def _():
        ar = are_ref[...]
        ai = aim_ref[...]
        dt = jnp.exp(ldt_ref[...])
        mag = jnp.exp(dt * ar)
        abr = mag * jnp.cos(dt * ai)
        abi = mag * jnp.sin(dt * ai)
        den = ar * ar + ai * ai
        nr = abr - 1.0
        fre = (nr * ar + abi * ai) / den
        fim = (abi * ar - nr * ai) / den
        abar_re[...] = abr
        abar_im[...] = abi
        for k in range(n_tiles):
            fr = fre[:, k * sw:(k + 1) * sw]
            fi = fim[:, k * sw:(k + 1) * sw]
            bb_re[k] = (fr * bdre_ref[k] - fi * bdim_ref[k]).astype(BF16)
            bb_im[k] = (fr * bdim_ref[k] + fi * bdre_ref[k]).astype(BF16)
        st_re[...] = jnp.zeros_like(st_re)
        st_im[...] = jnp.zeros_like(st_im)

    x = x_ref[...]
    xn = _rms(x, gpre_ref[...]).astype(BF16)
    u = _dot(xn, win_ref[...])
    ub = u.astype(BF16)
    for k in range(n_tiles):
        uk = ub[:, k * uw:(k + 1) * uw]
        bu_re[:, k * sw:(k + 1) * sw] = _dot(uk, bb_re[k])
        bu_im[:, k * sw:(k + 1) * sw] = _dot(uk, bb_im[k])

    for lb in range(n_state // lane_block):
        sl = slice(lb * lane_block, (lb + 1) * lane_block)
        a_r = abar_re[:, sl]
        a_i = abar_im[:, sl]

        def step(t, carry, sl=sl, a_r=a_r, a_i=a_i):
            sr, si = carry
            nr = a_r * sr - a_i * si + bu_re[pl.ds(t, 1), sl]
            ni = a_r * si + a_i * sr + bu_im[pl.ds(t, 1), sl]
            bu_re[pl.ds(t, 1), sl] = nr
            bu_im[pl.ds(t, 1), sl] = ni
            return nr, ni

        sr, si = lax.fori_loop(0, tt, step, (st_re[:, sl], st_im[:, sl]), unroll=8)
        st_re[:, sl] = sr
        st_im[:, sl] = si

    for k in range(n_tiles):
        sre = bu_re[:, k * sw:(k + 1) * sw].astype(BF16)
        sim = bu_im[:, k * sw:(k + 1) * sw].astype(BF16)
        y_scr[:, k * uw:(k + 1) * uw] = _dot(sre, cdre_ref[k]) - _dot(sim, cdim_ref[k])
    y = jax.nn.gelu(y_scr[...] + d_ref[...] * u)
    z = y * jax.nn.sigmoid(_dot(y.astype(BF16), wglu_ref[...]) + bglu_ref[...])
    h = _dot(z.astype(BF16), wout_ref[...])
    o_ref[...] = x + _rms(h, gpost_ref[...])


def _s5_block_diag(b, c):
    G, P, HG = b.shape
    gt = S5_GROUPS_PER_TILE
    eye = jnp.eye(gt, dtype=b.dtype)
    bt = b.reshape(G // gt, gt, P, HG).transpose(0, 1, 3, 2)
    bd = jnp.einsum('kghp,gj->kghjp', bt, eye).reshape(G // gt, gt * HG, gt * P)
    ct = c.reshape(G // gt, gt, HG, P)
    cd = jnp.einsum('kghp,gj->kjpgh', ct, eye).reshape(G // gt, gt * P, gt * HG)
    return bd, cd


def _s5_mixer(x, gpre, gpost, w_in, a_re, a_im, log_dt, b_re, b_im, c_re, c_im, d_skip, w_glu, b_glu,
              w_out, tt=256, lane_block=1024):
    L, D = x.shape
    G, P = a_re.shape
    n_state = G * P
    bd_re, cd_re = _s5_block_diag(b_re, c_re)
    bd_im, cd_im = _s5_block_diag(b_im, c_im)
    n_tiles, uw, sw = bd_re.shape
    full2 = lambda shape: pl.BlockSpec(shape, lambda i: (0, 0))
    full3 = lambda shape: pl.BlockSpec(shape, lambda i: (0, 0, 0))
    return pl.pallas_call(
        functools.partial(_s5_body, lane_block=lane_block),
        grid=(L // tt,),
        in_specs=[
            pl.BlockSpec((tt, D), lambda i: (i, 0)),
            full2((1, D)), full2((1, D)), full2((D, D)),
            full2((1, n_state)), full2((1, n_state)), full2((1, n_state)),
            full3((n_tiles, uw, sw)), full3((n_tiles, uw, sw)),
            full3((n_tiles, sw, uw)), full3((n_tiles, sw, uw)),
            full2((1, D)), full2((D, D)), full2((1, D)), full2((D, D)),
        ],
        out_specs=pl.BlockSpec((tt, D), lambda i: (i, 0)),
        out_shape=jax.ShapeDtypeStruct((L, D), F32),
        scratch_shapes=[
            pltpu.VMEM((1, n_state), F32), pltpu.VMEM((1, n_state), F32),
            pltpu.VMEM((n_tiles, uw, sw), BF16), pltpu.VMEM((n_tiles, uw, sw), BF16),
            pltpu.VMEM((1, n_state), F32), pltpu.VMEM((1, n_state), F32),
            pltpu.VMEM((tt, n_state), F32), pltpu.VMEM((tt, n_state), F32),
            pltpu.VMEM((tt, D), F32),
        ],
        compiler_params=_params("arbitrary"),
        name="s5_mixer",
    )(x, _row(gpre), _row(gpost), w_in.astype(BF16),
      _row(a_re), _row(a_im), _row(jnp.repeat(log_dt, P)),
      bd_re, bd_im, cd_re.astype(BF16), cd_im.astype(BF16),
      _row(d_skip), w_glu.astype(BF16), _row(b_glu), w_out.astype(BF16))


def _conv_body(x_ref, gpre_ref, gpost_ref, wa_ref, wg_ref, ba_ref, bg_ref, dw_ref, dwb_ref,
               lng_ref, lnb_ref, wout_ref, bout_ref, o_ref, ext, conv_scr):
    tm, D = x_ref.shape
    halo = ext.shape[0] - tm
    n_taps = dw_ref.shape[0]

    @pl.when(pl.program_id(0) == 0)
    def _():
        ext[0:halo, :] = jnp.zeros((halo, D), F32)

    x = x_ref[...]
    xn = _rms(x, gpre_ref[...]).astype(BF16)
    a = _dot(xn, wa_ref[...]) + ba_ref[...]
    g = _dot(xn, wg_ref[...]) + bg_ref[...]
    ext[halo:halo + tm, :] = a * jax.nn.sigmoid(g)

    base = halo - (n_taps - 1)
    for c in range(D // LANES):
        cs = slice(c * LANES, (c + 1) * LANES)
        acc = None
        for s in range(SUBLANES):
            rows = tm if s == 0 else tm + SUBLANES
            q = None
            for a in range((base + n_taps - 1) // SUBLANES + 1):
                k = SUBLANES * a + s - base
                if 0 <= k < n_taps:
                    term = dw_ref[k:k + 1, cs] * ext[SUBLANES * a:SUBLANES * a + rows, cs]
                    q = term if q is None else q + term
            part = q if s == 0 else q[s:s + tm]
            acc = part if acc is None else acc + part
        conv_scr[:, cs] = acc
    ext[0:halo, :] = ext[tm:tm + halo, :]

    y = _layernorm(conv_scr[...] + dwb_ref[...], lng_ref[...], lnb_ref[...])
    y = y * jax.nn.sigmoid(y)
    h = _dot(y.astype(BF16), wout_ref[...]) + bout_ref[...]
    o_ref[...] = x + _rms(h, gpost_ref[...])


def _conv_mixer(x, gpre, gpost, w_in, b_in, dw, dw_b, ln_g, ln_b, w_out, b_out, tm=256):
    L, D = x.shape
    full2 = lambda shape: pl.BlockSpec(shape, lambda i: (0, 0))
    return pl.pallas_call(
        _conv_body,
        grid=(L // tm,),
        in_specs=[
            pl.BlockSpec((tm, D), lambda i: (i, 0)),
            full2((1, D)), full2((1, D)), full2((D, D)), full2((D, D)), full2((1, D)), full2((1, D)),
            full2(dw.shape), full2((1, D)), full2((1, D)), full2((1, D)), full2((D, D)), full2((1, D)),
        ],
        out_specs=pl.BlockSpec((tm, D), lambda i: (i, 0)),
        out_shape=jax.ShapeDtypeStruct((L, D), F32),
        scratch_shapes=[pltpu.VMEM((tm + CONV_HALO, D), F32), pltpu.VMEM((tm, D), F32)],
        compiler_params=_params("arbitrary"),
        name="conv_mixer",
    )(x, _row(gpre), _row(gpost), w_in[:, :D].astype(BF16), w_in[:, D:].astype(BF16),
      _row(b_in[:D]), _row(b_in[D:]), dw, _row(dw_b), _row(ln_g), _row(ln_b),
      w_out.astype(BF16), _row(b_out))


def _gmlp_body(x_ref, gpre_ref, gpost_ref, win_ref, bin_ref, lng_ref, lnb_ref, ws_ref, bs_ref,
               wout_ref, bout_ref, o_ref, su_scr):
    tm = x_ref.shape[0]
    E = lng_ref.shape[1]
    n_heads, chunk, _ = ws_ref.shape
    hw = E // n_heads
    x = x_ref[...]
    xn = _rms(x, gpre_ref[...]).astype(BF16)
    z = jax.nn.gelu(_dot(xn, win_ref[...]) + bin_ref[...])
    u = z[:, :E]
    vb = _layernorm(z[:, E:], lng_ref[...], lnb_ref[...]).astype(BF16)
    causal = (lax.broadcasted_iota(jnp.int32, (chunk, chunk), 1)
              <= lax.broadcasted_iota(jnp.int32, (chunk, chunk), 0))
    for h in range(n_heads):
        wm = jnp.where(causal, ws_ref[h], 0.0).astype(BF16)
        bias = bs_ref[:, h:h + 1]
        for c in range(tm // chunk):
            rs = slice(c * chunk, (c + 1) * chunk)
            hs = slice(h * hw, (h + 1) * hw)
            s = _dot(wm, vb[rs, hs]) + bias
            su_scr[rs, hs] = (u[rs, hs] * s).astype(BF16)
    hout = _dot(su_scr[...], wout_ref[...]) + bout_ref[...]
    o_ref[...] = x + _rms(hout, gpost_ref[...])


def _gmlp_mixer(x, gpre, gpost, w_in, b_in, ln_g, ln_b, w_s, b_s, w_out, b_out, tm=256):
    L, D = x.shape
    E = ln_g.shape[0]
    full2 = lambda shape: pl.BlockSpec(shape, lambda i: (0, 0))
    return pl.pallas_call(
        _gmlp_body,
        grid=(L // tm,),
        in_specs=[
            pl.BlockSpec((tm, D), lambda i: (i, 0)),
            full2((1, D)), full2((1, D)), full2((D, 2 * E)), full2((1, 2 * E)), full2((1, E)), full2((1, E)),
            pl.BlockSpec(w_s.shape, lambda i: (0, 0, 0)), full2((GM_CHUNK, GM_HEADS)),
            full2((E, D)), full2((1, D)),
        ],
        out_specs=pl.BlockSpec((tm, D), lambda i: (i, 0)),
        out_shape=jax.ShapeDtypeStruct((L, D), F32),
        scratch_shapes=[pltpu.VMEM((tm, E), BF16)],
        compiler_params=_params("parallel"),
        name="gmlp_mixer",
    )(x, _row(gpre), _row(gpost), w_in.astype(BF16), _row(b_in), _row(ln_g), _row(ln_b),
      w_s, b_s.T, w_out.astype(BF16), _row(b_out))


def _qkv_body(x_ref, g_ref, w_ref, o_ref, xt_scr, xs_scr):
    d, rows, _ = o_ref.shape
    if d == 1:
        x = x_ref[...]
    else:
        for c in range(xt_scr.shape[0]):
            cs = slice(c * LANES, (c + 1) * LANES)
            xt_scr[c] = x_ref[:, cs]
            for r in range(d):
                xs_scr[r * rows:(r + 1) * rows, cs] = xt_scr[c, pl.ds(r, rows, stride=d), :]
        x = xs_scr[...]
    res = _dot(_rms(x, g_ref[...]).astype(BF16), w_ref[...])
    for r in range(d):
        o_ref[r] = res[r * rows:(r + 1) * rows, :].astype(o_ref.dtype)


def _qkv_proj(x, g, w, dilation, tm=512):
    L, D = x.shape
    N = w.shape[1]
    return pl.pallas_call(
        _qkv_body,
        grid=(L // tm,),
        in_specs=[
            pl.BlockSpec((tm, D), lambda i: (i, 0)),
            pl.BlockSpec((1, D), lambda i: (0, 0)),
            pl.BlockSpec((D, N), lambda i: (0, 0)),
        ],
        out_specs=pl.BlockSpec((dilation, tm // dilation, N), lambda i: (0, i, 0)),
        out_shape=jax.ShapeDtypeStruct((dilation, L // dilation, N), BF16),
        scratch_shapes=[pltpu.VMEM((D // LANES, tm, LANES), F32), pltpu.VMEM((tm, D), F32)],
        compiler_params=_params("parallel"),
        name=f"qkv_proj_d{dilation}",
    )(x, _row(g), w.astype(BF16))


def _t5_buckets(dilation):
    delta = (np.arange(BLOCK)[:, None] + BLOCK) - np.arange(2 * BLOCK)[None, :]
    dist = np.maximum(delta, 0) * dilation
    max_exact = NUM_BUCKETS // 2
    distf = np.maximum(dist, 1).astype(np.float32)
    large = max_exact + (np.log(distf / np.float32(max_exact)) / np.float32(math.log(MAX_DISTANCE / max_exact))
                         * np.float32(NUM_BUCKETS - max_exact)).astype(np.int32)
    large = np.minimum(large, NUM_BUCKETS - 1)
    return np.where(dist < max_exact, dist, large).astype(np.int32)


def _attn_body(tab_ref, bkt_ref, q_ref, kc_ref, vc_ref, o_ref, lse_ref, bias_scr, k2, v2, s_scr, p_scr, *, scale):
    blk = q_ref.shape[0]
    n_heads = bias_scr.shape[1]
    hd = q_ref.shape[1] // n_heads
    n = pl.program_id(1)

    @pl.when((pl.program_id(0) == 0) & (n == 0))
    def _():
        bkt = bkt_ref[...]
        qi = lax.broadcasted_iota(jnp.int32, bkt.shape, 0)
        ki = lax.broadcasted_iota(jnp.int32, bkt.shape, 1)
        delta = qi + blk - ki
        in_band = (delta >= 0) & (delta <= blk)
        for h in range(n_heads):
            acc = jnp.zeros(bkt.shape, F32)
            for b in range(tab_ref.shape[0]):
                acc = jnp.where(bkt == b, tab_ref[b, h], acc)
            acc = jnp.where(in_band, acc, NEG_BIG)
            bias_scr[1, h] = acc
            bias_scr[0, h] = jnp.where(ki >= blk, acc, NEG_BIG)

    @pl.when(n == 0)
    def _():
        k2[0:blk, :] = jnp.zeros((blk, k2.shape[1]), k2.dtype)
        v2[0:blk, :] = jnp.zeros((blk, v2.shape[1]), v2.dtype)

    k2[blk:2 * blk, :] = kc_ref[...]
    v2[blk:2 * blk, :] = vc_ref[...]
    slot = jnp.minimum(n, 1)
    dims = (((1,), (1,)), ((), ()))
    per_tile = LANES // hd
    lane = lax.broadcasted_iota(jnp.int32, (1, LANES), 1)
    own = [(lane >= i * hd) & (lane < (i + 1) * hd) for i in range(per_tile)]
    n_tiles = n_heads // per_tile

    for t in range(n_tiles):
        cs = slice(t * LANES, (t + 1) * LANES)
        qt = q_ref[:, cs] * scale
        kt = k2[:, cs]
        for i in range(per_tile):
            h = t * per_tile + i
            qh = jnp.where(own[i], qt, jnp.zeros_like(qt))
            s_scr[h] = lax.dot_general(qh, kt, dims, preferred_element_type=F32) + bias_scr[slot, h]

    for t in range(n_tiles):
        lse_t = jnp.zeros((blk, LANES), F32)
        for i in range(per_tile):
            h = t * per_tile + i
            s = s_scr[h]
            m = jnp.max(s, axis=-1, keepdims=True)
            e = jnp.exp(s - m)
            den = jnp.sum(e, axis=-1, keepdims=True)
            p_scr[h] = (e * (1.0 / den)).astype(BF16)
            lse_t = jnp.where(own[i], m + jnp.log(den), lse_t)
        lse_ref[:, t * LANES:(t + 1) * LANES] = lse_t

    for t in range(n_tiles):
        cs = slice(t * LANES, (t + 1) * LANES)
        vt = v2[:, cs]
        acc = jnp.zeros((blk, LANES), F32)
        for i in range(per_tile):
            acc = acc + _dot(p_scr[t * per_tile + i], jnp.where(own[i], vt, jnp.zeros_like(vt)))
        o_ref[:, cs] = acc
    k2[0:blk, :] = k2[blk:2 * blk, :]
    v2[0:blk, :] = v2[blk:2 * blk, :]


def _dilated_attention(qkv, table, dilation):
    d, rows, W = qkv.shape
    D = W // 3
    n_heads = D // HEAD_DIM
    blk = lambda c: pl.BlockSpec((None, BLOCK, D), lambda r, n: (r, n, c))
    return pl.pallas_call(
        functools.partial(_attn_body, scale=HEAD_DIM ** -0.5),
        grid=(d, rows // BLOCK),
        in_specs=[
            pl.BlockSpec(memory_space=pltpu.SMEM),
            pl.BlockSpec((BLOCK, 2 * BLOCK), lambda r, n: (0, 0)),
            blk(0), blk(1), blk(2),
        ],
        out_specs=[blk(0)] * 2,
        out_shape=[jax.ShapeDtypeStruct((d, rows, D), F32)] * 2,
        scratch_shapes=[pltpu.VMEM((2, n_heads, BLOCK, 2 * BLOCK), F32),
                        pltpu.VMEM((2 * BLOCK, D), BF16), pltpu.VMEM((2 * BLOCK, D), BF16),
                        pltpu.VMEM((n_heads, BLOCK, 2 * BLOCK), F32),
                        pltpu.VMEM((n_heads, BLOCK, 2 * BLOCK), BF16)],
        compiler_params=_params("arbitrary", "arbitrary"),
        name=f"dilated_attn_d{dilation}",
    )(table, jnp.asarray(_t5_buckets(dilation)), qkv, qkv, qkv)


def _attn_out_body(x_ref, gpost_ref, o0_ref, o1_ref, o2_ref, l0_ref, l1_ref, l2_ref, wout_ref, out_ref, *scr):
    def natural(ref, s):
        d, rows, _ = ref.shape
        if d == 1:
            return ref[0]
        for c in range(s.shape[0]):
            for r in range(d):
                s[c, pl.ds(r, rows, stride=d), :] = ref[r, :, c * LANES:(c + 1) * LANES]
        return jnp.concatenate([s[c] for c in range(s.shape[0])], axis=-1)

    l0, l1, l2 = natural(l0_ref, None), natural(l1_ref, scr[0]), natural(l2_ref, scr[1])
    m = jnp.maximum(jnp.maximum(l0, l1), l2)
    e0, e1, e2 = jnp.exp(l0 - m), jnp.exp(l1 - m), jnp.exp(l2 - m)
    o = (e0 * natural(o0_ref, None) + e1 * natural(o1_ref, scr[2]) + e2 * natural(o2_ref, scr[3])) / (e0 + e1 + e2)
    h = _dot(o.astype(BF16), wout_ref[...])
    out_ref[...] = x_ref[...] + _rms(h, gpost_ref[...])


def _attention_mixer(x, gpre, gpost, w_qkv, w_out, rel_bias, tm=512):
    L, D = x.shape
    n_heads = D // HEAD_DIM
    outs, lses = [], []
    for g, (window, dilation) in enumerate(PATTERNS):
        assert window // dilation == BLOCK and L % (BLOCK * dilation) == 0
        qkv = _qkv_proj(x, gpre, w_qkv[:, 3 * D * g:3 * D * (g + 1)], dilation)
        o, lse = _dilated_attention(qkv, rel_bias[:, g * n_heads:(g + 1) * n_heads], dilation)
        outs.append(o)
        lses.append(lse)
    tile = pl.BlockSpec((tm, D), lambda i: (i, 0))
    grouped = [pl.BlockSpec((d, tm // d, D), lambda i: (0, i, 0)) for _, d in PATTERNS]
    return pl.pallas_call(
        _attn_out_body,
        grid=(L // tm,),
        in_specs=[tile, pl.BlockSpec((1, D), lambda i: (0, 0))] + grouped * 2
                 + [pl.BlockSpec((D, D), lambda i: (0, 0))],
        out_specs=tile,
        out_shape=jax.ShapeDtypeStruct((L, D), F32),
        scratch_shapes=[pltpu.VMEM((D // LANES, tm, LANES), F32)] * 4,
        compiler_params=_params("parallel"),
        name="attn_out",
    )(x, _row(gpost), *outs, *lses, w_out.astype(BF16))


def kernel(x, norm_pre, norm_post, ffn_w1, ffn_w3, ffn_w2, rel_bias, s5_w_in, s5_a_re, s5_a_im, s5_log_dt, s5_b_re, s5_b_im, s5_c_re, s5_c_im, s5_d, s5_w_glu, s5_b_glu, s5_w_out, cv_w_in, cv_b_in, cv_dw, cv_dw_b, cv_ln_g, cv_ln_b, cv_w_out, cv_b_out, gm_w_in, gm_b_in, gm_ln_g, gm_ln_b, gm_w_s, gm_b_s, gm_w_out, gm_b_out, at_w_qkv, at_w_out):
    bsz, seq, d_model = x.shape
    assert bsz == 1, "sequence mixers carry state along the row axis of one sequence"
    depth = norm_pre.shape[0]
    n_mixers = 4
    h = x.reshape(seq, d_model)
    for i in range(depth):
        kind, j = i % n_mixers, i // n_mixers
        h = _ffn(h, norm_pre[i, 0], norm_post[i, 0], ffn_w1[i, 0], ffn_w3[i, 0], ffn_w2[i, 0])
        if kind == 0:
            h = _s5_mixer(h, norm_pre[i, 1], norm_post[i, 1], s5_w_in[j], s5_a_re[j], s5_a_im[j], s5_log_dt[j],
                          s5_b_re[j], s5_b_im[j], s5_c_re[j], s5_c_im[j], s5_d[j], s5_w_glu[j], s5_b_glu[j],
                          s5_w_out[j])
        elif kind == 1:
            h = _conv_mixer(h, norm_pre[i, 1], norm_post[i, 1], cv_w_in[j], cv_b_in[j], cv_dw[j], cv_dw_b[j],
                            cv_ln_g[j], cv_ln_b[j], cv_w_out[j], cv_b_out[j])
        elif kind == 2:
            h = _gmlp_mixer(h, norm_pre[i, 1], norm_post[i, 1], gm_w_in[j], gm_b_in[j], gm_ln_g[j], gm_ln_b[j],
                            gm_w_s[j], gm_b_s[j], gm_w_out[j], gm_b_out[j])
        else:
            h = _attention_mixer(h, norm_pre[i, 1], norm_post[i, 1], at_w_qkv[j], at_w_out[j], rel_bias)
        h = _ffn(h, norm_pre[i, 2], norm_post[i, 2], ffn_w1[i, 1], ffn_w3[i, 1], ffn_w2[i, 1])
    return h.reshape(bsz, seq, d_model)
```

```python
import functools
import math

import numpy as np
import jax
import jax.numpy as jnp
from jax import lax
from jax.experimental import pallas as pl
from jax.experimental.pallas import tpu as pltpu

F32 = jnp.float32
BF16 = jnp.bfloat16

EPS = 1e-6
LANES = 128
SUBLANES = 8
S5_GROUP = 16
S5_STATE = 64
S5_GROUPS_PER_TILE = 8
CONV_W = 31
CONV_HALO = 32
GM_CHUNK = 128
GM_HEADS = 8
HEAD_DIM = 64
PATTERNS = ((128, 1), (512, 4), (2048, 16))
BLOCK = 128
NUM_BUCKETS = 32
MAX_DISTANCE = 2048
NEG_BIG = -1e30

VMEM_LIMIT_BYTES = 56 * 1024 * 1024


def _params(*sem):
    return pltpu.CompilerParams(dimension_semantics=sem, vmem_limit_bytes=VMEM_LIMIT_BYTES)


def _rms(x, g):
    return x * lax.rsqrt(jnp.mean(x * x, axis=-1, keepdims=True) + EPS) * g


def _layernorm(x, g, b):
    mu = jnp.mean(x, axis=-1, keepdims=True)
    xc = x - mu
    var = jnp.mean(xc * xc, axis=-1, keepdims=True)
    return xc * lax.rsqrt(var + EPS) * g + b


def _dot(a, b):
    return jnp.dot(a, b, preferred_element_type=F32)


def _row(v):
    return v.reshape(1, -1)


def _ffn_body(x_ref, gpre_ref, gpost_ref, w1_ref, w3_ref, w2_ref, o_ref, g_scr, *, fc):
    n_sub, sub, F = g_scr.shape
    for s in range(n_sub):
        rs = slice(s * sub, (s + 1) * sub)
        xn = _rms(x_ref[rs, :], gpre_ref[...]).astype(BF16)
        for c in range(F // fc):
            cs = slice(c * fc, (c + 1) * fc)
            h1 = _dot(xn, w1_ref[:, cs])
            h3 = _dot(xn, w3_ref[:, cs])
            g_scr[s, :, cs] = ((h1 * jax.nn.sigmoid(h1)) * h3).astype(BF16)
        h = _dot(g_scr[s], w2_ref[...])
        o_ref[rs, :] = x_ref[rs, :] + 0.5 * _rms(h, gpost_ref[...])


def _resident(shape):
    return pl.BlockSpec(shape, lambda i: (0,) * len(shape), pipeline_mode=pl.Buffered(1))


def _ffn(x, gpre, gpost, w1, w3, w2, tm=1024, sub=512, fc=256):
    L, D = x.shape
    F = w1.shape[1]
    return pl.pallas_call(
        functools.partial(_ffn_body, fc=fc),
        grid=(L // tm,),
        in_specs=[
            pl.BlockSpec((tm, D), lambda i: (i, 0)),
            _resident((1, D)), _resident((1, D)),
            _resident((D, F)), _resident((D, F)), _resident((F, D)),
        ],
        out_specs=pl.BlockSpec((tm, D), lambda i: (i, 0)),
        out_shape=jax.ShapeDtypeStruct((L, D), F32),
        scratch_shapes=[pltpu.VMEM((tm // sub, sub, F), BF16)],
        compiler_params=_params("parallel"),
        name="ffn",
    )(x, _row(gpre), _row(gpost), w1.astype(BF16), w3.astype(BF16), w2.astype(BF16))


def _s5_body(x_ref, gpre_ref, gpost_ref, win_ref, are_ref, aim_ref, ldt_ref, bdre_ref, bdim_ref,
             cdre_ref, cdim_ref, d_ref, wglu_ref, bglu_ref, wout_ref, o_ref,
             abar_re, abar_im, bb_re, bb_im, st_re, st_im, bu_re, bu_im, y_scr, *, unroll):
    tt = x_ref.shape[0]
    n_tiles, uw, sw = bdre_ref.shape
    nj = sw // LANES

    @pl.when(pl.program_id(0) == 0)
    def _():
        ar = are_ref[...]
        ai = aim_ref[...]
        dt = jnp.exp(ldt_ref[...])
        mag = jnp.exp(dt * ar)
        abr = mag * jnp.cos(dt * ai)
        abi = mag * jnp.sin(dt * ai)
        den = ar * ar + ai * ai
        nr = abr - 1.0
        fre = (nr * ar + abi * ai) / den
        fim = (abi * ar - nr * ai) / den
        abar_re[...] = abr
        abar_im[...] = abi
        for k in range(n_tiles):
            fr = fre[k:k + 1, :]
            fi = fim[k:k + 1, :]
            bb_re[k] = (fr * bdre_ref[k] - fi * bdim_ref[k]).astype(BF16)
            bb_im[k] = (fr * bdim_ref[k] + fi * bdre_ref[k]).astype(BF16)
        st_re[...] = jnp.zeros_like(st_re)
        st_im[...] = jnp.zeros_like(st_im)

    x = x_ref[...]
    xn = _rms(x, gpre_ref[...]).astype(BF16)
    u = _dot(xn, win_ref[...])
    ub = u.astype(BF16)
    for k in range(n_tiles):
        uk = ub[:, k * uw:(k + 1) * uw]
        br = _dot(uk, bb_re[k])
        bi = _dot(uk, bb_im[k])
        for j in range(nj):
            bu_re[j, pl.ds(k, tt, stride=n_tiles), :] = br[:, j * LANES:(j + 1) * LANES]
            bu_im[j, pl.ds(k, tt, stride=n_tiles), :] = bi[:, j * LANES:(j + 1) * LANES]

    a_r = [abar_re[:, j * LANES:(j + 1) * LANES] for j in range(nj)]
    a_i = [abar_im[:, j * LANES:(j + 1) * LANES] for j in range(nj)]

    def step(t, carry):
        rows = pl.ds(pl.multiple_of(t * n_tiles, n_tiles), n_tiles)
        out = []
        for j in range(nj):
            sr, si = carry[2 * j], carry[2 * j + 1]
            nr = a_r[j] * sr - a_i[j] * si + bu_re[j, rows, :]
            ni = a_r[j] * si + a_i[j] * sr + bu_im[j, rows, :]
            bu_re[j, rows, :] = nr
            bu_im[j, rows, :] = ni
            out += [nr, ni]
        return tuple(out)

    init = tuple(ref[j] for j in range(nj) for ref in (st_re, st_im))
    last = lax.fori_loop(0, tt, step, init, unroll=unroll)
    for j in range(nj):
        st_re[j] = last[2 * j]
        st_im[j] = last[2 * j + 1]

    for k in range(n_tiles):
        sre = jnp.concatenate([bu_re[j, pl.ds(k, tt, stride=n_tiles), :] for j in range(nj)], axis=-1)
        sim = jnp.concatenate([bu_im[j, pl.ds(k, tt, stride=n_tiles), :] for j in range(nj)], axis=-1)
        y_scr[:, k * uw:(k + 1) * uw] = _dot(sre.astype(BF16), cdre_ref[k]) - _dot(sim.astype(BF16), cdim_ref[k])
    y = jax.nn.gelu(y_scr[...] + d_ref[...] * u)
    z = y * jax.nn.sigmoid(_dot(y.astype(BF16), wglu_ref[...]) + bglu_ref[...])
    h = _dot(z.astype(BF16), wout_ref[...])
    o_ref[...] = x + _rms(h, gpost_ref[...])


def _s5_block_diag(b, c):
    G, P, HG = b.shape
    gt = S5_GROUPS_PER_TILE
    eye = jnp.eye(gt, dtype=b.dtype)
    bt = b.reshape(G // gt, gt, P, HG).transpose(0, 1, 3, 2)
    bd = jnp.einsum('kghp,gj->kghjp', bt, eye).reshape(G // gt, gt * HG, gt * P)
    ct = c.reshape(G // gt, gt, HG, P)
    cd = jnp.einsum('kghp,gj->kjpgh', ct, eye).reshape(G // gt, gt * P, gt * HG)
    return bd, cd


def _s5_mixer(x, gpre, gpost, w_in, a_re, a_im, log_dt, b_re, b_im, c_re, c_im, d_skip, w_glu, b_glu,
              w_out, tt=256, unroll=4):
    L, D = x.shape
    G, P = a_re.shape
    bd_re, cd_re = _s5_block_diag(b_re, c_re)
    bd_im, cd_im = _s5_block_diag(b_im, c_im)
    n_tiles, uw, sw = bd_re.shape
    assert n_tiles == SUBLANES and uw == LANES and sw % LANES == 0
    nj = sw // LANES
    per_tile = lambda v: v.reshape(n_tiles, sw)
    return pl.pallas_call(
        functools.partial(_s5_body, unroll=unroll),
        grid=(L // tt,),
        in_specs=[
            pl.BlockSpec((tt, D), lambda i: (i, 0)),
            _resident((1, D)), _resident((1, D)), _resident((D, D)),
            _resident((n_tiles, sw)), _resident((n_tiles, sw)), _resident((n_tiles, sw)),
            _resident((n_tiles, uw, sw)), _resident((n_tiles, uw, sw)),
            _resident((n_tiles, sw, uw)), _resident((n_tiles, sw, uw)),
            _resident((1, D)), _resident((D, D)), _resident((1, D)), _resident((D, D)),
        ],
        out_specs=pl.BlockSpec((tt, D), lambda i: (i, 0)),
        out_shape=jax.ShapeDtypeStruct((L, D), F32),
        scratch_shapes=[
            pltpu.VMEM((n_tiles, sw), F32), pltpu.VMEM((n_tiles, sw), F32),
            pltpu.VMEM((n_tiles, uw, sw), BF16), pltpu.VMEM((n_tiles, uw, sw), BF16),
            pltpu.VMEM((nj, n_tiles, LANES), F32), pltpu.VMEM((nj, n_tiles, LANES), F32),
            pltpu.VMEM((nj, tt * n_tiles, LANES), F32), pltpu.VMEM((nj, tt * n_tiles, LANES), F32),
            pltpu.VMEM((tt, D), F32),
        ],
        compiler_params=_params("arbitrary"),
        name="s5_mixer",
    )(x, _row(gpre), _row(gpost), w_in.astype(BF16),
      per_tile(a_re), per_tile(a_im), per_tile(jnp.broadcast_to(log_dt[:, None], (G, P))),
      bd_re, bd_im, cd_re.astype(BF16), cd_im.astype(BF16),
      _row(d_skip), w_glu.astype(BF16), _row(b_glu), w_out.astype(BF16))


def _conv_body(x_ref, gpre_ref, gpost_ref, wa_ref, wg_ref, ba_ref, bg_ref, dw_ref, dwb_ref,
               lng_ref, lnb_ref, wout_ref, bout_ref, o_ref, ext, conv_scr):
    tm, D = x_ref.shape
    halo = ext.shape[0] - tm
    n_taps = dw_ref.shape[0]

    @pl.when(pl.program_id(0) == 0)
    def _():
        ext[0:halo, :] = jnp.zeros((halo, D), F32)

    x = x_ref[...]
    xn = _rms(x, gpre_ref[...]).astype(BF16)
    a = _dot(xn, wa_ref[...]) + ba_ref[...]
    g = _dot(xn, wg_ref[...]) + bg_ref[...]
    ext[halo:halo + tm, :] = a * jax.nn.sigmoid(g)

    base = halo - (n_taps - 1)
    for c in range(D // LANES):
        cs = slice(c * LANES, (c + 1) * LANES)
        acc = None
        for s in range(SUBLANES):
            rows = tm if s == 0 else tm + SUBLANES
            q = None
            for a in range((base + n_taps - 1) // SUBLANES + 1):
                k = SUBLANES * a + s - base
                if 0 <= k < n_taps:
                    term = dw_ref[k:k + 1, cs] * ext[SUBLANES * a:SUBLANES * a + rows, cs]
                    q = term if q is None else q + term
            part = q if s == 0 else q[s:s + tm]
            acc = part if acc is None else acc + part
        conv_scr[:, cs] = acc
    ext[0:halo, :] = ext[tm:tm + halo, :]

    y = _layernorm(conv_scr[...] + dwb_ref[...], lng_ref[...], lnb_ref[...])
    y = y * jax.nn.sigmoid(y)
    h = _dot(y.astype(BF16), wout_ref[...]) + bout_ref[...]
    o_ref[...] = x + _rms(h, gpost_ref[...])


def _conv_mixer(x, gpre, gpost, w_in, b_in, dw, dw_b, ln_g, ln_b, w_out, b_out, tm=256):
    L, D = x.shape
    full2 = lambda shape: pl.BlockSpec(shape, lambda i: (0, 0))
    return pl.pallas_call(
        _conv_body,
        grid=(L // tm,),
        in_specs=[
            pl.BlockSpec((tm, D), lambda i: (i, 0)),
            full2((1, D)), full2((1, D)), full2((D, D)), full2((D, D)), full2((1, D)), full2((1, D)),
            full2(dw.shape), full2((1, D)), full2((1, D)), full2((1, D)), full2((D, D)), full2((1, D)),
        ],
        out_specs=pl.BlockSpec((tm, D), lambda i: (i, 0)),
        out_shape=jax.ShapeDtypeStruct((L, D), F32),
        scratch_shapes=[pltpu.VMEM((tm + CONV_HALO, D), F32), pltpu.VMEM((tm, D), F32)],
        compiler_params=_params("arbitrary"),
        name="conv_mixer",
    )(x, _row(gpre), _row(gpost), w_in[:, :D].astype(BF16), w_in[:, D:].astype(BF16),
      _row(b_in[:D]), _row(b_in[D:]), dw, _row(dw_b), _row(ln_g), _row(ln_b),
      w_out.astype(BF16), _row(b_out))


def _gmlp_body(x_ref, gpre_ref, gpost_ref, win_ref, bin_ref, lng_ref, lnb_ref, ws_ref, bs_ref,
               wout_ref, bout_ref, o_ref, su_scr):
    tm = x_ref.shape[0]
    E = lng_ref.shape[1]
    n_heads, chunk, _ = ws_ref.shape
    hw = E // n_heads
    x = x_ref[...]
    xn = _rms(x, gpre_ref[...]).astype(BF16)
    z = jax.nn.gelu(_dot(xn, win_ref[...]) + bin_ref[...])
    u = z[:, :E]
    vb = _layernorm(z[:, E:], lng_ref[...], lnb_ref[...]).astype(BF16)
    causal = (lax.broadcasted_iota(jnp.int32, (chunk, chunk), 1)
              <= lax.broadcasted_iota(jnp.int32, (chunk, chunk), 0))
    for h in range(n_heads):
        wm = jnp.where(causal, ws_ref[h], 0.0).astype(BF16)
        bias = bs_ref[:, h:h + 1]
        for c in range(tm // chunk):
            rs = slice(c * chunk, (c + 1) * chunk)
            hs = slice(h * hw, (h + 1) * hw)
            s = _dot(wm, vb[rs, hs]) + bias
            su_scr[rs, hs] = (u[rs, hs] * s).astype(BF16)
    hout = _dot(su_scr[...], wout_ref[...]) + bout_ref[...]
    o_ref[...] = x + _rms(hout, gpost_ref[...])


def _gmlp_mixer(x, gpre, gpost, w_in, b_in, ln_g, ln_b, w_s, b_s, w_out, b_out, tm=256):
    L, D = x.shape
    E = ln_g.shape[0]
    full2 = lambda shape: pl.BlockSpec(shape, lambda i: (0, 0))
    return pl.pallas_call(
        _gmlp_body,
        grid=(L // tm,),
        in_specs=[
            pl.BlockSpec((tm, D), lambda i: (i, 0)),
            full2((1, D)), full2((1, D)), full2((D, 2 * E)), full2((1, 2 * E)), full2((1, E)), full2((1, E)),
            pl.BlockSpec(w_s.shape, lambda i: (0, 0, 0)), full2((GM_CHUNK, GM_HEADS)),
            full2((E, D)), full2((1, D)),
        ],
        out_specs=pl.BlockSpec((tm, D), lambda i: (i, 0)),
        out_shape=jax.ShapeDtypeStruct((L, D), F32),
        scratch_shapes=[pltpu.VMEM((tm, E), BF16)],
        compiler_params=_params("parallel"),
        name="gmlp_mixer",
    )(x, _row(gpre), _row(gpost), w_in.astype(BF16), _row(b_in), _row(ln_g), _row(ln_b),
      w_s, b_s.T, w_out.astype(BF16), _row(b_out))


def _qkv_body(x_ref, g_ref, w_ref, o_ref, xt_scr, xs_scr):
    d, rows, _ = o_ref.shape
    if d == 1:
        x = x_ref[...]
    else:
        for c in range(xt_scr.shape[0]):
            cs = slice(c * LANES, (c + 1) * LANES)
            xt_scr[c] = x_ref[:, cs]
            for r in range(d):
                xs_scr[r * rows:(r + 1) * rows, cs] = xt_scr[c, pl.ds(r, rows, stride=d), :]
        x = xs_scr[...]
    res = _dot(_rms(x, g_ref[...]).astype(BF16), w_ref[...])
    for r in range(d):
        o_ref[r] = res[r * rows:(r + 1) * rows, :].astype(o_ref.dtype)


def _qkv_proj(x, g, w, dilation, tm=512):
    L, D = x.shape
    N = w.shape[1]
    return pl.pallas_call(
        _qkv_body,
        grid=(L // tm,),
        in_specs=[
            pl.BlockSpec((tm, D), lambda i: (i, 0)),
            pl.BlockSpec((1, D), lambda i: (0, 0)),
            pl.BlockSpec((D, N), lambda i: (0, 0)),
        ],
        out_specs=pl.BlockSpec((dilation, tm // dilation, N), lambda i: (0, i, 0)),
        out_shape=jax.ShapeDtypeStruct((dilation, L // dilation, N), BF16),
        scratch_shapes=[pltpu.VMEM((D // LANES, tm, LANES), F32), pltpu.VMEM((tm, D), F32)],
        compiler_params=_params("parallel"),
        name=f"qkv_proj_d{dilation}",
    )(x, _row(g), w.astype(BF16))


def _t5_buckets(dilation):
    delta = (np.arange(BLOCK)[:, None] + BLOCK) - np.arange(2 * BLOCK)[None, :]
    dist = np.maximum(delta, 0) * dilation
    max_exact = NUM_BUCKETS // 2
    distf = np.maximum(dist, 1).astype(np.float32)
    large = max_exact + (np.log(distf / np.float32(max_exact)) / np.float32(math.log(MAX_DISTANCE / max_exact))
                         * np.float32(NUM_BUCKETS - max_exact)).astype(np.int32)
    large = np.minimum(large, NUM_BUCKETS - 1)
    return np.where(dist < max_exact, dist, large).astype(np.int32)


def _attn_body(tab_ref, bkt_ref, q_ref, kc_ref, vc_ref, o_ref, lse_ref, bias_scr, k2, v2, s_scr, p_scr, *, scale):
    blk = q_ref.shape[0]
    n_heads = bias_scr.shape[1]
    hd = q_ref.shape[1] // n_heads
    n = pl.program_id(1)

    @pl.when((pl.program_id(0) == 0) & (n == 0))
    def _():
        bkt = bkt_ref[...]
        qi = lax.broadcasted_iota(jnp.int32, bkt.shape, 0)
        ki = lax.broadcasted_iota(jnp.int32, bkt.shape, 1)
        delta = qi + blk - ki
        in_band = (delta >= 0) & (delta <= blk)
        for h in range(n_heads):
            acc = jnp.zeros(bkt.shape, F32)
            for b in range(tab_ref.shape[0]):
                acc = jnp.where(bkt == b, tab_ref[b, h], acc)
            acc = jnp.where(in_band, acc, NEG_BIG)
            bias_scr[1, h] = acc
            bias_scr[0, h] = jnp.where(ki >= blk, acc, NEG_BIG)

    @pl.when(n == 0)
    def _():
        k2[0:blk, :] = jnp.zeros((blk, k2.shape[1]), k2.dtype)
        v2[0:blk, :] = jnp.zeros((blk, v2.shape[1]), v2.dtype)

    k2[blk:2 * blk, :] = kc_ref[...]
    v2[blk:2 * blk, :] = vc_ref[...]
    slot = jnp.minimum(n, 1)
    dims = (((1,), (1,)), ((), ()))
    per_tile = LANES // hd
    lane = lax.broadcasted_iota(jnp.int32, (1, LANES), 1)
    own = [(lane >= i * hd) & (lane < (i + 1) * hd) for i in range(per_tile)]
    n_tiles = n_heads // per_tile

    for t in range(n_tiles):
        cs = slice(t * LANES, (t + 1) * LANES)
        qt = q_ref[:, cs] * scale
        kt = k2[:, cs]
        for i in range(per_tile):
            h = t * per_tile + i
            qh = jnp.where(own[i], qt, jnp.zeros_like(qt))
            s_scr[h] = lax.dot_general(qh, kt, dims, preferred_element_type=F32) + bias_scr[slot, h]

    for t in range(n_tiles):
        lse_t = jnp.zeros((blk, LANES), F32)
        for i in range(per_tile):
            h = t * per_tile + i
            s = s_scr[h]
            m = jnp.max(s, axis=-1, keepdims=True)
            e = jnp.exp(s - m)
            den = jnp.sum(e, axis=-1, keepdims=True)
            p_scr[h] = (e * (1.0 / den)).astype(BF16)
            lse_t = jnp.where(own[i], m + jnp.log(den), lse_t)
        lse_ref[:, t * LANES:(t + 1) * LANES] = lse_t

    for t in range(n_tiles):
        cs = slice(t * LANES, (t + 1) * LANES)
        vt = v2[:, cs]
        acc = jnp.zeros((blk, LANES), F32)
        for i in range(per_tile):
            acc = acc + _dot(p_scr[t * per_tile + i], jnp.where(own[i], vt, jnp.zeros_like(vt)))
        o_ref[:, cs] = acc
    k2[0:blk, :] = k2[blk:2 * blk, :]
    v2[0:blk, :] = v2[blk:2 * blk, :]


def _dilated_attention(qkv, table, dilation):
    d, rows, W = qkv.shape
    D = W // 3
    n_heads = D // HEAD_DIM
    blk = lambda c: pl.BlockSpec((None, BLOCK, D), lambda r, n: (r, n, c))
    return pl.pallas_call(
        functools.partial(_attn_body, scale=HEAD_DIM ** -0.5),
        grid=(d, rows // BLOCK),
        in_specs=[
            pl.BlockSpec(memory_space=pltpu.SMEM),
            pl.BlockSpec((BLOCK, 2 * BLOCK), lambda r, n: (0, 0)),
            blk(0), blk(1), blk(2),
        ],
        out_specs=[blk(0)] * 2,
        out_shape=[jax.ShapeDtypeStruct((d, rows, D), F32)] * 2,
        scratch_shapes=[pltpu.VMEM((2, n_heads, BLOCK, 2 * BLOCK), F32),
                        pltpu.VMEM((2 * BLOCK, D), BF16), pltpu.VMEM((2 * BLOCK, D), BF16),
                        pltpu.VMEM((n_heads, BLOCK, 2 * BLOCK), F32),
                        pltpu.VMEM((n_heads, BLOCK, 2 * BLOCK), BF16)],
        compiler_params=_params("arbitrary", "arbitrary"),
        name=f"dilated_attn_d{dilation}",
    )(table, jnp.asarray(_t5_buckets(dilation)), qkv, qkv, qkv)


def _attn_out_body(x_ref, gpost_ref, o0_ref, o1_ref, o2_ref, l0_ref, l1_ref, l2_ref, wout_ref, out_ref, *scr):
    def natural(ref, s):
        d, rows, _ = ref.shape
        if d == 1:
            return ref[0]
        for c in range(s.shape[0]):
            for r in range(d):
                s[c, pl.ds(r, rows, stride=d), :] = ref[r, :, c * LANES:(c + 1) * LANES]
        return jnp.concatenate([s[c] for c in range(s.shape[0])], axis=-1)

    l0, l1, l2 = natural(l0_ref, None), natural(l1_ref, scr[0]), natural(l2_ref, scr[1])
    m = jnp.maximum(jnp.maximum(l0, l1), l2)
    e0, e1, e2 = jnp.exp(l0 - m), jnp.exp(l1 - m), jnp.exp(l2 - m)
    o = (e0 * natural(o0_ref, None) + e1 * natural(o1_ref, scr[2]) + e2 * natural(o2_ref, scr[3])) / (e0 + e1 + e2)
    h = _dot(o.astype(BF16), wout_ref[...])
    out_ref[...] = x_ref[...] + _rms(h, gpost_ref[...])


def _attention_mixer(x, gpre, gpost, w_qkv, w_out, rel_bias, tm=512):
    L, D = x.shape
    n_heads = D // HEAD_DIM
    outs, lses = [], []
    for g, (window, dilation) in enumerate(PATTERNS):
        assert window // dilation == BLOCK and L % (BLOCK * dilation) == 0
        qkv = _qkv_proj(x, gpre, w_qkv[:, 3 * D * g:3 * D * (g + 1)], dilation)
        o, lse = _dilated_attention(qkv, rel_bias[:, g * n_heads:(g + 1) * n_heads], dilation)
        outs.append(o)
        lses.append(lse)
    tile = pl.BlockSpec((tm, D), lambda i: (i, 0))
    grouped = [pl.BlockSpec((d, tm // d, D), lambda i: (0, i, 0)) for _, d in PATTERNS]
    return pl.pallas_call(
        _attn_out_body,
        grid=(L // tm,),
        in_specs=[tile, pl.BlockSpec((1, D), lambda i: (0, 0))] + grouped * 2
                 + [pl.BlockSpec((D, D), lambda i: (0, 0))],
        out_specs=tile,
        out_shape=jax.ShapeDtypeStruct((L, D), F32),
        scratch_shapes=[pltpu.VMEM((D // LANES, tm, LANES), F32)] * 4,
        compiler_params=_params("parallel"),
        name="attn_out",
    )(x, _row(gpost), *outs, *lses, w_out.astype(BF16))


def kernel(x, norm_pre, norm_post, ffn_w1, ffn_w3, ffn_w2, rel_bias, s5_w_in, s5_a_re, s5_a_im, s5_log_dt, s5_b_re, s5_b_im, s5_c_re, s5_c_im, s5_d, s5_w_glu, s5_b_glu, s5_w_out, cv_w_in, cv_b_in, cv_dw, cv_dw_b, cv_ln_g, cv_ln_b, cv_w_out, cv_b_out, gm_w_in, gm_b_in, gm_ln_g, gm_ln_b, gm_w_s, gm_b_s, gm_w_out, gm_b_out, at_w_qkv, at_w_out):
    bsz, seq, d_model = x.shape
    assert bsz == 1, "sequence mixers carry state along the row axis of one sequence"
    depth = norm_pre.shape[0]
    n_mixers = 4
    h = x.reshape(seq, d_model)
    for i in range(depth):
        kind, j = i % n_mixers, i // n_mixers
        h = _ffn(h, norm_pre[i, 0], norm_post[i, 0], ffn_w1[i, 0], ffn_w3[i, 0], ffn_w2[i, 0])
        if kind == 0:
            h = _s5_mixer(h, norm_pre[i, 1], norm_post[i, 1], s5_w_in[j], s5_a_re[j], s5_a_im[j], s5_log_dt[j],
                          s5_b_re[j], s5_b_im[j], s5_c_re[j], s5_c_im[j], s5_d[j], s5_w_glu[j], s5_b_glu[j],
                          s5_w_out[j])
        elif kind == 1:
            h = _conv_mixer(h, norm_pre[i, 1], norm_post[i, 1], cv_w_in[j], cv_b_in[j], cv_dw[j], cv_dw_b[j],
                            cv_ln_g[j], cv_ln_b[j], cv_w_out[j], cv_b_out[j])
        elif kind == 2:
            h = _gmlp_mixer(h, norm_pre[i, 1], norm_post[i, 1], gm_w_in[j], gm_b_in[j], gm_ln_g[j], gm_ln_b[j],
                            gm_w_s[j], gm_b_s[j], gm_w_out[j], gm_b_out[j])
        else:
            h = _attention_mixer(h, norm_pre[i, 1], norm_post[i, 1], at_w_qkv[j], at_w_out[j], rel_bias)
        h = _ffn(h, norm_pre[i, 2], norm_post[i, 2], ffn_w1[i, 1], ffn_w3[i, 1], ffn_w2[i, 1])
    return h.reshape(bsz, seq, d_model)
```

```python
import functools
import math

import numpy as np
import jax
import jax.numpy as jnp
from jax import lax
from jax.experimental import pallas as pl
from jax.experimental.pallas import tpu as pltpu

F32 = jnp.float32
BF16 = jnp.bfloat16

EPS = 1e-6
LANES = 128
SUBLANES = 8
S5_GROUP = 16
S5_STATE = 64
S5_GROUPS_PER_TILE = 8
CONV_W = 31
CONV_HALO = 32
GM_CHUNK = 128
GM_HEADS = 8
HEAD_DIM = 64
PATTERNS = ((128, 1), (512, 4), (2048, 16))
BLOCK = 128
NUM_BUCKETS = 32
MAX_DISTANCE = 2048
NEG_BIG = -1e30

VMEM_LIMIT_BYTES = 56 * 1024 * 1024


def _params(*sem):
    return pltpu.CompilerParams(dimension_semantics=sem, vmem_limit_bytes=VMEM_LIMIT_BYTES)


def _rms(x, g):
    return x * lax.rsqrt(jnp.mean(x * x, axis=-1, keepdims=True) + EPS) * g


def _layernorm(x, g, b):
    mu = jnp.mean(x, axis=-1, keepdims=True)
    xc = x - mu
    var = jnp.mean(xc * xc, axis=-1, keepdims=True)
    return xc * lax.rsqrt(var + EPS) * g + b


def _dot(a, b):
    return jnp.dot(a, b, preferred_element_type=F32)


def _row(v):
    return v.reshape(1, -1)


def _ffn_body(x_ref, gpre_ref, gpost_ref, w1_ref, w3_ref, w2_ref, o_ref, g_scr, *, fc):
    n_sub, sub, F = g_scr.shape
    for s in range(n_sub):
        rs = slice(s * sub, (s + 1) * sub)
        xn = _rms(x_ref[rs, :], gpre_ref[...]).astype(BF16)
        for c in range(F // fc):
            cs = slice(c * fc, (c + 1) * fc)
            h1 = _dot(xn, w1_ref[:, cs])
            h3 = _dot(xn, w3_ref[:, cs])
            g_scr[s, :, cs] = ((h1 * jax.nn.sigmoid(h1)) * h3).astype(BF16)
        h = _dot(g_scr[s], w2_ref[...])
        o_ref[rs, :] = x_ref[rs, :] + 0.5 * _rms(h, gpost_ref[...])


def _resident(shape):
    return pl.BlockSpec(shape, lambda i: (0,) * len(shape), pipeline_mode=pl.Buffered(1))


def _ffn(x, gpre, gpost, w1, w3, w2, tm=1024, sub=512, fc=256):
    L, D = x.shape
    F = w1.shape[1]
    return pl.pallas_call(
        functools.partial(_ffn_body, fc=fc),
        grid=(L // tm,),
        in_specs=[
            pl.BlockSpec((tm, D), lambda i: (i, 0)),
            _resident((1, D)), _resident((1, D)),
            _resident((D, F)), _resident((D, F)), _resident((F, D)),
        ],
        out_specs=pl.BlockSpec((tm, D), lambda i: (i, 0)),
        out_shape=jax.ShapeDtypeStruct((L, D), F32),
        scratch_shapes=[pltpu.VMEM((tm // sub, sub, F), BF16)],
        compiler_params=_params("parallel"),
        name="ffn",
    )(x, _row(gpre), _row(gpost), w1.astype(BF16), w3.astype(BF16), w2.astype(BF16))


def _s5_body(x_ref, gpre_ref, gpost_ref, win_ref, are_ref, aim_ref, ldt_ref, bdre_ref, bdim_ref,
             cdre_ref, cdim_ref, d_ref, wglu_ref, bglu_ref, wout_ref, o_ref,
             abar_re, abar_im, bb_re, bb_im, st_re, st_im, bu_re, bu_im, y_scr, *, unroll):
    tt = x_ref.shape[0]
    n_tiles, uw, sw = bdre_ref.shape
    nj = sw // LANES

    @pl.when(pl.program_id(0) == 0)
    def _():
        ar = are_ref[...]
        ai = aim_ref[...]
        dt = jnp.exp(ldt_ref[...])
        mag = jnp.exp(dt * ar)
        abr = mag * jnp.cos(dt * ai)
        abi = mag * jnp.sin(dt * ai)
        den = ar * ar + ai * ai
        nr = abr - 1.0
        fre = (nr * ar + abi * ai) / den
        fim = (abi * ar - nr * ai) / den
        abar_re[...] = abr
        abar_im[...] = abi
        for k in range(n_tiles):
            fr = fre[k:k + 1, :]
            fi = fim[k:k + 1, :]
            bb_re[k] = (fr * bdre_ref[k] - fi * bdim_ref[k]).astype(BF16)
            bb_im[k] = (fr * bdim_ref[k] + fi * bdre_ref[k]).astype(BF16)
        st_re[...] = jnp.zeros_like(st_re)
        st_im[...] = jnp.zeros_like(st_im)

    x = x_ref[...]
    xn = _rms(x, gpre_ref[...]).astype(BF16)
    u = _dot(xn, win_ref[...])
    ub = u.astype(BF16)
    for k in range(n_tiles):
        uk = ub[:, k * uw:(k + 1) * uw]
        br = _dot(uk, bb_re[k])
        bi = _dot(uk, bb_im[k])
        for j in range(nj):
            bu_re[j, pl.ds(k, tt, stride=n_tiles), :] = br[:, j * LANES:(j + 1) * LANES]
            bu_im[j, pl.ds(k, tt, stride=n_tiles), :] = bi[:, j * LANES:(j + 1) * LANES]

    a_r = [abar_re[:, j * LANES:(j + 1) * LANES] for j in range(nj)]
    a_i = [abar_im[:, j * LANES:(j + 1) * LANES] for j in range(nj)]

    def step(t, carry):
        rows = pl.ds(pl.multiple_of(t * n_tiles, n_tiles), n_tiles)
        out = []
        for j in range(nj):
            sr, si = carry[2 * j], carry[2 * j + 1]
            nr = a_r[j] * sr - a_i[j] * si + bu_re[j, rows, :]
            ni = a_r[j] * si + a_i[j] * sr + bu_im[j, rows, :]
            bu_re[j, rows, :] = nr
            bu_im[j, rows, :] = ni
            out += [nr, ni]
        return tuple(out)

    init = tuple(ref[j] for j in range(nj) for ref in (st_re, st_im))
    last = lax.fori_loop(0, tt, step, init, unroll=unroll)
    for j in range(nj):
        st_re[j] = last[2 * j]
        st_im[j] = last[2 * j + 1]

    for k in range(n_tiles):
        sre = jnp.concatenate([bu_re[j, pl.ds(k, tt, stride=n_tiles), :] for j in range(nj)], axis=-1)
        sim = jnp.concatenate([bu_im[j, pl.ds(k, tt, stride=n_tiles), :] for j in range(nj)], axis=-1)
        y_scr[:, k * uw:(k + 1) * uw] = _dot(sre.astype(BF16), cdre_ref[k]) - _dot(sim.astype(BF16), cdim_ref[k])
    y = jax.nn.gelu(y_scr[...] + d_ref[...] * u)
    z = y * jax.nn.sigmoid(_dot(y.astype(BF16), wglu_ref[...]) + bglu_ref[...])
    h = _dot(z.astype(BF16), wout_ref[...])
    o_ref[...] = x + _rms(h, gpost_ref[...])


def _s5_block_diag(b, c):
    G, P, HG = b.shape
    gt = S5_GROUPS_PER_TILE
    eye = jnp.eye(gt, dtype=b.dtype)
    bt = b.reshape(G // gt, gt, P, HG).transpose(0, 1, 3, 2)
    bd = jnp.einsum('kghp,gj->kghjp', bt, eye).reshape(G // gt, gt * HG, gt * P)
    ct = c.reshape(G // gt, gt, HG, P)
    cd = jnp.einsum('kghp,gj->kjpgh', ct, eye).reshape(G // gt, gt * P, gt * HG)
    return bd, cd


def _s5_mixer(x, gpre, gpost, w_in, a_re, a_im, log_dt, b_re, b_im, c_re, c_im, d_skip, w_glu, b_glu,
              w_out, tt=256, unroll=4):
    L, D = x.shape
    G, P = a_re.shape
    bd_re, cd_re = _s5_block_diag(b_re, c_re)
    bd_im, cd_im = _s5_block_diag(b_im, c_im)
    n_tiles, uw, sw = bd_re.shape
    assert n_tiles == SUBLANES and uw == LANES and sw % LANES == 0
    nj = sw // LANES
    per_tile = lambda v: v.reshape(n_tiles, sw)
    return pl.pallas_call(
        functools.partial(_s5_body, unroll=unroll),
        grid=(L // tt,),
        in_specs=[
            pl.BlockSpec((tt, D), lambda i: (i, 0)),
            _resident((1, D)), _resident((1, D)), _resident((D, D)),
            _resident((n_tiles, sw)), _resident((n_tiles, sw)), _resident((n_tiles, sw)),
            _resident((n_tiles, uw, sw)), _resident((n_tiles, uw, sw)),
            _resident((n_tiles, sw, uw)), _resident((n_tiles, sw, uw)),
            _resident((1, D)), _resident((D, D)), _resident((1, D)), _resident((D, D)),
        ],
        out_specs=pl.BlockSpec((tt, D), lambda i: (i, 0)),
        out_shape=jax.ShapeDtypeStruct((L, D), F32),
        scratch_shapes=[
            pltpu.VMEM((n_tiles, sw), F32), pltpu.VMEM((n_tiles, sw), F32),
            pltpu.VMEM((n_tiles, uw, sw), BF16), pltpu.VMEM((n_tiles, uw, sw), BF16),
            pltpu.VMEM((nj, n_tiles, LANES), F32), pltpu.VMEM((nj, n_tiles, LANES), F32),
            pltpu.VMEM((nj, tt * n_tiles, LANES), F32), pltpu.VMEM((nj, tt * n_tiles, LANES), F32),
            pltpu.VMEM((tt, D), F32),
        ],
        compiler_params=_params("arbitrary"),
        name="s5_mixer",
    )(x, _row(gpre), _row(gpost), w_in.astype(BF16),
      per_tile(a_re), per_tile(a_im), per_tile(jnp.broadcast_to(log_dt[:, None], (G, P))),
      bd_re, bd_im, cd_re.astype(BF16), cd_im.astype(BF16),
      _row(d_skip), w_glu.astype(BF16), _row(b_glu), w_out.astype(BF16))


def _conv_body(x_ref, gpre_ref, gpost_ref, win_ref, bin_ref, dw_ref, dwb_ref,
               lng_ref, lnb_ref, wout_ref, bout_ref, o_ref, ext, conv_scr):
    tm, D = x_ref.shape
    halo = ext.shape[0] - tm
    n_taps = dw_ref.shape[0]

    @pl.when(pl.program_id(0) == 0)
    def _():
        ext[0:halo, :] = jnp.zeros((halo, D), F32)

    x = x_ref[...]
    xn = _rms(x, gpre_ref[...]).astype(BF16)
    a = _dot(xn, win_ref[:, :D]) + bin_ref[:, :D]
    g = _dot(xn, win_ref[:, D:]) + bin_ref[:, D:]
    ext[halo:halo + tm, :] = a * jax.nn.sigmoid(g)

    base = halo - (n_taps - 1)
    for c in range(D // LANES):
        cs = slice(c * LANES, (c + 1) * LANES)
        acc = None
        for s in range(SUBLANES):
            rows = tm if s == 0 else tm + SUBLANES
            q = None
            for a in range((base + n_taps - 1) // SUBLANES + 1):
                k = SUBLANES * a + s - base
                if 0 <= k < n_taps:
                    term = dw_ref[k:k + 1, cs] * ext[SUBLANES * a:SUBLANES * a + rows, cs]
                    q = term if q is None else q + term
            part = q if s == 0 else q[s:s + tm]
            acc = part if acc is None else acc + part
        conv_scr[:, cs] = acc
    ext[0:halo, :] = ext[tm:tm + halo, :]

    y = _layernorm(conv_scr[...] + dwb_ref[...], lng_ref[...], lnb_ref[...])
    y = y * jax.nn.sigmoid(y)
    h = _dot(y.astype(BF16), wout_ref[...]) + bout_ref[...]
    o_ref[...] = x + _rms(h, gpost_ref[...])


def _conv_mixer(x, gpre, gpost, w_in, b_in, dw, dw_b, ln_g, ln_b, w_out, b_out, tm=256):
    L, D = x.shape
    return pl.pallas_call(
        _conv_body,
        grid=(L // tm,),
        in_specs=[
            pl.BlockSpec((tm, D), lambda i: (i, 0)),
            _resident((1, D)), _resident((1, D)), _resident((D, 2 * D)), _resident((1, 2 * D)),
            _resident(dw.shape), _resident((1, D)), _resident((1, D)), _resident((1, D)),
            _resident((D, D)), _resident((1, D)),
        ],
        out_specs=pl.BlockSpec((tm, D), lambda i: (i, 0)),
        out_shape=jax.ShapeDtypeStruct((L, D), F32),
        scratch_shapes=[pltpu.VMEM((tm + CONV_HALO, D), F32), pltpu.VMEM((tm, D), F32)],
        compiler_params=_params("arbitrary"),
        name="conv_mixer",
    )(x, _row(gpre), _row(gpost), w_in.astype(BF16), _row(b_in), dw, _row(dw_b), _row(ln_g), _row(ln_b),
      w_out.astype(BF16), _row(b_out))


def _gmlp_body(x_ref, gpre_ref, gpost_ref, win_ref, bin_ref, lng_ref, lnb_ref, ws_ref, bs_ref,
               wout_ref, bout_ref, o_ref, z_scr, su_scr, *, fc):
    n_sub, sub, _ = su_scr.shape
    E = lng_ref.shape[1]
    n_heads, chunk, _ = ws_ref.shape
    hw = E // n_heads
    causal = (lax.broadcasted_iota(jnp.int32, (chunk, chunk), 1)
              <= lax.broadcasted_iota(jnp.int32, (chunk, chunk), 0))
    wm = [jnp.where(causal, ws_ref[h], 0.0).astype(BF16) for h in range(n_heads)]
    for i in range(n_sub):
        rows = slice(i * sub, (i + 1) * sub)
        x = x_ref[rows, :]
        xn = _rms(x, gpre_ref[...]).astype(BF16)
        for c in range(2 * E // fc):
            cs = slice(c * fc, (c + 1) * fc)
            z_scr[i, :, cs] = jax.nn.gelu(_dot(xn, win_ref[:, cs]) + bin_ref[:, cs])
        vb = _layernorm(z_scr[i, :, E:], lng_ref[...], lnb_ref[...]).astype(BF16)
        for h in range(n_heads):
            bias = bs_ref[:, h:h + 1]
            hs = slice(h * hw, (h + 1) * hw)
            for c in range(sub // chunk):
                rs = slice(c * chunk, (c + 1) * chunk)
                s = _dot(wm[h], vb[rs, hs]) + bias
                su_scr[i, rs, hs] = (z_scr[i, rs, hs] * s).astype(BF16)
        hout = _dot(su_scr[i], wout_ref[...]) + bout_ref[...]
        o_ref[rows, :] = x + _rms(hout, gpost_ref[...])


def _gmlp_mixer(x, gpre, gpost, w_in, b_in, ln_g, ln_b, w_s, b_s, w_out, b_out, tm=512, sub=512, fc=512):
    L, D = x.shape
    E = ln_g.shape[0]
    return pl.pallas_call(
        functools.partial(_gmlp_body, fc=fc),
        grid=(L // tm,),
        in_specs=[
            pl.BlockSpec((tm, D), lambda i: (i, 0)),
            _resident((1, D)), _resident((1, D)), _resident((D, 2 * E)), _resident((1, 2 * E)),
            _resident((1, E)), _resident((1, E)),
            _resident(w_s.shape), _resident((GM_CHUNK, GM_HEADS)),
            _resident((E, D)), _resident((1, D)),
        ],
        out_specs=pl.BlockSpec((tm, D), lambda i: (i, 0)),
        out_shape=jax.ShapeDtypeStruct((L, D), F32),
        scratch_shapes=[pltpu.VMEM((tm // sub, sub, 2 * E), F32), pltpu.VMEM((tm // sub, sub, E), BF16)],
        compiler_params=_params("parallel"),
        name="gmlp_mixer",
    )(x, _row(gpre), _row(gpost), w_in.astype(BF16), _row(b_in), _row(ln_g), _row(ln_b),
      w_s, b_s.T, w_out.astype(BF16), _row(b_out))


def _qkv_body(x_ref, g_ref, w_ref, o_ref, xt_scr, xs_scr):
    d, rows, _ = o_ref.shape
    if d == 1:
        x = x_ref[...]
    else:
        for c in range(xt_scr.shape[0]):
            cs = slice(c * LANES, (c + 1) * LANES)
            xt_scr[c] = x_ref[:, cs]
            for r in range(d):
                xs_scr[r * rows:(r + 1) * rows, cs] = xt_scr[c, pl.ds(r, rows, stride=d), :]
        x = xs_scr[...]
    res = _dot(_rms(x, g_ref[...]).astype(BF16), w_ref[...])
    for r in range(d):
        o_ref[r] = res[r * rows:(r + 1) * rows, :].astype(o_ref.dtype)


def _qkv_proj(x, g, w, pattern, dilation, tm=512):
    L, D = x.shape
    N = 3 * D
    return pl.pallas_call(
        _qkv_body,
        grid=(L // tm,),
        in_specs=[
            pl.BlockSpec((tm, D), lambda i: (i, 0)),
            _resident((1, D)),
            pl.BlockSpec((D, N), lambda i: (0, pattern), pipeline_mode=pl.Buffered(1)),
        ],
        out_specs=pl.BlockSpec((dilation, tm // dilation, N), lambda i: (0, i, 0)),
        out_shape=jax.ShapeDtypeStruct((dilation, L // dilation, N), BF16),
        scratch_shapes=[pltpu.VMEM((D // LANES, tm, LANES), F32), pltpu.VMEM((tm, D), F32)],
        compiler_params=_params("parallel"),
        name=f"qkv_proj_d{dilation}",
    )(x, _row(g), w)


def _t5_buckets(dilation):
    delta = (np.arange(BLOCK)[:, None] + BLOCK) - np.arange(2 * BLOCK)[None, :]
    dist = np.maximum(delta, 0) * dilation
    max_exact = NUM_BUCKETS // 2
    distf = np.maximum(dist, 1).astype(np.float32)
    large = max_exact + (np.log(distf / np.float32(max_exact)) / np.float32(math.log(MAX_DISTANCE / max_exact))
                         * np.float32(NUM_BUCKETS - max_exact)).astype(np.int32)
    large = np.minimum(large, NUM_BUCKETS - 1)
    return np.where(dist < max_exact, dist, large).astype(np.int32)


def _attn_body(tab_ref, bkt_ref, q_ref, kc_ref, vc_ref, o_ref, lse_ref, bias_scr, k2, v2, s_scr, p_scr, *, scale):
    blk = q_ref.shape[0]
    n_heads = bias_scr.shape[1]
    hd = q_ref.shape[1] // n_heads
    n = pl.program_id(1)

    @pl.when((pl.program_id(0) == 0) & (n == 0))
    def _():
        bkt = bkt_ref[...]
        qi = lax.broadcasted_iota(jnp.int32, bkt.shape, 0)
        ki = lax.broadcasted_iota(jnp.int32, bkt.shape, 1)
        delta = qi + blk - ki
        in_band = (delta >= 0) & (delta <= blk)
        for h in range(n_heads):
            acc = jnp.zeros(bkt.shape, F32)
            for b in range(tab_ref.shape[0]):
                acc = jnp.where(bkt == b, tab_ref[b, h], acc)
            acc = jnp.where(in_band, acc, NEG_BIG)
            bias_scr[1, h] = acc
            bias_scr[0, h] = jnp.where(ki >= blk, acc, NEG_BIG)

    @pl.when(n == 0)
    def _():
        k2[0:blk, :] = jnp.zeros((blk, k2.shape[1]), k2.dtype)
        v2[0:blk, :] = jnp.zeros((blk, v2.shape[1]), v2.dtype)

    k2[blk:2 * blk, :] = kc_ref[...]
    v2[blk:2 * blk, :] = vc_ref[...]
    slot = jnp.minimum(n, 1)
    dims = (((1,), (1,)), ((), ()))
    per_tile = LANES // hd
    lane = lax.broadcasted_iota(jnp.int32, (1, LANES), 1)
    own = [(lane >= i * hd) & (lane < (i + 1) * hd) for i in range(per_tile)]
    n_tiles = n_heads // per_tile

    for t in range(n_tiles):
        cs = slice(t * LANES, (t + 1) * LANES)
        qt = q_ref[:, cs] * scale
        kt = k2[:, cs]
        for i in range(per_tile):
            h = t * per_tile + i
            qh = jnp.where(own[i], qt, jnp.zeros_like(qt))
            s_scr[h] = lax.dot_general(qh, kt, dims, preferred_element_type=F32) + bias_scr[slot, h]

    for t in range(n_tiles):
        lse_t = jnp.zeros((blk, LANES), F32)
        for i in range(per_tile):
            h = t * per_tile + i
            s = s_scr[h]
            m = jnp.max(s, axis=-1, keepdims=True)
            e = jnp.exp(s - m)
            den = jnp.sum(e, axis=-1, keepdims=True)
            p_scr[h] = (e * (1.0 / den)).astype(BF16)
            lse_t = jnp.where(own[i], m + jnp.log(den), lse_t)
        lse_ref[:, t * LANES:(t + 1) * LANES] = lse_t

    for t in range(n_tiles):
        cs = slice(t * LANES, (t + 1) * LANES)
        vt = v2[:, cs]
        acc = jnp.zeros((blk, LANES), F32)
        for i in range(per_tile):
            acc = acc + _dot(p_scr[t * per_tile + i], jnp.where(own[i], vt, jnp.zeros_like(vt)))
        o_ref[:, cs] = acc
    k2[0:blk, :] = k2[blk:2 * blk, :]
    v2[0:blk, :] = v2[blk:2 * blk, :]


def _dilated_attention(qkv, table, dilation):
    d, rows, W = qkv.shape
    D = W // 3
    n_heads = D // HEAD_DIM
    blk = lambda c: pl.BlockSpec((None, BLOCK, D), lambda r, n: (r, n, c))
    return pl.pallas_call(
        functools.partial(_attn_body, scale=HEAD_DIM ** -0.5),
        grid=(d, rows // BLOCK),
        in_specs=[
            pl.BlockSpec(memory_space=pltpu.SMEM),
            pl.BlockSpec((BLOCK, 2 * BLOCK), lambda r, n: (0, 0)),
            blk(0), blk(1), blk(2),
        ],
        out_specs=[blk(0)] * 2,
        out_shape=[jax.ShapeDtypeStruct((d, rows, D), F32)] * 2,
        scratch_shapes=[pltpu.VMEM((2, n_heads, BLOCK, 2 * BLOCK), F32),
                        pltpu.VMEM((2 * BLOCK, D), BF16), pltpu.VMEM((2 * BLOCK, D), BF16),
                        pltpu.VMEM((n_heads, BLOCK, 2 * BLOCK), F32),
                        pltpu.VMEM((n_heads, BLOCK, 2 * BLOCK), BF16)],
        compiler_params=_params("arbitrary", "arbitrary"),
        name=f"dilated_attn_d{dilation}",
    )(table, jnp.asarray(_t5_buckets(dilation)), qkv, qkv, qkv)


def _attn_out_body(x_ref, gpost_ref, o0_ref, o1_ref, o2_ref, l0_ref, l1_ref, l2_ref, wout_ref, out_ref, *scr):
    def natural(ref, s):
        d, rows, _ = ref.shape
        if d == 1:
            return ref[0]
        for c in range(s.shape[0]):
            for r in range(d):
                s[c, pl.ds(r, rows, stride=d), :] = ref[r, :, c * LANES:(c + 1) * LANES]
        return jnp.concatenate([s[c] for c in range(s.shape[0])], axis=-1)

    l0, l1, l2 = natural(l0_ref, None), natural(l1_ref, scr[0]), natural(l2_ref, scr[1])
    m = jnp.maximum(jnp.maximum(l0, l1), l2)
    e0, e1, e2 = jnp.exp(l0 - m), jnp.exp(l1 - m), jnp.exp(l2 - m)
    o = (e0 * natural(o0_ref, None) + e1 * natural(o1_ref, scr[2]) + e2 * natural(o2_ref, scr[3])) / (e0 + e1 + e2)
    h = _dot(o.astype(BF16), wout_ref[...])
    out_ref[...] = x_ref[...] + _rms(h, gpost_ref[...])


def _attention_mixer(x, gpre, gpost, w_qkv, w_out, rel_bias, tm=512):
    L, D = x.shape
    n_heads = D // HEAD_DIM
    outs, lses = [], []
    w_qkv = w_qkv.astype(BF16)
    for g, (window, dilation) in enumerate(PATTERNS):
        assert window // dilation == BLOCK and L % (BLOCK * dilation) == 0
        qkv = _qkv_proj(x, gpre, w_qkv, g, dilation)
        o, lse = _dilated_attention(qkv, rel_bias[:, g * n_heads:(g + 1) * n_heads], dilation)
        outs.append(o)
        lses.append(lse)
    tile = pl.BlockSpec((tm, D), lambda i: (i, 0))
    grouped = [pl.BlockSpec((d, tm // d, D), lambda i: (0, i, 0)) for _, d in PATTERNS]
    return pl.pallas_call(
        _attn_out_body,
        grid=(L // tm,),
        in_specs=[tile, pl.BlockSpec((1, D), lambda i: (0, 0))] + grouped * 2
                 + [pl.BlockSpec((D, D), lambda i: (0, 0))],
        out_specs=tile,
        out_shape=jax.ShapeDtypeStruct((L, D), F32),
        scratch_shapes=[pltpu.VMEM((D // LANES, tm, LANES), F32)] * 4,
        compiler_params=_params("parallel"),
        name="attn_out",
    )(x, _row(gpost), *outs, *lses, w_out.astype(BF16))


def kernel(x, norm_pre, norm_post, ffn_w1, ffn_w3, ffn_w2, rel_bias, s5_w_in, s5_a_re, s5_a_im, s5_log_dt, s5_b_re, s5_b_im, s5_c_re, s5_c_im, s5_d, s5_w_glu, s5_b_glu, s5_w_out, cv_w_in, cv_b_in, cv_dw, cv_dw_b, cv_ln_g, cv_ln_b, cv_w_out, cv_b_out, gm_w_in, gm_b_in, gm_ln_g, gm_ln_b, gm_w_s, gm_b_s, gm_w_out, gm_b_out, at_w_qkv, at_w_out):
    bsz, seq, d_model = x.shape
    assert bsz == 1, "sequence mixers carry state along the row axis of one sequence"
    depth = norm_pre.shape[0]
    n_mixers = 4
    h = x.reshape(seq, d_model)
    for i in range(depth):
        kind, j = i % n_mixers, i // n_mixers
        h = _ffn(h, norm_pre[i, 0], norm_post[i, 0], ffn_w1[i, 0], ffn_w3[i, 0], ffn_w2[i, 0])
        if kind == 0:
            h = _s5_mixer(h, norm_pre[i, 1], norm_post[i, 1], s5_w_in[j], s5_a_re[j], s5_a_im[j], s5_log_dt[j],
                          s5_b_re[j], s5_b_im[j], s5_c_re[j], s5_c_im[j], s5_d[j], s5_w_glu[j], s5_b_glu[j],
                          s5_w_out[j])
        elif kind == 1:
            h = _conv_mixer(h, norm_pre[i, 1], norm_post[i, 1], cv_w_in[j], cv_b_in[j], cv_dw[j], cv_dw_b[j],
                            cv_ln_g[j], cv_ln_b[j], cv_w_out[j], cv_b_out[j])
        elif kind == 2:
            h = _gmlp_mixer(h, norm_pre[i, 1], norm_post[i, 1], gm_w_in[j], gm_b_in[j], gm_ln_g[j], gm_ln_b[j],
                            gm_w_s[j], gm_b_s[j], gm_w_out[j], gm_b_out[j])
        else:
            h = _attention_mixer(h, norm_pre[i, 1], norm_post[i, 1], at_w_qkv[j], at_w_out[j], rel_bias)
        h = _ffn(h, norm_pre[i, 2], norm_post[i, 2], ffn_w1[i, 1], ffn_w3[i, 1], ffn_w2[i, 1])
    return h.reshape(bsz, seq, d_model)
```

```python
import functools
import math

import numpy as np
import jax
import jax.numpy as jnp
from jax import lax
from jax.experimental import pallas as pl
from jax.experimental.pallas import tpu as pltpu

F32 = jnp.float32
BF16 = jnp.bfloat16

EPS = 1e-6
LANES = 128
SUBLANES = 8
S5_GROUP = 16
S5_STATE = 64
S5_GROUPS_PER_TILE = 8
CONV_W = 31
CONV_HALO = 32
GM_CHUNK = 128
GM_HEADS = 8
HEAD_DIM = 64
PATTERNS = ((128, 1), (512, 4), (2048, 16))
BLOCK = 128
NUM_BUCKETS = 32
MAX_DISTANCE = 2048
NEG_BIG = -1e30

VMEM_LIMIT_BYTES = 56 * 1024 * 1024


def _params(*sem):
    return pltpu.CompilerParams(dimension_semantics=sem, vmem_limit_bytes=VMEM_LIMIT_BYTES)


def _rms(x, g):
    return x * lax.rsqrt(jnp.mean(x * x, axis=-1, keepdims=True) + EPS) * g


def _layernorm(x, g, b):
    mu = jnp.mean(x, axis=-1, keepdims=True)
    xc = x - mu
    var = jnp.mean(xc * xc, axis=-1, keepdims=True)
    return xc * lax.rsqrt(var + EPS) * g + b


def _dot(a, b):
    return jnp.dot(a, b, preferred_element_type=F32)


def _row(v):
    return v.reshape(1, -1)


def _ffn_body(x_ref, gpre_ref, gpost_ref, w1_ref, w3_ref, w2_ref, o_ref, g_scr, *, fc):
    n_sub, sub, F = g_scr.shape
    for s in range(n_sub):
        rs = slice(s * sub, (s + 1) * sub)
        xn = _rms(x_ref[rs, :], gpre_ref[...]).astype(BF16)
        for c in range(F // fc):
            cs = slice(c * fc, (c + 1) * fc)
            h1 = _dot(xn, w1_ref[:, cs])
            h3 = _dot(xn, w3_ref[:, cs])
            g_scr[s, :, cs] = ((h1 * jax.nn.sigmoid(h1)) * h3).astype(BF16)
        h = _dot(g_scr[s], w2_ref[...])
        o_ref[rs, :] = x_ref[rs, :] + 0.5 * _rms(h, gpost_ref[...])


def _resident(shape):
    return pl.BlockSpec(shape, lambda i: (0,) * len(shape), pipeline_mode=pl.Buffered(1))


def _ffn(x, gpre, gpost, w1, w3, w2, layer, half, tm=1024, sub=512, fc=256):
    L, D = x.shape
    F = w1.shape[-1]
    picked = lambda r, c: pl.BlockSpec((None, None, r, c), lambda i: (layer, half, 0, 0),
                                       pipeline_mode=pl.Buffered(1))
    return pl.pallas_call(
        functools.partial(_ffn_body, fc=fc),
        grid=(L // tm,),
        in_specs=[
            pl.BlockSpec((tm, D), lambda i: (i, 0)),
            _resident((1, D)), _resident((1, D)),
            picked(D, F), picked(D, F), picked(F, D),
        ],
        out_specs=pl.BlockSpec((tm, D), lambda i: (i, 0)),
        out_shape=jax.ShapeDtypeStruct((L, D), F32),
        scratch_shapes=[pltpu.VMEM((tm // sub, sub, F), BF16)],
        compiler_params=_params("parallel"),
        name="ffn",
    )(x, _row(gpre), _row(gpost), w1, w3, w2)


def _s5_body(x_ref, gpre_ref, gpost_ref, win_ref, are_ref, aim_ref, ldt_ref, bdre_ref, bdim_ref,
             cdre_ref, cdim_ref, d_ref, wglu_ref, bglu_ref, wout_ref, o_ref,
             abar_re, abar_im, bb_re, bb_im, st_re, st_im, bu_re, bu_im, y_scr, *, unroll):
    tt = x_ref.shape[0]
    n_tiles, uw, sw = bdre_ref.shape
    nj = sw // LANES

    @pl.when(pl.program_id(0) == 0)
    def _():
        ar = are_ref[...]
        ai = aim_ref[...]
        dt = jnp.exp(ldt_ref[...])
        mag = jnp.exp(dt * ar)
        abr = mag * jnp.cos(dt * ai)
        abi = mag * jnp.sin(dt * ai)
        den = ar * ar + ai * ai
        nr = abr - 1.0
        fre = (nr * ar + abi * ai) / den
        fim = (abi * ar - nr * ai) / den
        abar_re[...] = abr
        abar_im[...] = abi
        for k in range(n_tiles):
            fr = fre[k:k + 1, :]
            fi = fim[k:k + 1, :]
            bb_re[k] = (fr * bdre_ref[k] - fi * bdim_ref[k]).astype(BF16)
            bb_im[k] = (fr * bdim_ref[k] + fi * bdre_ref[k]).astype(BF16)
        st_re[...] = jnp.zeros_like(st_re)
        st_im[...] = jnp.zeros_like(st_im)

    x = x_ref[...]
    xn = _rms(x, gpre_ref[...]).astype(BF16)
    u = _dot(xn, win_ref[...])
    ub = u.astype(BF16)
    for k in range(n_tiles):
        uk = ub[:, k * uw:(k + 1) * uw]
        br = _dot(uk, bb_re[k])
        bi = _dot(uk, bb_im[k])
        for j in range(nj):
            bu_re[j, pl.ds(k, tt, stride=n_tiles), :] = br[:, j * LANES:(j + 1) * LANES]
            bu_im[j, pl.ds(k, tt, stride=n_tiles), :] = bi[:, j * LANES:(j + 1) * LANES]

    a_r = [abar_re[:, j * LANES:(j + 1) * LANES] for j in range(nj)]
    a_i = [abar_im[:, j * LANES:(j + 1) * LANES] for j in range(nj)]

    def step(t, carry):
        rows = pl.ds(pl.multiple_of(t * n_tiles, n_tiles), n_tiles)
        out = []
        for j in range(nj):
            sr, si = carry[2 * j], carry[2 * j + 1]
            nr = a_r[j] * sr - a_i[j] * si + bu_re[j, rows, :]
            ni = a_r[j] * si + a_i[j] * sr + bu_im[j, rows, :]
            bu_re[j, rows, :] = nr
            bu_im[j, rows, :] = ni
            out += [nr, ni]
        return tuple(out)

    init = tuple(ref[j] for j in range(nj) for ref in (st_re, st_im))
    last = lax.fori_loop(0, tt, step, init, unroll=unroll)
    for j in range(nj):
        st_re[j] = last[2 * j]
        st_im[j] = last[2 * j + 1]

    for k in range(n_tiles):
        sre = jnp.concatenate([bu_re[j, pl.ds(k, tt, stride=n_tiles), :] for j in range(nj)], axis=-1)
        sim = jnp.concatenate([bu_im[j, pl.ds(k, tt, stride=n_tiles), :] for j in range(nj)], axis=-1)
        y_scr[:, k * uw:(k + 1) * uw] = _dot(sre.astype(BF16), cdre_ref[k]) - _dot(sim.astype(BF16), cdim_ref[k])
    y = jax.nn.gelu(y_scr[...] + d_ref[...] * u)
    z = y * jax.nn.sigmoid(_dot(y.astype(BF16), wglu_ref[...]) + bglu_ref[...])
    h = _dot(z.astype(BF16), wout_ref[...])
    o_ref[...] = x + _rms(h, gpost_ref[...])


def _s5_block_diag(b, c):
    G, P, HG = b.shape
    gt = S5_GROUPS_PER_TILE
    eye = jnp.eye(gt, dtype=b.dtype)
    bt = b.reshape(G // gt, gt, P, HG).transpose(0, 1, 3, 2)
    bd = jnp.einsum('kghp,gj->kghjp', bt, eye).reshape(G // gt, gt * HG, gt * P)
    ct = c.reshape(G // gt, gt, HG, P)
    cd = jnp.einsum('kghp,gj->kjpgh', ct, eye).reshape(G // gt, gt * P, gt * HG)
    return bd, cd


def _s5_mixer(x, gpre, gpost, w_in, a_re, a_im, log_dt, b_re, b_im, c_re, c_im, d_skip, w_glu, b_glu,
              w_out, tt=256, unroll=4):
    L, D = x.shape
    G, P = a_re.shape
    bd_re, cd_re = _s5_block_diag(b_re, c_re)
    bd_im, cd_im = _s5_block_diag(b_im, c_im)
    n_tiles, uw, sw = bd_re.shape
    assert n_tiles == SUBLANES and uw == LANES and sw % LANES == 0
    nj = sw // LANES
    per_tile = lambda v: v.reshape(n_tiles, sw)
    return pl.pallas_call(
        functools.partial(_s5_body, unroll=unroll),
        grid=(L // tt,),
        in_specs=[
            pl.BlockSpec((tt, D), lambda i: (i, 0)),
            _resident((1, D)), _resident((1, D)), _resident((D, D)),
            _resident((n_tiles, sw)), _resident((n_tiles, sw)), _resident((n_tiles, sw)),
            _resident((n_tiles, uw, sw)), _resident((n_tiles, uw, sw)),
            _resident((n_tiles, sw, uw)), _resident((n_tiles, sw, uw)),
            _resident((1, D)), _resident((D, D)), _resident((1, D)), _resident((D, D)),
        ],
        out_specs=pl.BlockSpec((tt, D), lambda i: (i, 0)),
        out_shape=jax.ShapeDtypeStruct((L, D), F32),
        scratch_shapes=[
            pltpu.VMEM((n_tiles, sw), F32), pltpu.VMEM((n_tiles, sw), F32),
            pltpu.VMEM((n_tiles, uw, sw), BF16), pltpu.VMEM((n_tiles, uw, sw), BF16),
            pltpu.VMEM((nj, n_tiles, LANES), F32), pltpu.VMEM((nj, n_tiles, LANES), F32),
            pltpu.VMEM((nj, tt * n_tiles, LANES), F32), pltpu.VMEM((nj, tt * n_tiles, LANES), F32),
            pltpu.VMEM((tt, D), F32),
        ],
        compiler_params=_params("arbitrary"),
        name="s5_mixer",
    )(x, _row(gpre), _row(gpost), w_in.astype(BF16),
      per_tile(a_re), per_tile(a_im), per_tile(jnp.broadcast_to(log_dt[:, None], (G, P))),
      bd_re, bd_im, cd_re.astype(BF16), cd_im.astype(BF16),
      _row(d_skip), w_glu.astype(BF16), _row(b_glu), w_out.astype(BF16))


def _conv_body(x_ref, gpre_ref, gpost_ref, win_ref, bin_ref, dw_ref, dwb_ref,
               lng_ref, lnb_ref, wout_ref, bout_ref, o_ref, ext, conv_scr):
    tm, D = x_ref.shape
    halo = ext.shape[0] - tm
    n_taps = dw_ref.shape[0]

    @pl.when(pl.program_id(0) == 0)
    def _():
        ext[0:halo, :] = jnp.zeros((halo, D), F32)

    x = x_ref[...]
    xn = _rms(x, gpre_ref[...]).astype(BF16)
    a = _dot(xn, win_ref[:, :D]) + bin_ref[:, :D]
    g = _dot(xn, win_ref[:, D:]) + bin_ref[:, D:]
    ext[halo:halo + tm, :] = a * jax.nn.sigmoid(g)

    base = halo - (n_taps - 1)
    for c in range(D // LANES):
        cs = slice(c * LANES, (c + 1) * LANES)
        acc = None
        for s in range(SUBLANES):
            rows = tm if s == 0 else tm + SUBLANES
            q = None
            for a in range((base + n_taps - 1) // SUBLANES + 1):
                k = SUBLANES * a + s - base
                if 0 <= k < n_taps:
                    term = dw_ref[k:k + 1, cs] * ext[SUBLANES * a:SUBLANES * a + rows, cs]
                    q = term if q is None else q + term
            part = q if s == 0 else q[s:s + tm]
            acc = part if acc is None else acc + part
        conv_scr[:, cs] = acc
    ext[0:halo, :] = ext[tm:tm + halo, :]

    y = _layernorm(conv_scr[...] + dwb_ref[...], lng_ref[...], lnb_ref[...])
    y = y * jax.nn.sigmoid(y)
    h = _dot(y.astype(BF16), wout_ref[...]) + bout_ref[...]
    o_ref[...] = x + _rms(h, gpost_ref[...])


def _conv_mixer(x, gpre, gpost, w_in, b_in, dw, dw_b, ln_g, ln_b, w_out, b_out, tm=256):
    L, D = x.shape
    return pl.pallas_call(
        _conv_body,
        grid=(L // tm,),
        in_specs=[
            pl.BlockSpec((tm, D), lambda i: (i, 0)),
            _resident((1, D)), _resident((1, D)), _resident((D, 2 * D)), _resident((1, 2 * D)),
            _resident(dw.shape), _resident((1, D)), _resident((1, D)), _resident((1, D)),
            _resident((D, D)), _resident((1, D)),
        ],
        out_specs=pl.BlockSpec((tm, D), lambda i: (i, 0)),
        out_shape=jax.ShapeDtypeStruct((L, D), F32),
        scratch_shapes=[pltpu.VMEM((tm + CONV_HALO, D), F32), pltpu.VMEM((tm, D), F32)],
        compiler_params=_params("arbitrary"),
        name="conv_mixer",
    )(x, _row(gpre), _row(gpost), w_in.astype(BF16), _row(b_in), dw, _row(dw_b), _row(ln_g), _row(ln_b),
      w_out.astype(BF16), _row(b_out))


def _gmlp_body(x_ref, gpre_ref, gpost_ref, win_ref, bin_ref, lng_ref, lnb_ref, ws_ref, bs_ref,
               wout_ref, bout_ref, o_ref, z_scr, su_scr, *, fc):
    n_sub, sub, _ = su_scr.shape
    E = lng_ref.shape[1]
    n_heads, chunk, _ = ws_ref.shape
    hw = E // n_heads
    causal = (lax.broadcasted_iota(jnp.int32, (chunk, chunk), 1)
              <= lax.broadcasted_iota(jnp.int32, (chunk, chunk), 0))
    wm = [jnp.where(causal, ws_ref[h], 0.0).astype(BF16) for h in range(n_heads)]
    for i in range(n_sub):
        rows = slice(i * sub, (i + 1) * sub)
        x = x_ref[rows, :]
        xn = _rms(x, gpre_ref[...]).astype(BF16)
        for c in range(2 * E // fc):
            cs = slice(c * fc, (c + 1) * fc)
            z_scr[i, :, cs] = jax.nn.gelu(_dot(xn, win_ref[:, cs]) + bin_ref[:, cs])
        vb = _layernorm(z_scr[i, :, E:], lng_ref[...], lnb_ref[...]).astype(BF16)
        for h in range(n_heads):
            bias = bs_ref[:, h:h + 1]
            hs = slice(h * hw, (h + 1) * hw)
            for c in range(sub // chunk):
                rs = slice(c * chunk, (c + 1) * chunk)
                s = _dot(wm[h], vb[rs, hs]) + bias
                su_scr[i, rs, hs] = (z_scr[i, rs, hs] * s).astype(BF16)
        hout = _dot(su_scr[i], wout_ref[...]) + bout_ref[...]
        o_ref[rows, :] = x + _rms(hout, gpost_ref[...])


def _gmlp_mixer(x, gpre, gpost, w_in, b_in, ln_g, ln_b, w_s, b_s, w_out, b_out, tm=512, sub=512, fc=512):
    L, D = x.shape
    E = ln_g.shape[0]
    return pl.pallas_call(
        functools.partial(_gmlp_body, fc=fc),
        grid=(L // tm,),
        in_specs=[
            pl.BlockSpec((tm, D), lambda i: (i, 0)),
            _resident((1, D)), _resident((1, D)), _resident((D, 2 * E)), _resident((1, 2 * E)),
            _resident((1, E)), _resident((1, E)),
            _resident(w_s.shape), _resident((GM_CHUNK, GM_HEADS)),
            _resident((E, D)), _resident((1, D)),
        ],
        out_specs=pl.BlockSpec((tm, D), lambda i: (i, 0)),
        out_shape=jax.ShapeDtypeStruct((L, D), F32),
        scratch_shapes=[pltpu.VMEM((tm // sub, sub, 2 * E), F32), pltpu.VMEM((tm // sub, sub, E), BF16)],
        compiler_params=_params("parallel"),
        name="gmlp_mixer",
    )(x, _row(gpre), _row(gpost), w_in.astype(BF16), _row(b_in), _row(ln_g), _row(ln_b),
      w_s, b_s.T, w_out.astype(BF16), _row(b_out))


def _regroup_body(x_ref, g_ref, *refs):
    *o_refs, xt_scr = refs
    xn = _rms(x_ref[...], g_ref[...])
    for c in range(xt_scr.shape[0]):
        xt_scr[c] = xn[:, c * LANES:(c + 1) * LANES]
    for o_ref in o_refs:
        d, rows, _ = o_ref.shape
        if d == 1:
            o_ref[0] = xn.astype(o_ref.dtype)
            continue
        for r in range(d):
            o_ref[r] = jnp.concatenate([xt_scr[c, pl.ds(r, rows, stride=d), :] for c in range(xt_scr.shape[0])],
                                       axis=-1).astype(o_ref.dtype)


def _regroup_rows(x, g, dilations, tm=512):
    L, D = x.shape
    return pl.pallas_call(
        _regroup_body,
        grid=(L // tm,),
        in_specs=[pl.BlockSpec((tm, D), lambda i: (i, 0)), _resident((1, D))],
        out_specs=[pl.BlockSpec((d, tm // d, D), lambda i: (0, i, 0)) for d in dilations],
        out_shape=[jax.ShapeDtypeStruct((d, L // d, D), BF16) for d in dilations],
        scratch_shapes=[pltpu.VMEM((D // LANES, tm, LANES), F32)],
        compiler_params=_params("parallel"),
        name="regroup_rows",
    )(x, _row(g))


def _t5_buckets(dilation):
    delta = (np.arange(BLOCK)[:, None] + BLOCK) - np.arange(2 * BLOCK)[None, :]
    dist = np.maximum(delta, 0) * dilation
    max_exact = NUM_BUCKETS // 2
    distf = np.maximum(dist, 1).astype(np.float32)
    large = max_exact + (np.log(distf / np.float32(max_exact)) / np.float32(math.log(MAX_DISTANCE / max_exact))
                         * np.float32(NUM_BUCKETS - max_exact)).astype(np.int32)
    large = np.minimum(large, NUM_BUCKETS - 1)
    return np.where(dist < max_exact, dist, large).astype(np.int32)


def _attn_body(tab_ref, bkt_ref, xc_ref, xn_ref, w_ref, o_ref, lse_ref,
               bias_scr, q_cur, k_win, v_win, q_nxt, k_nxt, v_nxt, s_scr, p_scr, *, scale):
    blk = bias_scr.shape[2]
    n_heads = bias_scr.shape[1]
    D = q_cur.shape[1]
    hd = D // n_heads
    qb = q_cur.shape[0] // blk
    n = pl.program_id(1)

    def project(x_ref, q_dst, k_dst, v_dst, k_off):
        xb = x_ref[...]
        q_dst[...] = (_dot(xb, w_ref[:, 0:D]) * scale).astype(BF16)
        k_dst[k_off:k_off + qb * blk, :] = _dot(xb, w_ref[:, D:2 * D]).astype(BF16)
        v_dst[k_off:k_off + qb * blk, :] = _dot(xb, w_ref[:, 2 * D:3 * D]).astype(BF16)

    @pl.when(n == 0)
    def _():
        k_win[0:blk, :] = jnp.zeros((blk, D), BF16)
        v_win[0:blk, :] = jnp.zeros((blk, D), BF16)
        project(xc_ref, q_cur, k_win, v_win, blk)

    @pl.when((pl.program_id(0) == 0) & (n == 0))
    def _():
        bkt = bkt_ref[...]
        qi = lax.broadcasted_iota(jnp.int32, bkt.shape, 0)
        ki = lax.broadcasted_iota(jnp.int32, bkt.shape, 1)
        delta = qi + blk - ki
        in_band = (delta >= 0) & (delta <= blk)
        for h in range(n_heads):
            acc = jnp.zeros(bkt.shape, F32)
            for b in range(tab_ref.shape[0]):
                acc = jnp.where(bkt == b, tab_ref[b, h], acc)
            acc = jnp.where(in_band, acc, NEG_BIG)
            bias_scr[1, h] = acc
            bias_scr[0, h] = jnp.where(ki >= blk, acc, NEG_BIG)

    project(xn_ref, q_nxt, k_nxt, v_nxt, 0)

    dims = (((1,), (1,)), ((), ()))
    per_tile = LANES // hd
    lane = lax.broadcasted_iota(jnp.int32, (1, LANES), 1)
    own = [(lane >= i * hd) & (lane < (i + 1) * hd) for i in range(per_tile)]
    n_tiles = n_heads // per_tile

    for b in range(qb):
        rows = slice(b * blk, (b + 1) * blk)
        keys = slice(b * blk, (b + 2) * blk)
        slot = jnp.minimum(n, 1) if b == 0 else 1
        for t in range(n_tiles):
            cs = slice(t * LANES, (t + 1) * LANES)
            qt = q_cur[rows, cs]
            kt = k_win[keys, cs]
            for i in range(per_tile):
                h = t * per_tile + i
                qh = jnp.where(own[i], qt, jnp.zeros_like(qt))
                s_scr[b, h] = lax.dot_general(qh, kt, dims, preferred_element_type=F32) + bias_scr[slot, h]

        lse_b = jnp.zeros((blk, LANES), F32)
        for h in range(n_heads):
            s = s_scr[b, h]
            m = jnp.max(s, axis=-1, keepdims=True)
            e = jnp.exp(s - m)
            den = jnp.sum(e, axis=-1, keepdims=True)
            p_scr[b, h] = (e * (1.0 / den)).astype(BF16)
            lse_b = jnp.where(lane == h, m + jnp.log(den), lse_b)
        lse_ref[rows, :] = lse_b

        for t in range(n_tiles):
            cs = slice(t * LANES, (t + 1) * LANES)
            vt = v_win[keys, cs]
            acc = jnp.zeros((blk, LANES), F32)
            for i in range(per_tile):
                acc = acc + _dot(p_scr[b, t * per_tile + i], jnp.where(own[i], vt, jnp.zeros_like(vt)))
            o_ref[rows, cs] = acc

    k_win[0:blk, :] = k_win[qb * blk:(qb + 1) * blk, :]
    v_win[0:blk, :] = v_win[qb * blk:(qb + 1) * blk, :]
    k_win[blk:(qb + 1) * blk, :] = k_nxt[...]
    v_win[blk:(qb + 1) * blk, :] = v_nxt[...]
    q_cur[...] = q_nxt[...]


def _dilated_attention(xn, w_qkv, pattern, table, dilation, qb=2):
    d, rows, D = xn.shape
    n_heads = D // HEAD_DIM
    tq = qb * BLOCK
    n_steps = rows // tq
    out = lambda w: pl.BlockSpec((None, tq, w), lambda r, n: (r, n, 0))
    return pl.pallas_call(
        functools.partial(_attn_body, scale=HEAD_DIM ** -0.5),
        grid=(d, n_steps),
        in_specs=[
            pl.BlockSpec(memory_space=pltpu.SMEM),
            pl.BlockSpec((BLOCK, 2 * BLOCK), lambda r, n: (0, 0), pipeline_mode=pl.Buffered(1)),
            pl.BlockSpec((None, tq, D), lambda r, n: (r, n, 0)),
            pl.BlockSpec((None, tq, D), lambda r, n: (r, jnp.minimum(n + 1, n_steps - 1), 0)),
            pl.BlockSpec((D, 3 * D), lambda r, n: (0, pattern), pipeline_mode=pl.Buffered(1)),
        ],
        out_specs=[out(D), out(LANES)],
        out_shape=[jax.ShapeDtypeStruct((d, rows, D), F32), jax.ShapeDtypeStruct((d, rows, LANES), F32)],
        scratch_shapes=[pltpu.VMEM((2, n_heads, BLOCK, 2 * BLOCK), F32),
                        pltpu.VMEM((tq, D), BF16),
                        pltpu.VMEM((tq + BLOCK, D), BF16), pltpu.VMEM((tq + BLOCK, D), BF16),
                        pltpu.VMEM((tq, D), BF16), pltpu.VMEM((tq, D), BF16), pltpu.VMEM((tq, D), BF16),
                        pltpu.VMEM((qb, n_heads, BLOCK, 2 * BLOCK), F32),
                        pltpu.VMEM((qb, n_heads, BLOCK, 2 * BLOCK), BF16)],
        compiler_params=_params("arbitrary", "arbitrary"),
        name=f"dilated_attn_d{dilation}",
    )(table, jnp.asarray(_t5_buckets(dilation)), xn, xn, w_qkv)


def _attn_out_body(x_ref, gpost_ref, o0_ref, o1_ref, o2_ref, l0_ref, l1_ref, l2_ref, wout_ref, out_ref, *scr):
    def natural(ref, s):
        d, rows, width = ref.shape
        if d == 1:
            return ref[0]
        n_lt = width // LANES
        for c in range(n_lt):
            for r in range(d):
                s[c, pl.ds(r, rows, stride=d), :] = ref[r, :, c * LANES:(c + 1) * LANES]
        return jnp.concatenate([s[c] for c in range(n_lt)], axis=-1)

    D = x_ref.shape[1]
    hd = D // (D // HEAD_DIM)
    spread = (lax.broadcasted_iota(jnp.int32, (LANES, D), 1) // hd
              == lax.broadcasted_iota(jnp.int32, (LANES, D), 0)).astype(BF16)

    def per_column(w):
        hi = w.astype(BF16)
        lo = (w - hi.astype(F32)).astype(BF16)
        return _dot(hi, spread) + _dot(lo, spread)

    l0, l1, l2 = natural(l0_ref, None), natural(l1_ref, scr[0]), natural(l2_ref, scr[1])
    m = jnp.maximum(jnp.maximum(l0, l1), l2)
    e0, e1, e2 = jnp.exp(l0 - m), jnp.exp(l1 - m), jnp.exp(l2 - m)
    inv = 1.0 / (e0 + e1 + e2)
    o = (per_column(e0 * inv) * natural(o0_ref, None) + per_column(e1 * inv) * natural(o1_ref, scr[2])
         + per_column(e2 * inv) * natural(o2_ref, scr[3]))
    h = _dot(o.astype(BF16), wout_ref[...])
    out_ref[...] = x_ref[...] + _rms(h, gpost_ref[...])


def _attention_mixer(x, gpre, gpost, w_qkv, w_out, rel_bias, tm=512):
    L, D = x.shape
    n_heads = D // HEAD_DIM
    outs, lses = [], []
    w_qkv = w_qkv.astype(BF16)
    xns = _regroup_rows(x, gpre, [d for _, d in PATTERNS])
    for g, (window, dilation) in enumerate(PATTERNS):
        assert window // dilation == BLOCK and L % (BLOCK * dilation) == 0
        o, lse = _dilated_attention(xns[g], w_qkv, g, rel_bias[:, g * n_heads:(g + 1) * n_heads], dilation)
        outs.append(o)
        lses.append(lse)
    tile = pl.BlockSpec((tm, D), lambda i: (i, 0))
    grouped = lambda w: [pl.BlockSpec((d, tm // d, w), lambda i: (0, i, 0)) for _, d in PATTERNS]
    return pl.pallas_call(
        _attn_out_body,
        grid=(L // tm,),
        in_specs=[tile, _resident((1, D))] + grouped(D) + grouped(LANES) + [_resident((D, D))],
        out_specs=tile,
        out_shape=jax.ShapeDtypeStruct((L, D), F32),
        scratch_shapes=[pltpu.VMEM((1, tm, LANES), F32)] * 2 + [pltpu.VMEM((D // LANES, tm, LANES), F32)] * 2,
        compiler_params=_params("parallel"),
        name="attn_out",
    )(x, _row(gpost), *outs, *lses, w_out.astype(BF16))


def kernel(x, norm_pre, norm_post, ffn_w1, ffn_w3, ffn_w2, rel_bias, s5_w_in, s5_a_re, s5_a_im, s5_log_dt, s5_b_re, s5_b_im, s5_c_re, s5_c_im, s5_d, s5_w_glu, s5_b_glu, s5_w_out, cv_w_in, cv_b_in, cv_dw, cv_dw_b, cv_ln_g, cv_ln_b, cv_w_out, cv_b_out, gm_w_in, gm_b_in, gm_ln_g, gm_ln_b, gm_w_s, gm_b_s, gm_w_out, gm_b_out, at_w_qkv, at_w_out):
    bsz, seq, d_model = x.shape
    assert bsz == 1, "sequence mixers carry state along the row axis of one sequence"
    depth = norm_pre.shape[0]
    n_mixers = 4
    h = x.reshape(seq, d_model)
    w1, w3, w2 = ffn_w1.astype(BF16), ffn_w3.astype(BF16), ffn_w2.astype(BF16)
    for i in range(depth):
        kind, j = i % n_mixers, i // n_mixers
        h = _ffn(h, norm_pre[i, 0], norm_post[i, 0], w1, w3, w2, i, 0)
        if kind == 0:
            h = _s5_mixer(h, norm_pre[i, 1], norm_post[i, 1], s5_w_in[j], s5_a_re[j], s5_a_im[j], s5_log_dt[j],
                          s5_b_re[j], s5_b_im[j], s5_c_re[j], s5_c_im[j], s5_d[j], s5_w_glu[j], s5_b_glu[j],
                          s5_w_out[j])
        elif kind == 1:
            h = _conv_mixer(h, norm_pre[i, 1], norm_post[i, 1], cv_w_in[j], cv_b_in[j], cv_dw[j], cv_dw_b[j],
                            cv_ln_g[j], cv_ln_b[j], cv_w_out[j], cv_b_out[j])
        elif kind == 2:
            h = _gmlp_mixer(h, norm_pre[i, 1], norm_post[i, 1], gm_w_in[j], gm_b_in[j], gm_ln_g[j], gm_ln_b[j],
                            gm_w_s[j], gm_b_s[j], gm_w_out[j], gm_b_out[j])
        else:
            h = _attention_mixer(h, norm_pre[i, 1], norm_post[i, 1], at_w_qkv[j], at_w_out[j], rel_bias)
        h = _ffn(h, norm_pre[i, 2], norm_post[i, 2], w1, w3, w2, i, 1)
    return h.reshape(bsz, seq, d_model)
```

```python
import functools
import math

import numpy as np
import jax
import jax.numpy as jnp
from jax import lax
from jax.experimental import pallas as pl
from jax.experimental.pallas import tpu as pltpu

F32 = jnp.float32
BF16 = jnp.bfloat16

EPS = 1e-6
LANES = 128
SUBLANES = 8
S5_GROUP = 16
S5_STATE = 64
S5_GROUPS_PER_TILE = 8
CONV_W = 31
CONV_HALO = 32
GM_CHUNK = 128
GM_HEADS = 8
HEAD_DIM = 64
PATTERNS = ((128, 1), (512, 4), (2048, 16))
BLOCK = 128
NUM_BUCKETS = 32
MAX_DISTANCE = 2048
NEG_BIG = -1e30

VMEM_LIMIT_BYTES = 56 * 1024 * 1024


def _params(*sem):
    return pltpu.CompilerParams(dimension_semantics=sem, vmem_limit_bytes=VMEM_LIMIT_BYTES)


def _rms(x, g):
    return x * lax.rsqrt(jnp.mean(x * x, axis=-1, keepdims=True) + EPS) * g


def _layernorm(x, g, b):
    mu = jnp.mean(x, axis=-1, keepdims=True)
    xc = x - mu
    var = jnp.mean(xc * xc, axis=-1, keepdims=True)
    return xc * lax.rsqrt(var + EPS) * g + b


def _dot(a, b):
    return jnp.dot(a, b, preferred_element_type=F32)


def _row(v):
    return v.reshape(1, -1)


def _ffn_body(x_ref, gpre_ref, gpost_ref, w1_ref, w3_ref, w2_ref, o_ref, g_scr, *, fc):
    n_sub, sub, F = g_scr.shape
    for s in range(n_sub):
        rs = slice(s * sub, (s + 1) * sub)
        xn = _rms(x_ref[rs, :], gpre_ref[...]).astype(BF16)
        for c in range(F // fc):
            cs = slice(c * fc, (c + 1) * fc)
            h1 = _dot(xn, w1_ref[:, cs])
            h3 = _dot(xn, w3_ref[:, cs])
            g_scr[s, :, cs] = ((h1 * jax.nn.sigmoid(h1)) * h3).astype(BF16)
        h = _dot(g_scr[s], w2_ref[...])
        o_ref[rs, :] = x_ref[rs, :] + 0.5 * _rms(h, gpost_ref[...])


def _resident(shape):
    return pl.BlockSpec(shape, lambda i: (0,) * len(shape), pipeline_mode=pl.Buffered(1))


def _ffn(x, gpre, gpost, w1, w3, w2, layer, half, tm=1024, sub=512, fc=256):
    L, D = x.shape
    F = w1.shape[-1]
    picked = lambda r, c: pl.BlockSpec((None, None, r, c), lambda i: (layer, half, 0, 0),
                                       pipeline_mode=pl.Buffered(1))
    return pl.pallas_call(
        functools.partial(_ffn_body, fc=fc),
        grid=(L // tm,),
        in_specs=[
            pl.BlockSpec((tm, D), lambda i: (i, 0)),
            _resident((1, D)), _resident((1, D)),
            picked(D, F), picked(D, F), picked(F, D),
        ],
        out_specs=pl.BlockSpec((tm, D), lambda i: (i, 0)),
        out_shape=jax.ShapeDtypeStruct((L, D), F32),
        scratch_shapes=[pltpu.VMEM((tm // sub, sub, F), BF16)],
        compiler_params=_params("parallel"),
        name="ffn",
    )(x, _row(gpre), _row(gpost), w1, w3, w2)


def _s5_body(x_ref, gpre_ref, gpost_ref, win_ref, are_ref, aim_ref, ldt_ref, bdre_ref, bdim_ref,
             cdre_ref, cdim_ref, d_ref, wglu_ref, bglu_ref, wout_ref, o_ref,
             abar_re, abar_im, bb_re, bb_im, st_re, st_im, bu_re, bu_im, u_scr, y_scr):
    tt = x_ref.shape[0]
    n_tiles, uw, sw = bdre_ref.shape
    nj = sw // LANES

    @pl.when(pl.program_id(0) == 0)
    def _():
        ar = are_ref[...]
        ai = aim_ref[...]
        dt = jnp.exp(ldt_ref[...])
        mag = jnp.exp(dt * ar)
        abr = mag * jnp.cos(dt * ai)
        abi = mag * jnp.sin(dt * ai)
        den = ar * ar + ai * ai
        nr = abr - 1.0
        fre = (nr * ar + abi * ai) / den
        fim = (abi * ar - nr * ai) / den
        abar_re[...] = abr
        abar_im[...] = abi
        for k in range(n_tiles):
            fr = fre[k:k + 1, :]
            fi = fim[k:k + 1, :]
            bb_re[k] = (fr * bdre_ref[k] - fi * bdim_ref[k]).astype(BF16)
            bb_im[k] = (fr * bdim_ref[k] + fi * bdre_ref[k]).astype(BF16)
        st_re[...] = jnp.zeros_like(st_re)
        st_im[...] = jnp.zeros_like(st_im)

    n_sub = bu_re.shape[0]
    sub = tt // n_sub
    for s in range(n_sub):
        rs = slice(s * sub, (s + 1) * sub)
        xn = _rms(x_ref[rs, :], gpre_ref[...]).astype(BF16)
        u = _dot(xn, win_ref[...])
        u_scr[rs, :] = u
        ub = u.astype(BF16)
        for k in range(n_tiles):
            uk = ub[:, k * uw:(k + 1) * uw]
            br = _dot(uk, bb_re[k])
            bi = _dot(uk, bb_im[k])
            for j in range(nj):
                bu_re[s, j, pl.ds(k, sub, stride=n_tiles), :] = br[:, j * LANES:(j + 1) * LANES]
                bu_im[s, j, pl.ds(k, sub, stride=n_tiles), :] = bi[:, j * LANES:(j + 1) * LANES]

    a_r = [abar_re[:, j * LANES:(j + 1) * LANES] for j in range(nj)]
    a_i = [abar_im[:, j * LANES:(j + 1) * LANES] for j in range(nj)]
    state = [(st_re[j], st_im[j]) for j in range(nj)]
    for s in range(n_sub):
        for t in range(sub):
            rows = slice(t * n_tiles, (t + 1) * n_tiles)
            for j in range(nj):
                sr, si = state[j]
                nr = a_r[j] * sr - a_i[j] * si + bu_re[s, j, rows, :]
                ni = a_r[j] * si + a_i[j] * sr + bu_im[s, j, rows, :]
                bu_re[s, j, rows, :] = nr
                bu_im[s, j, rows, :] = ni
                state[j] = (nr, ni)
    for j in range(nj):
        st_re[j], st_im[j] = state[j]

    for s in range(n_sub):
        rs = slice(s * sub, (s + 1) * sub)
        for k in range(n_tiles):
            sre = jnp.concatenate([bu_re[s, j, pl.ds(k, sub, stride=n_tiles), :] for j in range(nj)], axis=-1)
            sim = jnp.concatenate([bu_im[s, j, pl.ds(k, sub, stride=n_tiles), :] for j in range(nj)], axis=-1)
            y_scr[rs, k * uw:(k + 1) * uw] = (_dot(sre.astype(BF16), cdre_ref[k])
                                              - _dot(sim.astype(BF16), cdim_ref[k]))
        y = jax.nn.gelu(y_scr[rs, :] + d_ref[...] * u_scr[rs, :])
        z = y * jax.nn.sigmoid(_dot(y.astype(BF16), wglu_ref[...]) + bglu_ref[...])
        h = _dot(z.astype(BF16), wout_ref[...])
        o_ref[rs, :] = x_ref[rs, :] + _rms(h, gpost_ref[...])


def _s5_block_diag(b, c):
    G, P, HG = b.shape
    gt = S5_GROUPS_PER_TILE
    eye = jnp.eye(gt, dtype=b.dtype)
    bt = b.reshape(G // gt, gt, P, HG).transpose(0, 1, 3, 2)
    bd = jnp.einsum('kghp,gj->kghjp', bt, eye).reshape(G // gt, gt * HG, gt * P)
    ct = c.reshape(G // gt, gt, HG, P)
    cd = jnp.einsum('kghp,gj->kjpgh', ct, eye).reshape(G // gt, gt * P, gt * HG)
    return bd, cd


def _s5_mixer(x, gpre, gpost, w_in, a_re, a_im, log_dt, b_re, b_im, c_re, c_im, d_skip, w_glu, b_glu,
              w_out, tt=512, sub=256):
    L, D = x.shape
    G, P = a_re.shape
    bd_re, cd_re = _s5_block_diag(b_re, c_re)
    bd_im, cd_im = _s5_block_diag(b_im, c_im)
    n_tiles, uw, sw = bd_re.shape
    assert n_tiles == SUBLANES and uw == LANES and sw % LANES == 0
    nj = sw // LANES
    per_tile = lambda v: v.reshape(n_tiles, sw)
    return pl.pallas_call(
        _s5_body,
        grid=(L // tt,),
        in_specs=[
            pl.BlockSpec((tt, D), lambda i: (i, 0)),
            _resident((1, D)), _resident((1, D)), _resident((D, D)),
            _resident((n_tiles, sw)), _resident((n_tiles, sw)), _resident((n_tiles, sw)),
            _resident((n_tiles, uw, sw)), _resident((n_tiles, uw, sw)),
            _resident((n_tiles, sw, uw)), _resident((n_tiles, sw, uw)),
            _resident((1, D)), _resident((D, D)), _resident((1, D)), _resident((D, D)),
        ],
        out_specs=pl.BlockSpec((tt, D), lambda i: (i, 0)),
        out_shape=jax.ShapeDtypeStruct((L, D), F32),
        scratch_shapes=[
            pltpu.VMEM((n_tiles, sw), F32), pltpu.VMEM((n_tiles, sw), F32),
            pltpu.VMEM((n_tiles, uw, sw), BF16), pltpu.VMEM((n_tiles, uw, sw), BF16),
            pltpu.VMEM((nj, n_tiles, LANES), F32), pltpu.VMEM((nj, n_tiles, LANES), F32),
            pltpu.VMEM((tt // sub, nj, sub * n_tiles, LANES), F32),
            pltpu.VMEM((tt // sub, nj, sub * n_tiles, LANES), F32),
            pltpu.VMEM((tt, D), F32), pltpu.VMEM((tt, D), F32),
        ],
        compiler_params=_params("arbitrary"),
        name="s5_mixer",
    )(x, _row(gpre), _row(gpost), w_in.astype(BF16),
      per_tile(a_re), per_tile(a_im), per_tile(jnp.broadcast_to(log_dt[:, None], (G, P))),
      bd_re, bd_im, cd_re.astype(BF16), cd_im.astype(BF16),
      _row(d_skip), w_glu.astype(BF16), _row(b_glu), w_out.astype(BF16))


def _conv_body(x_ref, gpre_ref, gpost_ref, win_ref, bin_ref, dw_ref, dwb_ref,
               lng_ref, lnb_ref, wout_ref, bout_ref, o_ref, ext, conv_scr):
    tm, D = x_ref.shape
    halo = ext.shape[0] - tm
    n_taps = dw_ref.shape[0]

    @pl.when(pl.program_id(0) == 0)
    def _():
        ext[0:halo, :] = jnp.zeros((halo, D), F32)

    x = x_ref[...]
    xn = _rms(x, gpre_ref[...]).astype(BF16)
    a = _dot(xn, win_ref[:, :D]) + bin_ref[:, :D]
    g = _dot(xn, win_ref[:, D:]) + bin_ref[:, D:]
    ext[halo:halo + tm, :] = a * jax.nn.sigmoid(g)

    base = halo - (n_taps - 1)
    for c in range(D // LANES):
        cs = slice(c * LANES, (c + 1) * LANES)
        acc = None
        for s in range(SUBLANES):
            rows = tm if s == 0 else tm + SUBLANES
            q = None
            for a in range((base + n_taps - 1) // SUBLANES + 1):
                k = SUBLANES * a + s - base
                if 0 <= k < n_taps:
                    term = dw_ref[k:k + 1, cs] * ext[SUBLANES * a:SUBLANES * a + rows, cs]
                    q = term if q is None else q + term
            part = q if s == 0 else q[s:s + tm]
            acc = part if acc is None else acc + part
        conv_scr[:, cs] = acc
    ext[0:halo, :] = ext[tm:tm + halo, :]

    y = _layernorm(conv_scr[...] + dwb_ref[...], lng_ref[...], lnb_ref[...])
    y = y * jax.nn.sigmoid(y)
    h = _dot(y.astype(BF16), wout_ref[...]) + bout_ref[...]
    o_ref[...] = x + _rms(h, gpost_ref[...])


def _conv_mixer(x, gpre, gpost, w_in, b_in, dw, dw_b, ln_g, ln_b, w_out, b_out, tm=256):
    L, D = x.shape
    return pl.pallas_call(
        _conv_body,
        grid=(L // tm,),
        in_specs=[
            pl.BlockSpec((tm, D), lambda i: (i, 0)),
            _resident((1, D)), _resident((1, D)), _resident((D, 2 * D)), _resident((1, 2 * D)),
            _resident(dw.shape), _resident((1, D)), _resident((1, D)), _resident((1, D)),
            _resident((D, D)), _resident((1, D)),
        ],
        out_specs=pl.BlockSpec((tm, D), lambda i: (i, 0)),
        out_shape=jax.ShapeDtypeStruct((L, D), F32),
        scratch_shapes=[pltpu.VMEM((tm + CONV_HALO, D), F32), pltpu.VMEM((tm, D), F32)],
        compiler_params=_params("arbitrary"),
        name="conv_mixer",
    )(x, _row(gpre), _row(gpost), w_in.astype(BF16), _row(b_in), dw, _row(dw_b), _row(ln_g), _row(ln_b),
      w_out.astype(BF16), _row(b_out))


def _gmlp_body(x_ref, gpre_ref, gpost_ref, win_ref, bin_ref, lng_ref, lnb_ref, ws_ref, bs_ref,
               wout_ref, bout_ref, o_ref, z_scr, su_scr, *, fc):
    n_sub, sub, _ = su_scr.shape
    E = lng_ref.shape[1]
    n_heads, chunk, _ = ws_ref.shape
    hw = E // n_heads
    causal = (lax.broadcasted_iota(jnp.int32, (chunk, chunk), 1)
              <= lax.broadcasted_iota(jnp.int32, (chunk, chunk), 0))
    wm = [jnp.where(causal, ws_ref[h], 0.0).astype(BF16) for h in range(n_heads)]
    for i in range(n_sub):
        rows = slice(i * sub, (i + 1) * sub)
        x = x_ref[rows, :]
        xn = _rms(x, gpre_ref[...]).astype(BF16)
        for c in range(2 * E // fc):
            cs = slice(c * fc, (c + 1) * fc)
            z_scr[i, :, cs] = jax.nn.gelu(_dot(xn, win_ref[:, cs]) + bin_ref[:, cs])
        vb = _layernorm(z_scr[i, :, E:], lng_ref[...], lnb_ref[...]).astype(BF16)
        for h in range(n_heads):
            bias = bs_ref[:, h:h + 1]
            hs = slice(h * hw, (h + 1) * hw)
            for c in range(sub // chunk):
                rs = slice(c * chunk, (c + 1) * chunk)
                s = _dot(wm[h], vb[rs, hs]) + bias
                su_scr[i, rs, hs] = (z_scr[i, rs, hs] * s).astype(BF16)
        hout = _dot(su_scr[i], wout_ref[...]) + bout_ref[...]
        o_ref[rows, :] = x + _rms(hout, gpost_ref[...])


def _gmlp_mixer(x, gpre, gpost, w_in, b_in, ln_g, ln_b, w_s, b_s, w_out, b_out, tm=512, sub=512, fc=512):
    L, D = x.shape
    E = ln_g.shape[0]
    return pl.pallas_call(
        functools.partial(_gmlp_body, fc=fc),
        grid=(L // tm,),
        in_specs=[
            pl.BlockSpec((tm, D), lambda i: (i, 0)),
            _resident((1, D)), _resident((1, D)), _resident((D, 2 * E)), _resident((1, 2 * E)),
            _resident((1, E)), _resident((1, E)),
            _resident(w_s.shape), _resident((GM_CHUNK, GM_HEADS)),
            _resident((E, D)), _resident((1, D)),
        ],
        out_specs=pl.BlockSpec((tm, D), lambda i: (i, 0)),
        out_shape=jax.ShapeDtypeStruct((L, D), F32),
        scratch_shapes=[pltpu.VMEM((tm // sub, sub, 2 * E), F32), pltpu.VMEM((tm // sub, sub, E), BF16)],
        compiler_params=_params("parallel"),
        name="gmlp_mixer",
    )(x, _row(gpre), _row(gpost), w_in.astype(BF16), _row(b_in), _row(ln_g), _row(ln_b),
      w_s, b_s.T, w_out.astype(BF16), _row(b_out))


def _regroup_body(x_ref, g_ref, *refs):
    *o_refs, xt_scr = refs
    xn = _rms(x_ref[...], g_ref[...])
    for c in range(xt_scr.shape[0]):
        xt_scr[c] = xn[:, c * LANES:(c + 1) * LANES]
    for o_ref in o_refs:
        d, rows, _ = o_ref.shape
        if d == 1:
            o_ref[0] = xn.astype(o_ref.dtype)
            continue
        for r in range(d):
            o_ref[r] = jnp.concatenate([xt_scr[c, pl.ds(r, rows, stride=d), :] for c in range(xt_scr.shape[0])],
                                       axis=-1).astype(o_ref.dtype)


def _regroup_rows(x, g, dilations, tm=512):
    L, D = x.shape
    return pl.pallas_call(
        _regroup_body,
        grid=(L // tm,),
        in_specs=[pl.BlockSpec((tm, D), lambda i: (i, 0)), _resident((1, D))],
        out_specs=[pl.BlockSpec((d, tm // d, D), lambda i: (0, i, 0)) for d in dilations],
        out_shape=[jax.ShapeDtypeStruct((d, L // d, D), BF16) for d in dilations],
        scratch_shapes=[pltpu.VMEM((D // LANES, tm, LANES), F32)],
        compiler_params=_params("parallel"),
        name="regroup_rows",
    )(x, _row(g))


def _t5_buckets(dilation):
    delta = (np.arange(BLOCK)[:, None] + BLOCK) - np.arange(2 * BLOCK)[None, :]
    dist = np.maximum(delta, 0) * dilation
    max_exact = NUM_BUCKETS // 2
    distf = np.maximum(dist, 1).astype(np.float32)
    large = max_exact + (np.log(distf / np.float32(max_exact)) / np.float32(math.log(MAX_DISTANCE / max_exact))
                         * np.float32(NUM_BUCKETS - max_exact)).astype(np.int32)
    large = np.minimum(large, NUM_BUCKETS - 1)
    return np.where(dist < max_exact, dist, large).astype(np.int32)


def _attn_body(tab_ref, bkt_ref, xc_ref, xn_ref, w_ref, o_ref, lse_ref,
               bias_scr, q_cur, k_win, v_win, q_nxt, k_nxt, v_nxt, s_scr, p_scr, *, scale):
    blk = bias_scr.shape[2]
    n_heads = bias_scr.shape[1]
    D = q_cur.shape[1]
    hd = D // n_heads
    qb = q_cur.shape[0] // blk
    n = pl.program_id(1)

    def project(x_ref, q_dst, k_dst, v_dst, k_off):
        xb = x_ref[...]
        q_dst[...] = (_dot(xb, w_ref[:, 0:D]) * scale).astype(BF16)
        k_dst[k_off:k_off + qb * blk, :] = _dot(xb, w_ref[:, D:2 * D]).astype(BF16)
        v_dst[k_off:k_off + qb * blk, :] = _dot(xb, w_ref[:, 2 * D:3 * D]).astype(BF16)

    @pl.when(n == 0)
    def _():
        k_win[0:blk, :] = jnp.zeros((blk, D), BF16)
        v_win[0:blk, :] = jnp.zeros((blk, D), BF16)

    @pl.when((pl.program_id(0) == 0) & (n == 0))
    def _():
        project(xc_ref, q_cur, k_win, v_win, blk)

    @pl.when((pl.program_id(0) == 0) & (n == 0))
    def _():
        bkt = bkt_ref[...]
        qi = lax.broadcasted_iota(jnp.int32, bkt.shape, 0)
        ki = lax.broadcasted_iota(jnp.int32, bkt.shape, 1)
        delta = qi + blk - ki
        in_band = (delta >= 0) & (delta <= blk)
        for h in range(n_heads):
            acc = jnp.zeros(bkt.shape, F32)
            for b in range(tab_ref.shape[0]):
                acc = jnp.where(bkt == b, tab_ref[b, h], acc)
            acc = jnp.where(in_band, acc, NEG_BIG)
            bias_scr[1, h] = acc
            bias_scr[0, h] = jnp.where(ki >= blk, acc, NEG_BIG)

    project(xn_ref, q_nxt, k_nxt, v_nxt, 0)

    dims = (((1,), (1,)), ((), ()))
    per_tile = LANES // hd
    lane = lax.broadcasted_iota(jnp.int32, (1, LANES), 1)
    own = [(lane >= i * hd) & (lane < (i + 1) * hd) for i in range(per_tile)]
    n_tiles = n_heads // per_tile

    for b in range(qb):
        rows = slice(b * blk, (b + 1) * blk)
        keys = slice(b * blk, (b + 2) * blk)
        slot = jnp.minimum(n, 1) if b == 0 else 1
        for t in range(n_tiles):
            cs = slice(t * LANES, (t + 1) * LANES)
            qt = q_cur[rows, cs]
            kt = k_win[keys, cs]
            for i in range(per_tile):
                h = t * per_tile + i
                qh = jnp.where(own[i], qt, jnp.zeros_like(qt))
                s_scr[b, h] = lax.dot_general(qh, kt, dims, preferred_element_type=F32) + bias_scr[slot, h]

        lse_b = jnp.zeros((blk, LANES), F32)
        for h in range(n_heads):
            s = s_scr[b, h]
            m = jnp.max(s, axis=-1, keepdims=True)
            e = jnp.exp(s - m)
            den = jnp.sum(e, axis=-1, keepdims=True)
            p_scr[b, h] = (e * (1.0 / den)).astype(BF16)
            lse_b = jnp.where(lane == h, m + jnp.log(den), lse_b)
        lse_ref[rows, :] = lse_b

        for t in range(n_tiles):
            cs = slice(t * LANES, (t + 1) * LANES)
            vt = v_win[keys, cs]
            acc = jnp.zeros((blk, LANES), F32)
            for i in range(per_tile):
                acc = acc + _dot(p_scr[b, t * per_tile + i], jnp.where(own[i], vt, jnp.zeros_like(vt)))
            o_ref[rows, cs] = acc

    k_win[0:blk, :] = k_win[qb * blk:(qb + 1) * blk, :]
    v_win[0:blk, :] = v_win[qb * blk:(qb + 1) * blk, :]
    k_win[blk:(qb + 1) * blk, :] = k_nxt[...]
    v_win[blk:(qb + 1) * blk, :] = v_nxt[...]
    q_cur[...] = q_nxt[...]


def _dilated_attention(xn, w_qkv, pattern, table, dilation, qb=2):
    d, rows, D = xn.shape
    n_heads = D // HEAD_DIM
    tq = qb * BLOCK
    n_steps = rows // tq
    out = lambda w: pl.BlockSpec((None, tq, w), lambda r, n: (r, n, 0))

    def following(r, n):
        nxt = jnp.minimum(r * n_steps + n + 1, d * n_steps - 1)
        return nxt // n_steps, nxt % n_steps, 0

    return pl.pallas_call(
        functools.partial(_attn_body, scale=HEAD_DIM ** -0.5),
        grid=(d, n_steps),
        in_specs=[
            pl.BlockSpec(memory_space=pltpu.SMEM),
            pl.BlockSpec((BLOCK, 2 * BLOCK), lambda r, n: (0, 0), pipeline_mode=pl.Buffered(1)),
            pl.BlockSpec((None, tq, D), lambda r, n: (0, 0, 0)),
            pl.BlockSpec((None, tq, D), following),
            pl.BlockSpec((D, 3 * D), lambda r, n: (0, pattern), pipeline_mode=pl.Buffered(1)),
        ],
        out_specs=[out(D), out(LANES)],
        out_shape=[jax.ShapeDtypeStruct((d, rows, D), F32), jax.ShapeDtypeStruct((d, rows, LANES), F32)],
        scratch_shapes=[pltpu.VMEM((2, n_heads, BLOCK, 2 * BLOCK), F32),
                        pltpu.VMEM((tq, D), BF16),
                        pltpu.VMEM((tq + BLOCK, D), BF16), pltpu.VMEM((tq + BLOCK, D), BF16),
                        pltpu.VMEM((tq, D), BF16), pltpu.VMEM((tq, D), BF16), pltpu.VMEM((tq, D), BF16),
                        pltpu.VMEM((qb, n_heads, BLOCK, 2 * BLOCK), F32),
                        pltpu.VMEM((qb, n_heads, BLOCK, 2 * BLOCK), BF16)],
        compiler_params=_params("arbitrary", "arbitrary"),
        name=f"dilated_attn_d{dilation}",
    )(table, jnp.asarray(_t5_buckets(dilation)), xn, xn, w_qkv)


def _attn_out_body(x_ref, gpost_ref, o0_ref, o1_ref, o2_ref, l0_ref, l1_ref, l2_ref, wout_ref, out_ref, *scr):
    def natural(ref, s):
        d, rows, width = ref.shape
        if d == 1:
            return ref[0]
        n_lt = width // LANES
        for c in range(n_lt):
            for r in range(d):
                s[c, pl.ds(r, rows, stride=d), :] = ref[r, :, c * LANES:(c + 1) * LANES]
        return jnp.concatenate([s[c] for c in range(n_lt)], axis=-1)

    D = x_ref.shape[1]
    hd = D // (D // HEAD_DIM)
    spread = (lax.broadcasted_iota(jnp.int32, (LANES, D), 1) // hd
              == lax.broadcasted_iota(jnp.int32, (LANES, D), 0)).astype(BF16)

    def per_column(w):
        hi = w.astype(BF16)
        lo = (w - hi.astype(F32)).astype(BF16)
        return _dot(hi, spread) + _dot(lo, spread)

    l0, l1, l2 = natural(l0_ref, None), natural(l1_ref, scr[0]), natural(l2_ref, scr[1])
    m = jnp.maximum(jnp.maximum(l0, l1), l2)
    e0, e1, e2 = jnp.exp(l0 - m), jnp.exp(l1 - m), jnp.exp(l2 - m)
    inv = 1.0 / (e0 + e1 + e2)
    o = (per_column(e0 * inv) * natural(o0_ref, None) + per_column(e1 * inv) * natural(o1_ref, scr[2])
         + per_column(e2 * inv) * natural(o2_ref, scr[3]))
    h = _dot(o.astype(BF16), wout_ref[...])
    out_ref[...] = x_ref[...] + _rms(h, gpost_ref[...])


def _attention_mixer(x, gpre, gpost, w_qkv, w_out, rel_bias, tm=512):
    L, D = x.shape
    n_heads = D // HEAD_DIM
    outs, lses = [], []
    w_qkv = w_qkv.astype(BF16)
    xns = _regroup_rows(x, gpre, [d for _, d in PATTERNS])
    for g, (window, dilation) in enumerate(PATTERNS):
        assert window // dilation == BLOCK and L % (BLOCK * dilation) == 0
        o, lse = _dilated_attention(xns[g], w_qkv, g, rel_bias[:, g * n_heads:(g + 1) * n_heads], dilation)
        outs.append(o)
        lses.append(lse)
    tile = pl.BlockSpec((tm, D), lambda i: (i, 0))
    grouped = lambda w: [pl.BlockSpec((d, tm // d, w), lambda i: (0, i, 0)) for _, d in PATTERNS]
    return pl.pallas_call(
        _attn_out_body,
        grid=(L // tm,),
        in_specs=[tile, _resident((1, D))] + grouped(D) + grouped(LANES) + [_resident((D, D))],
        out_specs=tile,
        out_shape=jax.ShapeDtypeStruct((L, D), F32),
        scratch_shapes=[pltpu.VMEM((1, tm, LANES), F32)] * 2 + [pltpu.VMEM((D // LANES, tm, LANES), F32)] * 2,
        compiler_params=_params("parallel"),
        name="attn_out",
    )(x, _row(gpost), *outs, *lses, w_out.astype(BF16))


def kernel(x, norm_pre, norm_post, ffn_w1, ffn_w3, ffn_w2, rel_bias, s5_w_in, s5_a_re, s5_a_im, s5_log_dt, s5_b_re, s5_b_im, s5_c_re, s5_c_im, s5_d, s5_w_glu, s5_b_glu, s5_w_out, cv_w_in, cv_b_in, cv_dw, cv_dw_b, cv_ln_g, cv_ln_b, cv_w_out, cv_b_out, gm_w_in, gm_b_in, gm_ln_g, gm_ln_b, gm_w_s, gm_b_s, gm_w_out, gm_b_out, at_w_qkv, at_w_out):
    bsz, seq, d_model = x.shape
    assert bsz == 1, "sequence mixers carry state along the row axis of one sequence"
    depth = norm_pre.shape[0]
    n_mixers = 4
    h = x.reshape(seq, d_model)
    w1, w3, w2 = ffn_w1.astype(BF16), ffn_w3.astype(BF16), ffn_w2.astype(BF16)
    for i in range(depth):
        kind, j = i % n_mixers, i // n_mixers
        h = _ffn(h, norm_pre[i, 0], norm_post[i, 0], w1, w3, w2, i, 0)
        if kind == 0:
            h = _s5_mixer(h, norm_pre[i, 1], norm_post[i, 1], s5_w_in[j], s5_a_re[j], s5_a_im[j], s5_log_dt[j],
                          s5_b_re[j], s5_b_im[j], s5_c_re[j], s5_c_im[j], s5_d[j], s5_w_glu[j], s5_b_glu[j],
                          s5_w_out[j])
        elif kind == 1:
            h = _conv_mixer(h, norm_pre[i, 1], norm_post[i, 1], cv_w_in[j], cv_b_in[j], cv_dw[j], cv_dw_b[j],
                            cv_ln_g[j], cv_ln_b[j], cv_w_out[j], cv_b_out[j])
        elif kind == 2:
            h = _gmlp_mixer(h, norm_pre[i, 1], norm_post[i, 1], gm_w_in[j], gm_b_in[j], gm_ln_g[j], gm_ln_b[j],
                            gm_w_s[j], gm_b_s[j], gm_w_out[j], gm_b_out[j])
        else:
            h = _attention_mixer(h, norm_pre[i, 1], norm_post[i, 1], at_w_qkv[j], at_w_out[j], rel_bias)
        h = _ffn(h, norm_pre[i, 2], norm_post[i, 2], w1, w3, w2, i, 1)
    return h.reshape(bsz, seq, d_model)
```

```python
import functools
import math

import numpy as np
import jax
import jax.numpy as jnp
from jax import lax
from jax.experimental import pallas as pl
from jax.experimental.pallas import tpu as pltpu

F32 = jnp.float32
BF16 = jnp.bfloat16

EPS = 1e-6
LANES = 128
SUBLANES = 8
S5_GROUP = 16
S5_STATE = 64
S5_GROUPS_PER_TILE = 8
CONV_W = 31
CONV_HALO = 32
GM_CHUNK = 128
GM_HEADS = 8
HEAD_DIM = 64
PATTERNS = ((128, 1), (512, 4), (2048, 16))
BLOCK = 128
NUM_BUCKETS = 32
MAX_DISTANCE = 2048
NEG_BIG = -1e30

VMEM_LIMIT_BYTES = 56 * 1024 * 1024


def _params(*sem):
    return pltpu.CompilerParams(dimension_semantics=sem, vmem_limit_bytes=VMEM_LIMIT_BYTES)


def _rms(x, g):
    return x * lax.rsqrt(jnp.mean(x * x, axis=-1, keepdims=True) + EPS) * g


def _layernorm(x, g, b):
    mu = jnp.mean(x, axis=-1, keepdims=True)
    xc = x - mu
    var = jnp.mean(xc * xc, axis=-1, keepdims=True)
    return xc * lax.rsqrt(var + EPS) * g + b


def _dot(a, b):
    return jnp.dot(a, b, preferred_element_type=F32)


def _row(v):
    return v.reshape(1, -1)


def _ffn_body(x_ref, gpre_ref, gpost_ref, w1_ref, w3_ref, w2_ref, *refs, fc):
    n_cast = (len(refs) - 2) // 2
    src_refs, o_ref, dst_refs, g_scr = refs[:n_cast], refs[n_cast], refs[n_cast + 1:-1], refs[-1]
    for src, dst in zip(src_refs, dst_refs):
        dst[...] = src[...].astype(dst.dtype)
    n_sub, sub, F = g_scr.shape
    for s in range(n_sub):
        rs = slice(s * sub, (s + 1) * sub)
        xn = _rms(x_ref[rs, :], gpre_ref[...]).astype(BF16)
        for c in range(F // fc):
            cs = slice(c * fc, (c + 1) * fc)
            h1 = _dot(xn, w1_ref[:, cs])
            h3 = _dot(xn, w3_ref[:, cs])
            g_scr[s, :, cs] = ((h1 * jax.nn.sigmoid(h1)) * h3).astype(BF16)
        h = _dot(g_scr[s], w2_ref[...])
        o_ref[rs, :] = x_ref[rs, :] + 0.5 * _rms(h, gpost_ref[...])


def _resident(shape):
    return pl.BlockSpec(shape, lambda i: (0,) * len(shape), pipeline_mode=pl.Buffered(1))


def _ffn(x, gpre, gpost, weights, following=None, tm=1024, sub=512, fc=256):
    L, D = x.shape
    w1, w3, w2 = weights
    F = w1.shape[-1]
    n_steps = L // tm
    in_specs = [pl.BlockSpec((tm, D), lambda i: (i, 0)), _resident((1, D)), _resident((1, D)),
                _resident((D, F)), _resident((D, F)), _resident((F, D))]
    out_specs = [pl.BlockSpec((tm, D), lambda i: (i, 0))]
    out_shape = [jax.ShapeDtypeStruct((L, D), F32)]
    operands = [x, _row(gpre), _row(gpost), w1, w3, w2]
    if following is not None:
        stacks, layer, half = following
        for w in stacks:
            rows, cols = w.shape[-2] // n_steps, w.shape[-1]
            in_specs.append(pl.BlockSpec((None, None, rows, cols), lambda i: (layer, half, i, 0)))
            out_specs.append(pl.BlockSpec((rows, cols), lambda i: (i, 0)))
            out_shape.append(jax.ShapeDtypeStruct(w.shape[-2:], BF16))
            operands.append(w)
    outs = pl.pallas_call(
        functools.partial(_ffn_body, fc=fc),
        grid=(n_steps,),
        in_specs=in_specs,
        out_specs=out_specs,
        out_shape=out_shape,
        scratch_shapes=[pltpu.VMEM((tm // sub, sub, F), BF16)],
        compiler_params=_params("arbitrary"),
        name="ffn",
    )(*operands)
    return outs[0], (tuple(outs[1:]) if following is not None else None)


def _s5_body(x_ref, gpre_ref, gpost_ref, win_ref, are_ref, aim_ref, ldt_ref, bdre_ref, bdim_ref,
             cdre_ref, cdim_ref, d_ref, wglu_ref, bglu_ref, wout_ref, o_ref,
             abar_re, abar_im, bb_re, bb_im, st_re, st_im, bu_re, bu_im, u_scr, y_scr):
    tt = x_ref.shape[0]
    n_tiles, uw, sw = bdre_ref.shape
    nj = sw // LANES

    @pl.when(pl.program_id(0) == 0)
    def _():
        ar = are_ref[...]
        ai = aim_ref[...]
        dt = jnp.exp(ldt_ref[...])
        mag = jnp.exp(dt * ar)
        abr = mag * jnp.cos(dt * ai)
        abi = mag * jnp.sin(dt * ai)
        den = ar * ar + ai * ai
        nr = abr - 1.0
        fre = (nr * ar + abi * ai) / den
        fim = (abi * ar - nr * ai) / den
        abar_re[...] = abr
        abar_im[...] = abi
        for k in range(n_tiles):
            fr = fre[k:k + 1, :]
            fi = fim[k:k + 1, :]
            bb_re[k] = (fr * bdre_ref[k] - fi * bdim_ref[k]).astype(BF16)
            bb_im[k] = (fr * bdim_ref[k] + fi * bdre_ref[k]).astype(BF16)
        st_re[...] = jnp.zeros_like(st_re)
        st_im[...] = jnp.zeros_like(st_im)

    n_sub = bu_re.shape[0]
    sub = tt // n_sub
    for s in range(n_sub):
        rs = slice(s * sub, (s + 1) * sub)
        xn = _rms(x_ref[rs, :], gpre_ref[...]).astype(BF16)
        u = _dot(xn, win_ref[...])
        u_scr[rs, :] = u
        ub = u.astype(BF16)
        for k in range(n_tiles):
            uk = ub[:, k * uw:(k + 1) * uw]
            br = _dot(uk, bb_re[k])
            bi = _dot(uk, bb_im[k])
            for j in range(nj):
                bu_re[s, j, pl.ds(k, sub, stride=n_tiles), :] = br[:, j * LANES:(j + 1) * LANES]
                bu_im[s, j, pl.ds(k, sub, stride=n_tiles), :] = bi[:, j * LANES:(j + 1) * LANES]

    a_r = [abar_re[:, j * LANES:(j + 1) * LANES] for j in range(nj)]
    a_i = [abar_im[:, j * LANES:(j + 1) * LANES] for j in range(nj)]
    state = [(st_re[j], st_im[j]) for j in range(nj)]
    for s in range(n_sub):
        for t in range(sub):
            rows = slice(t * n_tiles, (t + 1) * n_tiles)
            for j in range(nj):
                sr, si = state[j]
                nr = a_r[j] * sr - a_i[j] * si + bu_re[s, j, rows, :]
                ni = a_r[j] * si + a_i[j] * sr + bu_im[s, j, rows, :]
                bu_re[s, j, rows, :] = nr
                bu_im[s, j, rows, :] = ni
                state[j] = (nr, ni)
    for j in range(nj):
        st_re[j], st_im[j] = state[j]

    for s in range(n_sub):
        rs = slice(s * sub, (s + 1) * sub)
        for k in range(n_tiles):
            sre = jnp.concatenate([bu_re[s, j, pl.ds(k, sub, stride=n_tiles), :] for j in range(nj)], axis=-1)
            sim = jnp.concatenate([bu_im[s, j, pl.ds(k, sub, stride=n_tiles), :] for j in range(nj)], axis=-1)
            y_scr[rs, k * uw:(k + 1) * uw] = (_dot(sre.astype(BF16), cdre_ref[k])
                                              - _dot(sim.astype(BF16), cdim_ref[k]))
        y = jax.nn.gelu(y_scr[rs, :] + d_ref[...] * u_scr[rs, :])
        z = y * jax.nn.sigmoid(_dot(y.astype(BF16), wglu_ref[...]) + bglu_ref[...])
        h = _dot(z.astype(BF16), wout_ref[...])
        o_ref[rs, :] = x_ref[rs, :] + _rms(h, gpost_ref[...])


def _s5_block_diag(b, c):
    G, P, HG = b.shape
    gt = S5_GROUPS_PER_TILE
    eye = jnp.eye(gt, dtype=b.dtype)
    bt = b.reshape(G // gt, gt, P, HG).transpose(0, 1, 3, 2)
    bd = jnp.einsum('kghp,gj->kghjp', bt, eye).reshape(G // gt, gt * HG, gt * P)
    ct = c.reshape(G // gt, gt, HG, P)
    cd = jnp.einsum('kghp,gj->kjpgh', ct, eye).reshape(G // gt, gt * P, gt * HG)
    return bd, cd


def _s5_mixer(x, gpre, gpost, w_in, a_re, a_im, log_dt, b_re, b_im, c_re, c_im, d_skip, w_glu, b_glu,
              w_out, tt=512, sub=256):
    L, D = x.shape
    G, P = a_re.shape
    bd_re, cd_re = _s5_block_diag(b_re, c_re)
    bd_im, cd_im = _s5_block_diag(b_im, c_im)
    n_tiles, uw, sw = bd_re.shape
    assert n_tiles == SUBLANES and uw == LANES and sw % LANES == 0
    nj = sw // LANES
    per_tile = lambda v: v.reshape(n_tiles, sw)
    return pl.pallas_call(
        _s5_body,
        grid=(L // tt,),
        in_specs=[
            pl.BlockSpec((tt, D), lambda i: (i, 0)),
            _resident((1, D)), _resident((1, D)), _resident((D, D)),
            _resident((n_tiles, sw)), _resident((n_tiles, sw)), _resident((n_tiles, sw)),
            _resident((n_tiles, uw, sw)), _resident((n_tiles, uw, sw)),
            _resident((n_tiles, sw, uw)), _resident((n_tiles, sw, uw)),
            _resident((1, D)), _resident((D, D)), _resident((1, D)), _resident((D, D)),
        ],
        out_specs=pl.BlockSpec((tt, D), lambda i: (i, 0)),
        out_shape=jax.ShapeDtypeStruct((L, D), F32),
        scratch_shapes=[
            pltpu.VMEM((n_tiles, sw), F32), pltpu.VMEM((n_tiles, sw), F32),
            pltpu.VMEM((n_tiles, uw, sw), BF16), pltpu.VMEM((n_tiles, uw, sw), BF16),
            pltpu.VMEM((nj, n_tiles, LANES), F32), pltpu.VMEM((nj, n_tiles, LANES), F32),
            pltpu.VMEM((tt // sub, nj, sub * n_tiles, LANES), F32),
            pltpu.VMEM((tt // sub, nj, sub * n_tiles, LANES), F32),
            pltpu.VMEM((tt, D), F32), pltpu.VMEM((tt, D), F32),
        ],
        compiler_params=_params("arbitrary"),
        name="s5_mixer",
    )(x, _row(gpre), _row(gpost), w_in.astype(BF16),
      per_tile(a_re), per_tile(a_im), per_tile(jnp.broadcast_to(log_dt[:, None], (G, P))),
      bd_re, bd_im, cd_re.astype(BF16), cd_im.astype(BF16),
      _row(d_skip), w_glu.astype(BF16), _row(b_glu), w_out.astype(BF16))


def _conv_body(x_ref, gpre_ref, gpost_ref, win_ref, bin_ref, dw_ref, dwb_ref,
               lng_ref, lnb_ref, wout_ref, bout_ref, o_ref, ext, conv_scr):
    tm, D = x_ref.shape
    halo = ext.shape[0] - tm
    n_taps = dw_ref.shape[0]

    @pl.when(pl.program_id(0) == 0)
    def _():
        ext[0:halo, :] = jnp.zeros((halo, D), F32)

    x = x_ref[...]
    xn = _rms(x, gpre_ref[...]).astype(BF16)
    a = _dot(xn, win_ref[:, :D]) + bin_ref[:, :D]
    g = _dot(xn, win_ref[:, D:]) + bin_ref[:, D:]
    ext[halo:halo + tm, :] = a * jax.nn.sigmoid(g)

    base = halo - (n_taps - 1)
    for c in range(D // LANES):
        cs = slice(c * LANES, (c + 1) * LANES)
        acc = None
        for s in range(SUBLANES):
            rows = tm if s == 0 else tm + SUBLANES
            q = None
            for a in range((base + n_taps - 1) // SUBLANES + 1):
                k = SUBLANES * a + s - base
                if 0 <= k < n_taps:
                    term = dw_ref[k:k + 1, cs] * ext[SUBLANES * a:SUBLANES * a + rows, cs]
                    q = term if q is None else q + term
            part = q if s == 0 else q[s:s + tm]
            acc = part if acc is None else acc + part
        conv_scr[:, cs] = acc
    ext[0:halo, :] = ext[tm:tm + halo, :]

    y = _layernorm(conv_scr[...] + dwb_ref[...], lng_ref[...], lnb_ref[...])
    y = y * jax.nn.sigmoid(y)
    h = _dot(y.astype(BF16), wout_ref[...]) + bout_ref[...]
    o_ref[...] = x + _rms(h, gpost_ref[...])


def _conv_mixer(x, gpre, gpost, w_in, b_in, dw, dw_b, ln_g, ln_b, w_out, b_out, tm=256):
    L, D = x.shape
    return pl.pallas_call(
        _conv_body,
        grid=(L // tm,),
        in_specs=[
            pl.BlockSpec((tm, D), lambda i: (i, 0)),
            _resident((1, D)), _resident((1, D)), _resident((D, 2 * D)), _resident((1, 2 * D)),
            _resident(dw.shape), _resident((1, D)), _resident((1, D)), _resident((1, D)),
            _resident((D, D)), _resident((1, D)),
        ],
        out_specs=pl.BlockSpec((tm, D), lambda i: (i, 0)),
        out_shape=jax.ShapeDtypeStruct((L, D), F32),
        scratch_shapes=[pltpu.VMEM((tm + CONV_HALO, D), F32), pltpu.VMEM((tm, D), F32)],
        compiler_params=_params("arbitrary"),
        name="conv_mixer",
    )(x, _row(gpre), _row(gpost), w_in.astype(BF16), _row(b_in), dw, _row(dw_b), _row(ln_g), _row(ln_b),
      w_out.astype(BF16), _row(b_out))


def _gmlp_body(x_ref, gpre_ref, gpost_ref, win_ref, bin_ref, lng_ref, lnb_ref, ws_ref, bs_ref,
               wout_ref, bout_ref, o_ref, z_scr, su_scr, *, fc):
    n_sub, sub, _ = su_scr.shape
    E = lng_ref.shape[1]
    n_heads, chunk, _ = ws_ref.shape
    hw = E // n_heads
    causal = (lax.broadcasted_iota(jnp.int32, (chunk, chunk), 1)
              <= lax.broadcasted_iota(jnp.int32, (chunk, chunk), 0))
    wm = [jnp.where(causal, ws_ref[h], 0.0).astype(BF16) for h in range(n_heads)]
    for i in range(n_sub):
        rows = slice(i * sub, (i + 1) * sub)
        x = x_ref[rows, :]
        xn = _rms(x, gpre_ref[...]).astype(BF16)
        for c in range(2 * E // fc):
            cs = slice(c * fc, (c + 1) * fc)
            z_scr[i, :, cs] = jax.nn.gelu(_dot(xn, win_ref[:, cs]) + bin_ref[:, cs])
        vb = _layernorm(z_scr[i, :, E:], lng_ref[...], lnb_ref[...]).astype(BF16)
        for h in range(n_heads):
            bias = bs_ref[:, h:h + 1]
            hs = slice(h * hw, (h + 1) * hw)
            for c in range(sub // chunk):
                rs = slice(c * chunk, (c + 1) * chunk)
                s = _dot(wm[h], vb[rs, hs]) + bias
                su_scr[i, rs, hs] = (z_scr[i, rs, hs] * s).astype(BF16)
        hout = _dot(su_scr[i], wout_ref[...]) + bout_ref[...]
        o_ref[rows, :] = x + _rms(hout, gpost_ref[...])


def _gmlp_mixer(x, gpre, gpost, w_in, b_in, ln_g, ln_b, w_s, b_s, w_out, b_out, tm=512, sub=512, fc=512):
    L, D = x.shape
    E = ln_g.shape[0]
    return pl.pallas_call(
        functools.partial(_gmlp_body, fc=fc),
        grid=(L // tm,),
        in_specs=[
            pl.BlockSpec((tm, D), lambda i: (i, 0)),
            _resident((1, D)), _resident((1, D)), _resident((D, 2 * E)), _resident((1, 2 * E)),
            _resident((1, E)), _resident((1, E)),
            _resident(w_s.shape), _resident((GM_CHUNK, GM_HEADS)),
            _resident((E, D)), _resident((1, D)),
        ],
        out_specs=pl.BlockSpec((tm, D), lambda i: (i, 0)),
        out_shape=jax.ShapeDtypeStruct((L, D), F32),
        scratch_shapes=[pltpu.VMEM((tm // sub, sub, 2 * E), F32), pltpu.VMEM((tm // sub, sub, E), BF16)],
        compiler_params=_params("parallel"),
        name="gmlp_mixer",
    )(x, _row(gpre), _row(gpost), w_in.astype(BF16), _row(b_in), _row(ln_g), _row(ln_b),
      w_s, b_s.T, w_out.astype(BF16), _row(b_out))


def _regroup_body(x_ref, g_ref, *refs):
    *o_refs, xt_scr = refs
    xn = _rms(x_ref[...], g_ref[...])
    for c in range(xt_scr.shape[0]):
        xt_scr[c] = xn[:, c * LANES:(c + 1) * LANES]
    for o_ref in o_refs:
        d, rows, _ = o_ref.shape
        if d == 1:
            o_ref[0] = xn.astype(o_ref.dtype)
            continue
        for r in range(d):
            o_ref[r] = jnp.concatenate([xt_scr[c, pl.ds(r, rows, stride=d), :] for c in range(xt_scr.shape[0])],
                                       axis=-1).astype(o_ref.dtype)


def _regroup_rows(x, g, dilations, tm=512):
    L, D = x.shape
    return pl.pallas_call(
        _regroup_body,
        grid=(L // tm,),
        in_specs=[pl.BlockSpec((tm, D), lambda i: (i, 0)), _resident((1, D))],
        out_specs=[pl.BlockSpec((d, tm // d, D), lambda i: (0, i, 0)) for d in dilations],
        out_shape=[jax.ShapeDtypeStruct((d, L // d, D), BF16) for d in dilations],
        scratch_shapes=[pltpu.VMEM((D // LANES, tm, LANES), F32)],
        compiler_params=_params("parallel"),
        name="regroup_rows",
    )(x, _row(g))


def _t5_buckets(dilation):
    delta = (np.arange(BLOCK)[:, None] + BLOCK) - np.arange(2 * BLOCK)[None, :]
    dist = np.maximum(delta, 0) * dilation
    max_exact = NUM_BUCKETS // 2
    distf = np.maximum(dist, 1).astype(np.float32)
    large = max_exact + (np.log(distf / np.float32(max_exact)) / np.float32(math.log(MAX_DISTANCE / max_exact))
                         * np.float32(NUM_BUCKETS - max_exact)).astype(np.int32)
    large = np.minimum(large, NUM_BUCKETS - 1)
    return np.where(dist < max_exact, dist, large).astype(np.int32)


def _attn_body(tab_ref, bkt_ref, xc_ref, xn_ref, w_ref, o_ref, lse_ref,
               bias_scr, q_cur, k_win, v_win, q_nxt, k_nxt, v_nxt, s_scr, p_scr, *, scale):
    blk = bias_scr.shape[2]
    n_heads = bias_scr.shape[1]
    D = q_cur.shape[1]
    hd = D // n_heads
    qb = q_cur.shape[0] // blk
    n = pl.program_id(1)

    def project(x_ref, q_dst, k_dst, v_dst, k_off):
        xb = x_ref[...]
        q_dst[...] = (_dot(xb, w_ref[:, 0:D]) * scale).astype(BF16)
        k_dst[k_off:k_off + qb * blk, :] = _dot(xb, w_ref[:, D:2 * D]).astype(BF16)
        v_dst[k_off:k_off + qb * blk, :] = _dot(xb, w_ref[:, 2 * D:3 * D]).astype(BF16)

    @pl.when(n == 0)
    def _():
        k_win[0:blk, :] = jnp.zeros((blk, D), BF16)
        v_win[0:blk, :] = jnp.zeros((blk, D), BF16)

    @pl.when((pl.program_id(0) == 0) & (n == 0))
    def _():
        project(xc_ref, q_cur, k_win, v_win, blk)

    @pl.when((pl.program_id(0) == 0) & (n == 0))
    def _():
        bkt = bkt_ref[...]
        qi = lax.broadcasted_iota(jnp.int32, bkt.shape, 0)
        ki = lax.broadcasted_iota(jnp.int32, bkt.shape, 1)
        delta = qi + blk - ki
        in_band = (delta >= 0) & (delta <= blk)
        for h in range(n_heads):
            acc = jnp.zeros(bkt.shape, F32)
            for b in range(tab_ref.shape[0]):
                acc = jnp.where(bkt == b, tab_ref[b, h], acc)
            acc = jnp.where(in_band, acc, NEG_BIG)
            bias_scr[1, h] = acc
            bias_scr[0, h] = jnp.where(ki >= blk, acc, NEG_BIG)

    project(xn_ref, q_nxt, k_nxt, v_nxt, 0)

    dims = (((1,), (1,)), ((), ()))
    per_tile = LANES // hd
    lane = lax.broadcasted_iota(jnp.int32, (1, LANES), 1)
    own = [(lane >= i * hd) & (lane < (i + 1) * hd) for i in range(per_tile)]
    n_tiles = n_heads // per_tile

    for b in range(qb):
        rows = slice(b * blk, (b + 1) * blk)
        keys = slice(b * blk, (b + 2) * blk)
        slot = jnp.minimum(n, 1) if b == 0 else 1
        for t in range(n_tiles):
            cs = slice(t * LANES, (t + 1) * LANES)
            qt = q_cur[rows, cs]
            kt = k_win[keys, cs]
            for i in range(per_tile):
                h = t * per_tile + i
                qh = jnp.where(own[i], qt, jnp.zeros_like(qt))
                s_scr[b, h] = lax.dot_general(qh, kt, dims, preferred_element_type=F32) + bias_scr[slot, h]

        lse_b = jnp.zeros((blk, LANES), F32)
        for h in range(n_heads):
            s = s_scr[b, h]
            m = jnp.max(s, axis=-1, keepdims=True)
            e = jnp.exp(s - m)
            den = jnp.sum(e, axis=-1, keepdims=True)
            p_scr[b, h] = (e * (1.0 / den)).astype(BF16)
            lse_b = jnp.where(lane == h, m + jnp.log(den), lse_b)
        lse_ref[rows, :] = lse_b

        for t in range(n_tiles):
            cs = slice(t * LANES, (t + 1) * LANES)
            vt = v_win[keys, cs]
            acc = jnp.zeros((blk, LANES), F32)
            for i in range(per_tile):
                acc = acc + _dot(p_scr[b, t * per_tile + i], jnp.where(own[i], vt, jnp.zeros_like(vt)))
            o_ref[rows, cs] = acc

    k_win[0:blk, :] = k_win[qb * blk:(qb + 1) * blk, :]
    v_win[0:blk, :] = v_win[qb * blk:(qb + 1) * blk, :]
    k_win[blk:(qb + 1) * blk, :] = k_nxt[...]
    v_win[blk:(qb + 1) * blk, :] = v_nxt[...]
    q_cur[...] = q_nxt[...]


def _dilated_attention(xn, w_qkv, pattern, table, dilation, qb=2):
    d, rows, D = xn.shape
    n_heads = D // HEAD_DIM
    tq = qb * BLOCK
    n_steps = rows // tq
    out = lambda w: pl.BlockSpec((None, tq, w), lambda r, n: (r, n, 0))

    def following(r, n):
        nxt = jnp.minimum(r * n_steps + n + 1, d * n_steps - 1)
        return nxt // n_steps, nxt % n_steps, 0

    return pl.pallas_call(
        functools.partial(_attn_body, scale=HEAD_DIM ** -0.5),
        grid=(d, n_steps),
        in_specs=[
            pl.BlockSpec(memory_space=pltpu.SMEM),
            pl.BlockSpec((BLOCK, 2 * BLOCK), lambda r, n: (0, 0), pipeline_mode=pl.Buffered(1)),
            pl.BlockSpec((None, tq, D), lambda r, n: (0, 0, 0)),
            pl.BlockSpec((None, tq, D), following),
            pl.BlockSpec((D, 3 * D), lambda r, n: (0, pattern), pipeline_mode=pl.Buffered(1)),
        ],
        out_specs=[out(D), out(LANES)],
        out_shape=[jax.ShapeDtypeStruct((d, rows, D), F32), jax.ShapeDtypeStruct((d, rows, LANES), F32)],
        scratch_shapes=[pltpu.VMEM((2, n_heads, BLOCK, 2 * BLOCK), F32),
                        pltpu.VMEM((tq, D), BF16),
                        pltpu.VMEM((tq + BLOCK, D), BF16), pltpu.VMEM((tq + BLOCK, D), BF16),
                        pltpu.VMEM((tq, D), BF16), pltpu.VMEM((tq, D), BF16), pltpu.VMEM((tq, D), BF16),
                        pltpu.VMEM((qb, n_heads, BLOCK, 2 * BLOCK), F32),
                        pltpu.VMEM((qb, n_heads, BLOCK, 2 * BLOCK), BF16)],
        compiler_params=_params("arbitrary", "arbitrary"),
        name=f"dilated_attn_d{dilation}",
    )(table, jnp.asarray(_t5_buckets(dilation)), xn, xn, w_qkv)


def _attn_out_body(x_ref, gpost_ref, o0_ref, o1_ref, o2_ref, l0_ref, l1_ref, l2_ref, wout_ref, out_ref, *scr):
    def natural(ref, s):
        d, rows, width = ref.shape
        if d == 1:
            return ref[0]
        n_lt = width // LANES
        for c in range(n_lt):
            for r in range(d):
                s[c, pl.ds(r, rows, stride=d), :] = ref[r, :, c * LANES:(c + 1) * LANES]
        return jnp.concatenate([s[c] for c in range(n_lt)], axis=-1)

    D = x_ref.shape[1]
    hd = D // (D // HEAD_DIM)
    spread = (lax.broadcasted_iota(jnp.int32, (LANES, D), 1) // hd
              == lax.broadcasted_iota(jnp.int32, (LANES, D), 0)).astype(BF16)

    def per_column(w):
        hi = w.astype(BF16)
        lo = (w - hi.astype(F32)).astype(BF16)
        return _dot(hi, spread) + _dot(lo, spread)

    l0, l1, l2 = natural(l0_ref, None), natural(l1_ref, scr[0]), natural(l2_ref, scr[1])
    m = jnp.maximum(jnp.maximum(l0, l1), l2)
    e0, e1, e2 = jnp.exp(l0 - m), jnp.exp(l1 - m), jnp.exp(l2 - m)
    inv = 1.0 / (e0 + e1 + e2)
    o = (per_column(e0 * inv) * natural(o0_ref, None) + per_column(e1 * inv) * natural(o1_ref, scr[2])
         + per_column(e2 * inv) * natural(o2_ref, scr[3]))
    h = _dot(o.astype(BF16), wout_ref[...])
    out_ref[...] = x_ref[...] + _rms(h, gpost_ref[...])


def _attention_mixer(x, gpre, gpost, w_qkv, w_out, rel_bias, tm=512):
    L, D = x.shape
    n_heads = D // HEAD_DIM
    outs, lses = [], []
    w_qkv = w_qkv.astype(BF16)
    xns = _regroup_rows(x, gpre, [d for _, d in PATTERNS])
    for g, (window, dilation) in enumerate(PATTERNS):
        assert window // dilation == BLOCK and L % (BLOCK * dilation) == 0
        o, lse = _dilated_attention(xns[g], w_qkv, g, rel_bias[:, g * n_heads:(g + 1) * n_heads], dilation)
        outs.append(o)
        lses.append(lse)
    tile = pl.BlockSpec((tm, D), lambda i: (i, 0))
    grouped = lambda w: [pl.BlockSpec((d, tm // d, w), lambda i: (0, i, 0)) for _, d in PATTERNS]
    return pl.pallas_call(
        _attn_out_body,
        grid=(L // tm,),
        in_specs=[tile, _resident((1, D))] + grouped(D) + grouped(LANES) + [_resident((D, D))],
        out_specs=tile,
        out_shape=jax.ShapeDtypeStruct((L, D), F32),
        scratch_shapes=[pltpu.VMEM((1, tm, LANES), F32)] * 2 + [pltpu.VMEM((D // LANES, tm, LANES), F32)] * 2,
        compiler_params=_params("parallel"),
        name="attn_out",
    )(x, _row(gpost), *outs, *lses, w_out.astype(BF16))


def kernel(x, norm_pre, norm_post, ffn_w1, ffn_w3, ffn_w2, rel_bias, s5_w_in, s5_a_re, s5_a_im, s5_log_dt, s5_b_re, s5_b_im, s5_c_re, s5_c_im, s5_d, s5_w_glu, s5_b_glu, s5_w_out, cv_w_in, cv_b_in, cv_dw, cv_dw_b, cv_ln_g, cv_ln_b, cv_w_out, cv_b_out, gm_w_in, gm_b_in, gm_ln_g, gm_ln_b, gm_w_s, gm_b_s, gm_w_out, gm_b_out, at_w_qkv, at_w_out):
    bsz, seq, d_model = x.shape
    assert bsz == 1, "sequence mixers carry state along the row axis of one sequence"
    depth = norm_pre.shape[0]
    n_mixers = 4
    h = x.reshape(seq, d_model)
    stacks = (ffn_w1, ffn_w3, ffn_w2)
    weights = tuple(w[0, 0].astype(BF16) for w in stacks)
    for i in range(depth):
        kind, j = i % n_mixers, i // n_mixers
        h, weights = _ffn(h, norm_pre[i, 0], norm_post[i, 0], weights, (stacks, i, 1))
        if kind == 0:
            h = _s5_mixer(h, norm_pre[i, 1], norm_post[i, 1], s5_w_in[j], s5_a_re[j], s5_a_im[j], s5_log_dt[j],
                          s5_b_re[j], s5_b_im[j], s5_c_re[j], s5_c_im[j], s5_d[j], s5_w_glu[j], s5_b_glu[j],
                          s5_w_out[j])
        elif kind == 1:
            h = _conv_mixer(h, norm_pre[i, 1], norm_post[i, 1], cv_w_in[j], cv_b_in[j], cv_dw[j], cv_dw_b[j],
                            cv_ln_g[j], cv_ln_b[j], cv_w_out[j], cv_b_out[j])
        elif kind == 2:
            h = _gmlp_mixer(h, norm_pre[i, 1], norm_post[i, 1], gm_w_in[j], gm_b_in[j], gm_ln_g[j], gm_ln_b[j],
                            gm_w_s[j], gm_b_s[j], gm_w_out[j], gm_b_out[j])
        else:
            h = _attention_mixer(h, norm_pre[i, 1], norm_post[i, 1], at_w_qkv[j], at_w_out[j], rel_bias)
        h, weights = _ffn(h, norm_pre[i, 2], norm_post[i, 2], weights,
                          (stacks, i + 1, 0) if i + 1 < depth else None)
    return h.reshape(bsz, seq, d_model)
```

```python
import functools
import math

import numpy as np
import jax
import jax.numpy as jnp
from jax import lax
from jax.experimental import pallas as pl
from jax.experimental.pallas import tpu as pltpu

F32 = jnp.float32
BF16 = jnp.bfloat16

EPS = 1e-6
LANES = 128
SUBLANES = 8
MXU_WIDTH = 256
S5_GROUP = 16
S5_STATE = 64
S5_GROUPS_PER_TILE = 8
CONV_W = 31
CONV_HALO = 32
GM_CHUNK = 128
GM_HEADS = 8
HEAD_DIM = 64
PATTERNS = ((128, 1), (512, 4), (2048, 16))
BLOCK = 128
NUM_BUCKETS = 32
MAX_DISTANCE = 2048
NEG_BIG = -1e30

VMEM_LIMIT_BYTES = 56 * 1024 * 1024


def _params(*sem):
    return pltpu.CompilerParams(dimension_semantics=sem, vmem_limit_bytes=VMEM_LIMIT_BYTES)


def _rms(x, g):
    return x * lax.rsqrt(jnp.mean(x * x, axis=-1, keepdims=True) + EPS) * g


def _layernorm(x, g, b):
    mu = jnp.mean(x, axis=-1, keepdims=True)
    xc = x - mu
    var = jnp.mean(xc * xc, axis=-1, keepdims=True)
    return xc * lax.rsqrt(var + EPS) * g + b


def _dot(a, b):
    return jnp.dot(a, b, preferred_element_type=F32)


def _row(v):
    return v.reshape(1, -1)


def _ffn_body(x_ref, gpre_ref, gpost_ref, w1_ref, w3_ref, w2_ref, *refs, fc):
    n_cast = (len(refs) - 2) // 2
    src_refs, o_ref, dst_refs, g_scr = refs[:n_cast], refs[n_cast], refs[n_cast + 1:-1], refs[-1]
    for src, dst in zip(src_refs, dst_refs):
        dst[...] = src[...].astype(dst.dtype)
    n_sub, sub, F = g_scr.shape
    for s in range(n_sub):
        rs = slice(s * sub, (s + 1) * sub)
        xn = _rms(x_ref[rs, :], gpre_ref[...]).astype(BF16)
        for c in range(F // fc):
            cs = slice(c * fc, (c + 1) * fc)
            h1 = _dot(xn, w1_ref[:, cs])
            h3 = _dot(xn, w3_ref[:, cs])
            g_scr[s, :, cs] = ((h1 * jax.nn.sigmoid(h1)) * h3).astype(BF16)
        h = _dot(g_scr[s], w2_ref[...])
        o_ref[rs, :] = x_ref[rs, :] + 0.5 * _rms(h, gpost_ref[...])


def _resident(shape):
    return pl.BlockSpec(shape, lambda i: (0,) * len(shape), pipeline_mode=pl.Buffered(1))


def _ffn(x, gpre, gpost, weights, following=None, tm=1024, sub=512, fc=256):
    L, D = x.shape
    w1, w3, w2 = weights
    F = w1.shape[-1]
    n_steps = L // tm
    in_specs = [pl.BlockSpec((tm, D), lambda i: (i, 0)), _resident((1, D)), _resident((1, D)),
                _resident((D, F)), _resident((D, F)), _resident((F, D))]
    out_specs = [pl.BlockSpec((tm, D), lambda i: (i, 0))]
    out_shape = [jax.ShapeDtypeStruct((L, D), F32)]
    operands = [x, _row(gpre), _row(gpost), w1, w3, w2]
    if following is not None:
        stacks, layer, half = following
        for w in stacks:
            rows, cols = w.shape[-2] // n_steps, w.shape[-1]
            in_specs.append(pl.BlockSpec((None, None, rows, cols), lambda i: (layer, half, i, 0)))
            out_specs.append(pl.BlockSpec((rows, cols), lambda i: (i, 0)))
            out_shape.append(jax.ShapeDtypeStruct(w.shape[-2:], BF16))
            operands.append(w)
    outs = pl.pallas_call(
        functools.partial(_ffn_body, fc=fc),
        grid=(n_steps,),
        in_specs=in_specs,
        out_specs=out_specs,
        out_shape=out_shape,
        scratch_shapes=[pltpu.VMEM((tm // sub, sub, F), BF16)],
        compiler_params=_params("arbitrary"),
        name="ffn",
    )(*operands)
    return outs[0], (tuple(outs[1:]) if following is not None else None)


def _s5_body(x_ref, gpre_ref, gpost_ref, win_ref, are_ref, aim_ref, ldt_ref, bdre_ref, bdim_ref,
             cdre_ref, cdim_ref, d_ref, wglu_ref, bglu_ref, wout_ref, o_ref,
             abar_re, abar_im, bb_re, bb_im, st_re, st_im, bu_re, bu_im, u_scr, y_scr):
    tt = x_ref.shape[0]
    n_tiles, uw, sw = bdre_ref.shape
    nj = sw // LANES

    @pl.when(pl.program_id(0) == 0)
    def _():
        ar = are_ref[...]
        ai = aim_ref[...]
        dt = jnp.exp(ldt_ref[...])
        mag = jnp.exp(dt * ar)
        abr = mag * jnp.cos(dt * ai)
        abi = mag * jnp.sin(dt * ai)
        den = ar * ar + ai * ai
        nr = abr - 1.0
        fre = (nr * ar + abi * ai) / den
        fim = (abi * ar - nr * ai) / den
        abar_re[...] = abr
        abar_im[...] = abi
        for k in range(n_tiles):
            fr = fre[k:k + 1, :]
            fi = fim[k:k + 1, :]
            bb_re[k] = (fr * bdre_ref[k] - fi * bdim_ref[k]).astype(BF16)
            bb_im[k] = (fr * bdim_ref[k] + fi * bdre_ref[k]).astype(BF16)
        st_re[...] = jnp.zeros_like(st_re)
        st_im[...] = jnp.zeros_like(st_im)

    n_sub = bu_re.shape[0]
    sub = tt // n_sub
    for s in range(n_sub):
        rs = slice(s * sub, (s + 1) * sub)
        xn = _rms(x_ref[rs, :], gpre_ref[...]).astype(BF16)
        u = _dot(xn, win_ref[...])
        u_scr[rs, :] = u
        ub = u.astype(BF16)
        for k in range(n_tiles):
            uk = ub[:, k * uw:(k + 1) * uw]
            br = _dot(uk, bb_re[k])
            bi = _dot(uk, bb_im[k])
            for j in range(nj):
                bu_re[s, j, pl.ds(k, sub, stride=n_tiles), :] = br[:, j * LANES:(j + 1) * LANES]
                bu_im[s, j, pl.ds(k, sub, stride=n_tiles), :] = bi[:, j * LANES:(j + 1) * LANES]

    a_r = [abar_re[:, j * LANES:(j + 1) * LANES] for j in range(nj)]
    a_i = [abar_im[:, j * LANES:(j + 1) * LANES] for j in range(nj)]
    state = [(st_re[j], st_im[j]) for j in range(nj)]
    for s in range(n_sub):
        for t in range(sub):
            rows = slice(t * n_tiles, (t + 1) * n_tiles)
            for j in range(nj):
                sr, si = state[j]
                nr = a_r[j] * sr - a_i[j] * si + bu_re[s, j, rows, :]
                ni = a_r[j] * si + a_i[j] * sr + bu_im[s, j, rows, :]
                bu_re[s, j, rows, :] = nr
                bu_im[s, j, rows, :] = ni
                state[j] = (nr, ni)
    for j in range(nj):
        st_re[j], st_im[j] = state[j]

    for s in range(n_sub):
        rs = slice(s * sub, (s + 1) * sub)
        for k in range(n_tiles):
            sre = jnp.concatenate([bu_re[s, j, pl.ds(k, sub, stride=n_tiles), :] for j in range(nj)], axis=-1)
            sim = jnp.concatenate([bu_im[s, j, pl.ds(k, sub, stride=n_tiles), :] for j in range(nj)], axis=-1)
            y_scr[rs, k * uw:(k + 1) * uw] = (_dot(sre.astype(BF16), cdre_ref[k])
                                              - _dot(sim.astype(BF16), cdim_ref[k]))
        y = jax.nn.gelu(y_scr[rs, :] + d_ref[...] * u_scr[rs, :])
        z = y * jax.nn.sigmoid(_dot(y.astype(BF16), wglu_ref[...]) + bglu_ref[...])
        h = _dot(z.astype(BF16), wout_ref[...])
        o_ref[rs, :] = x_ref[rs, :] + _rms(h, gpost_ref[...])


def _s5_block_diag(b, c):
    G, P, HG = b.shape
    gt = S5_GROUPS_PER_TILE
    eye = jnp.eye(gt, dtype=b.dtype)
    bt = b.reshape(G // gt, gt, P, HG).transpose(0, 1, 3, 2)
    bd = jnp.einsum('kghp,gj->kghjp', bt, eye).reshape(G // gt, gt * HG, gt * P)
    ct = c.reshape(G // gt, gt, HG, P)
    cd = jnp.einsum('kghp,gj->kjpgh', ct, eye).reshape(G // gt, gt * P, gt * HG)
    return bd, cd


def _s5_mixer(x, gpre, gpost, w_in, a_re, a_im, log_dt, b_re, b_im, c_re, c_im, d_skip, w_glu, b_glu,
              w_out, tt=512, sub=256):
    L, D = x.shape
    G, P = a_re.shape
    bd_re, cd_re = _s5_block_diag(b_re, c_re)
    bd_im, cd_im = _s5_block_diag(b_im, c_im)
    n_tiles, uw, sw = bd_re.shape
    assert n_tiles == SUBLANES and uw == LANES and sw % LANES == 0
    nj = sw // LANES
    per_tile = lambda v: v.reshape(n_tiles, sw)
    return pl.pallas_call(
        _s5_body,
        grid=(L // tt,),
        in_specs=[
            pl.BlockSpec((tt, D), lambda i: (i, 0)),
            _resident((1, D)), _resident((1, D)), _resident((D, D)),
            _resident((n_tiles, sw)), _resident((n_tiles, sw)), _resident((n_tiles, sw)),
            _resident((n_tiles, uw, sw)), _resident((n_tiles, uw, sw)),
            _resident((n_tiles, sw, uw)), _resident((n_tiles, sw, uw)),
            _resident((1, D)), _resident((D, D)), _resident((1, D)), _resident((D, D)),
        ],
        out_specs=pl.BlockSpec((tt, D), lambda i: (i, 0)),
        out_shape=jax.ShapeDtypeStruct((L, D), F32),
        scratch_shapes=[
            pltpu.VMEM((n_tiles, sw), F32), pltpu.VMEM((n_tiles, sw), F32),
            pltpu.VMEM((n_tiles, uw, sw), BF16), pltpu.VMEM((n_tiles, uw, sw), BF16),
            pltpu.VMEM((nj, n_tiles, LANES), F32), pltpu.VMEM((nj, n_tiles, LANES), F32),
            pltpu.VMEM((tt // sub, nj, sub * n_tiles, LANES), F32),
            pltpu.VMEM((tt // sub, nj, sub * n_tiles, LANES), F32),
            pltpu.VMEM((tt, D), F32), pltpu.VMEM((tt, D), F32),
        ],
        compiler_params=_params("arbitrary"),
        name="s5_mixer",
    )(x, _row(gpre), _row(gpost), w_in.astype(BF16),
      per_tile(a_re), per_tile(a_im), per_tile(jnp.broadcast_to(log_dt[:, None], (G, P))),
      bd_re, bd_im, cd_re.astype(BF16), cd_im.astype(BF16),
      _row(d_skip), w_glu.astype(BF16), _row(b_glu), w_out.astype(BF16))


def _conv_body(x_ref, gpre_ref, gpost_ref, win_ref, bin_ref, dw_ref, dwb_ref,
               lng_ref, lnb_ref, wout_ref, bout_ref, o_ref, ext, conv_scr):
    tm, D = x_ref.shape
    halo = ext.shape[0] - tm
    n_taps = dw_ref.shape[0]

    @pl.when(pl.program_id(0) == 0)
    def _():
        ext[0:halo, :] = jnp.zeros((halo, D), F32)

    x = x_ref[...]
    xn = _rms(x, gpre_ref[...]).astype(BF16)
    a = _dot(xn, win_ref[:, :D]) + bin_ref[:, :D]
    g = _dot(xn, win_ref[:, D:]) + bin_ref[:, D:]
    ext[halo:halo + tm, :] = a * jax.nn.sigmoid(g)

    base = halo - (n_taps - 1)
    for c in range(D // LANES):
        cs = slice(c * LANES, (c + 1) * LANES)
        acc = None
        for s in range(SUBLANES):
            rows = tm if s == 0 else tm + SUBLANES
            q = None
            for a in range((base + n_taps - 1) // SUBLANES + 1):
                k = SUBLANES * a + s - base
                if 0 <= k < n_taps:
                    term = dw_ref[k:k + 1, cs] * ext[SUBLANES * a:SUBLANES * a + rows, cs]
                    q = term if q is None else q + term
            part = q if s == 0 else q[s:s + tm]
            acc = part if acc is None else acc + part
        conv_scr[:, cs] = acc
    ext[0:halo, :] = ext[tm:tm + halo, :]

    y = _layernorm(conv_scr[...] + dwb_ref[...], lng_ref[...], lnb_ref[...])
    y = y * jax.nn.sigmoid(y)
    h = _dot(y.astype(BF16), wout_ref[...]) + bout_ref[...]
    o_ref[...] = x + _rms(h, gpost_ref[...])


def _conv_mixer(x, gpre, gpost, w_in, b_in, dw, dw_b, ln_g, ln_b, w_out, b_out, tm=256):
    L, D = x.shape
    return pl.pallas_call(
        _conv_body,
        grid=(L // tm,),
        in_specs=[
            pl.BlockSpec((tm, D), lambda i: (i, 0)),
            _resident((1, D)), _resident((1, D)), _resident((D, 2 * D)), _resident((1, 2 * D)),
            _resident(dw.shape), _resident((1, D)), _resident((1, D)), _resident((1, D)),
            _resident((D, D)), _resident((1, D)),
        ],
        out_specs=pl.BlockSpec((tm, D), lambda i: (i, 0)),
        out_shape=jax.ShapeDtypeStruct((L, D), F32),
        scratch_shapes=[pltpu.VMEM((tm + CONV_HALO, D), F32), pltpu.VMEM((tm, D), F32)],
        compiler_params=_params("arbitrary"),
        name="conv_mixer",
    )(x, _row(gpre), _row(gpost), w_in.astype(BF16), _row(b_in), dw, _row(dw_b), _row(ln_g), _row(ln_b),
      w_out.astype(BF16), _row(b_out))


def _gmlp_body(x_ref, gpre_ref, gpost_ref, win_ref, bin_ref, lng_ref, lnb_ref, ws_ref, bs_ref,
               wout_ref, bout_ref, o_ref, z_scr, su_scr, *, fc):
    n_sub, sub, _ = su_scr.shape
    E = lng_ref.shape[1]
    n_heads, chunk, _ = ws_ref.shape
    hw = E // n_heads
    causal = (lax.broadcasted_iota(jnp.int32, (chunk, chunk), 1)
              <= lax.broadcasted_iota(jnp.int32, (chunk, chunk), 0))
    wm = [jnp.where(causal, ws_ref[h], 0.0).astype(BF16) for h in range(n_heads)]
    for i in range(n_sub):
        rows = slice(i * sub, (i + 1) * sub)
        x = x_ref[rows, :]
        xn = _rms(x, gpre_ref[...]).astype(BF16)
        for c in range(2 * E // fc):
            cs = slice(c * fc, (c + 1) * fc)
            z_scr[i, :, cs] = jax.nn.gelu(_dot(xn, win_ref[:, cs]) + bin_ref[:, cs])
        vb = _layernorm(z_scr[i, :, E:], lng_ref[...], lnb_ref[...]).astype(BF16)
        for h in range(n_heads):
            bias = bs_ref[:, h:h + 1]
            hs = slice(h * hw, (h + 1) * hw)
            for c in range(sub // chunk):
                rs = slice(c * chunk, (c + 1) * chunk)
                s = _dot(wm[h], vb[rs, hs]) + bias
                su_scr[i, rs, hs] = (z_scr[i, rs, hs] * s).astype(BF16)
        hout = _dot(su_scr[i], wout_ref[...]) + bout_ref[...]
        o_ref[rows, :] = x + _rms(hout, gpost_ref[...])


def _gmlp_mixer(x, gpre, gpost, w_in, b_in, ln_g, ln_b, w_s, b_s, w_out, b_out, tm=512, sub=512, fc=512):
    L, D = x.shape
    E = ln_g.shape[0]
    return pl.pallas_call(
        functools.partial(_gmlp_body, fc=fc),
        grid=(L // tm,),
        in_specs=[
            pl.BlockSpec((tm, D), lambda i: (i, 0)),
            _resident((1, D)), _resident((1, D)), _resident((D, 2 * E)), _resident((1, 2 * E)),
            _resident((1, E)), _resident((1, E)),
            _resident(w_s.shape), _resident((GM_CHUNK, GM_HEADS)),
            _resident((E, D)), _resident((1, D)),
        ],
        out_specs=pl.BlockSpec((tm, D), lambda i: (i, 0)),
        out_shape=jax.ShapeDtypeStruct((L, D), F32),
        scratch_shapes=[pltpu.VMEM((tm // sub, sub, 2 * E), F32), pltpu.VMEM((tm // sub, sub, E), BF16)],
        compiler_params=_params("parallel"),
        name="gmlp_mixer",
    )(x, _row(gpre), _row(gpost), w_in.astype(BF16), _row(b_in), _row(ln_g), _row(ln_b),
      w_s, b_s.T, w_out.astype(BF16), _row(b_out))


def _regroup_body(x_ref, g_ref, *refs):
    *o_refs, xt_scr = refs
    xn = _rms(x_ref[...], g_ref[...])
    for c in range(xt_scr.shape[0]):
        xt_scr[c] = xn[:, c * LANES:(c + 1) * LANES]
    for o_ref in o_refs:
        d, rows, _ = o_ref.shape
        if d == 1:
            o_ref[0] = xn.astype(o_ref.dtype)
            continue
        for r in range(d):
            o_ref[r] = jnp.concatenate([xt_scr[c, pl.ds(r, rows, stride=d), :] for c in range(xt_scr.shape[0])],
                                       axis=-1).astype(o_ref.dtype)


def _regroup_rows(x, g, dilations, tm=512):
    L, D = x.shape
    return pl.pallas_call(
        _regroup_body,
        grid=(L // tm,),
        in_specs=[pl.BlockSpec((tm, D), lambda i: (i, 0)), _resident((1, D))],
        out_specs=[pl.BlockSpec((d, tm // d, D), lambda i: (0, i, 0)) for d in dilations],
        out_shape=[jax.ShapeDtypeStruct((d, L // d, D), BF16) for d in dilations],
        scratch_shapes=[pltpu.VMEM((D // LANES, tm, LANES), F32)],
        compiler_params=_params("parallel"),
        name="regroup_rows",
    )(x, _row(g))


def _t5_buckets(dilation):
    delta = (np.arange(BLOCK)[:, None] + BLOCK) - np.arange(2 * BLOCK)[None, :]
    dist = np.maximum(delta, 0) * dilation
    max_exact = NUM_BUCKETS // 2
    distf = np.maximum(dist, 1).astype(np.float32)
    large = max_exact + (np.log(distf / np.float32(max_exact)) / np.float32(math.log(MAX_DISTANCE / max_exact))
                         * np.float32(NUM_BUCKETS - max_exact)).astype(np.int32)
    large = np.minimum(large, NUM_BUCKETS - 1)
    return np.where(dist < max_exact, dist, large).astype(np.int32)


def _attn_body(tab_ref, bkt_ref, xc_ref, xn_ref, w_ref, o_ref, lse_ref,
               bias_scr, q_cur, k_win, v_win, q_nxt, k_nxt, v_nxt, s_scr, p_scr, *, scale):
    blk = bias_scr.shape[2]
    n_heads = bias_scr.shape[1]
    D = q_cur.shape[1]
    hd = D // n_heads
    qb = q_cur.shape[0] // blk
    n = pl.program_id(1)

    def project(x_ref, q_dst, k_dst, v_dst, k_off):
        xb = x_ref[...]
        q_dst[...] = (_dot(xb, w_ref[:, 0:D]) * scale).astype(BF16)
        k_dst[k_off:k_off + qb * blk, :] = _dot(xb, w_ref[:, D:2 * D]).astype(BF16)
        v_dst[k_off:k_off + qb * blk, :] = _dot(xb, w_ref[:, 2 * D:3 * D]).astype(BF16)

    @pl.when(n == 0)
    def _():
        k_win[0:blk, :] = jnp.zeros((blk, D), BF16)
        v_win[0:blk, :] = jnp.zeros((blk, D), BF16)

    @pl.when((pl.program_id(0) == 0) & (n == 0))
    def _():
        project(xc_ref, q_cur, k_win, v_win, blk)

    @pl.when((pl.program_id(0) == 0) & (n == 0))
    def _():
        bkt = bkt_ref[...]
        qi = lax.broadcasted_iota(jnp.int32, bkt.shape, 0)
        ki = lax.broadcasted_iota(jnp.int32, bkt.shape, 1)
        delta = qi + blk - ki
        in_band = (delta >= 0) & (delta <= blk)
        for h in range(n_heads):
            acc = jnp.zeros(bkt.shape, F32)
            for b in range(tab_ref.shape[0]):
                acc = jnp.where(bkt == b, tab_ref[b, h], acc)
            acc = jnp.where(in_band, acc, NEG_BIG)
            bias_scr[1, h] = acc
            bias_scr[0, h] = jnp.where(ki >= blk, acc, NEG_BIG)

    dims = (((1,), (1,)), ((), ()))
    per_tile = LANES // hd
    lane = lax.broadcasted_iota(jnp.int32, (1, LANES), 1)
    own = [(lane >= i * hd) & (lane < (i + 1) * hd) for i in range(per_tile)]
    n_tiles = n_heads // per_tile

    pending = [(dst, which * D + c * MXU_WIDTH, c) for which, dst in enumerate((q_nxt, k_nxt, v_nxt))
               for c in range(D // MXU_WIDTH)]
    n_chunks = len(pending)
    n_slots = qb * n_heads

    def project_chunks(slot_idx):
        due = n_chunks * (slot_idx + 1) // n_slots
        while n_chunks - len(pending) < due:
            dst, col, c = pending.pop(0)
            res = _dot(xn_ref[...], w_ref[:, col:col + MXU_WIDTH])
            if dst is q_nxt:
                res = res * scale
            dst[:, c * MXU_WIDTH:(c + 1) * MXU_WIDTH] = res.astype(BF16)

    for b in range(qb):
        rows = slice(b * blk, (b + 1) * blk)
        keys = slice(b * blk, (b + 2) * blk)
        slot = jnp.minimum(n, 1) if b == 0 else 1
        for t in range(n_tiles):
            cs = slice(t * LANES, (t + 1) * LANES)
            qt = q_cur[rows, cs]
            kt = k_win[keys, cs]
            for i in range(per_tile):
                h = t * per_tile + i
                qh = jnp.where(own[i], qt, jnp.zeros_like(qt))
                s_scr[b, h] = lax.dot_general(qh, kt, dims, preferred_element_type=F32) + bias_scr[slot, h]

        lse_b = jnp.zeros((blk, LANES), F32)
        for h in range(n_heads):
            s = s_scr[b, h]
            m = jnp.max(s, axis=-1, keepdims=True)
            e = jnp.exp(s - m)
            den = jnp.sum(e, axis=-1, keepdims=True)
            p_scr[b, h] = (e * (1.0 / den)).astype(BF16)
            lse_b = jnp.where(lane == h, m + jnp.log(den), lse_b)
            project_chunks(b * n_heads + h)
        lse_ref[rows, :] = lse_b

        for t in range(n_tiles):
            cs = slice(t * LANES, (t + 1) * LANES)
            vt = v_win[keys, cs]
            acc = jnp.zeros((blk, LANES), F32)
            for i in range(per_tile):
                acc = acc + _dot(p_scr[b, t * per_tile + i], jnp.where(own[i], vt, jnp.zeros_like(vt)))
            o_ref[rows, cs] = acc

    k_win[0:blk, :] = k_win[qb * blk:(qb + 1) * blk, :]
    v_win[0:blk, :] = v_win[qb * blk:(qb + 1) * blk, :]
    k_win[blk:(qb + 1) * blk, :] = k_nxt[...]
    v_win[blk:(qb + 1) * blk, :] = v_nxt[...]
    q_cur[...] = q_nxt[...]


def _dilated_attention(xn, w_qkv, pattern, table, dilation, qb=2):
    d, rows, D = xn.shape
    n_heads = D // HEAD_DIM
    tq = qb * BLOCK
    n_steps = rows // tq
    out = lambda w: pl.BlockSpec((None, tq, w), lambda r, n: (r, n, 0))

    def following(r, n):
        nxt = jnp.minimum(r * n_steps + n + 1, d * n_steps - 1)
        return nxt // n_steps, nxt % n_steps, 0

    return pl.pallas_call(
        functools.partial(_attn_body, scale=HEAD_DIM ** -0.5),
        grid=(d, n_steps),
        in_specs=[
            pl.BlockSpec(memory_space=pltpu.SMEM),
            pl.BlockSpec((BLOCK, 2 * BLOCK), lambda r, n: (0, 0), pipeline_mode=pl.Buffered(1)),
            pl.BlockSpec((None, tq, D), lambda r, n: (0, 0, 0)),
            pl.BlockSpec((None, tq, D), following),
            pl.BlockSpec((D, 3 * D), lambda r, n: (0, pattern), pipeline_mode=pl.Buffered(1)),
        ],
        out_specs=[out(D), out(LANES)],
        out_shape=[jax.ShapeDtypeStruct((d, rows, D), F32), jax.ShapeDtypeStruct((d, rows, LANES), F32)],
        scratch_shapes=[pltpu.VMEM((2, n_heads, BLOCK, 2 * BLOCK), F32),
                        pltpu.VMEM((tq, D), BF16),
                        pltpu.VMEM((tq + BLOCK, D), BF16), pltpu.VMEM((tq + BLOCK, D), BF16),
                        pltpu.VMEM((tq, D), BF16), pltpu.VMEM((tq, D), BF16), pltpu.VMEM((tq, D), BF16),
                        pltpu.VMEM((qb, n_heads, BLOCK, 2 * BLOCK), F32),
                        pltpu.VMEM((qb, n_heads, BLOCK, 2 * BLOCK), BF16)],
        compiler_params=_params("arbitrary", "arbitrary"),
        name=f"dilated_attn_d{dilation}",
    )(table, jnp.asarray(_t5_buckets(dilation)), xn, xn, w_qkv)


def _attn_out_body(x_ref, gpost_ref, o0_ref, o1_ref, o2_ref, l0_ref, l1_ref, l2_ref, wout_ref, out_ref, *scr):
    def natural(ref, s):
        d, rows, width = ref.shape
        if d == 1:
            return ref[0]
        n_lt = width // LANES
        for c in range(n_lt):
            for r in range(d):
                s[c, pl.ds(r, rows, stride=d), :] = ref[r, :, c * LANES:(c + 1) * LANES]
        return jnp.concatenate([s[c] for c in range(n_lt)], axis=-1)

    D = x_ref.shape[1]
    hd = D // (D // HEAD_DIM)
    spread = (lax.broadcasted_iota(jnp.int32, (LANES, D), 1) // hd
              == lax.broadcasted_iota(jnp.int32, (LANES, D), 0)).astype(BF16)

    def per_column(w):
        hi = w.astype(BF16)
        lo = (w - hi.astype(F32)).astype(BF16)
        return _dot(hi, spread) + _dot(lo, spread)

    l0, l1, l2 = natural(l0_ref, None), natural(l1_ref, scr[0]), natural(l2_ref, scr[1])
    m = jnp.maximum(jnp.maximum(l0, l1), l2)
    e0, e1, e2 = jnp.exp(l0 - m), jnp.exp(l1 - m), jnp.exp(l2 - m)
    inv = 1.0 / (e0 + e1 + e2)
    o = (per_column(e0 * inv) * natural(o0_ref, None) + per_column(e1 * inv) * natural(o1_ref, scr[2])
         + per_column(e2 * inv) * natural(o2_ref, scr[3]))
    h = _dot(o.astype(BF16), wout_ref[...])
    out_ref[...] = x_ref[...] + _rms(h, gpost_ref[...])


def _attention_mixer(x, gpre, gpost, w_qkv, w_out, rel_bias, tm=512):
    L, D = x.shape
    n_heads = D // HEAD_DIM
    outs, lses = [], []
    w_qkv = w_qkv.astype(BF16)
    xns = _regroup_rows(x, gpre, [d for _, d in PATTERNS])
    for g, (window, dilation) in enumerate(PATTERNS):
        assert window // dilation == BLOCK and L % (BLOCK * dilation) == 0
        o, lse = _dilated_attention(xns[g], w_qkv, g, rel_bias[:, g * n_heads:(g + 1) * n_heads], dilation)
        outs.append(o)
        lses.append(lse)
    tile = pl.BlockSpec((tm, D), lambda i: (i, 0))
    grouped = lambda w: [pl.BlockSpec((d, tm // d, w), lambda i: (0, i, 0)) for _, d in PATTERNS]
    return pl.pallas_call(
        _attn_out_body,
        grid=(L // tm,),
        in_specs=[tile, _resident((1, D))] + grouped(D) + grouped(LANES) + [_resident((D, D))],
        out_specs=tile,
        out_shape=jax.ShapeDtypeStruct((L, D), F32),
        scratch_shapes=[pltpu.VMEM((1, tm, LANES), F32)] * 2 + [pltpu.VMEM((D // LANES, tm, LANES), F32)] * 2,
        compiler_params=_params("parallel"),
        name="attn_out",
    )(x, _row(gpost), *outs, *lses, w_out.astype(BF16))


def kernel(x, norm_pre, norm_post, ffn_w1, ffn_w3, ffn_w2, rel_bias, s5_w_in, s5_a_re, s5_a_im, s5_log_dt, s5_b_re, s5_b_im, s5_c_re, s5_c_im, s5_d, s5_w_glu, s5_b_glu, s5_w_out, cv_w_in, cv_b_in, cv_dw, cv_dw_b, cv_ln_g, cv_ln_b, cv_w_out, cv_b_out, gm_w_in, gm_b_in, gm_ln_g, gm_ln_b, gm_w_s, gm_b_s, gm_w_out, gm_b_out, at_w_qkv, at_w_out):
    bsz, seq, d_model = x.shape
    assert bsz == 1, "sequence mixers carry state along the row axis of one sequence"
    depth = norm_pre.shape[0]
    n_mixers = 4
    h = x.reshape(seq, d_model)
    stacks = (ffn_w1, ffn_w3, ffn_w2)
    weights = tuple(w[0, 0].astype(BF16) for w in stacks)
    for i in range(depth):
        kind, j = i % n_mixers, i // n_mixers
        h, weights = _ffn(h, norm_pre[i, 0], norm_post[i, 0], weights, (stacks, i, 1))
        if kind == 0:
            h = _s5_mixer(h, norm_pre[i, 1], norm_post[i, 1], s5_w_in[j], s5_a_re[j], s5_a_im[j], s5_log_dt[j],
                          s5_b_re[j], s5_b_im[j], s5_c_re[j], s5_c_im[j], s5_d[j], s5_w_glu[j], s5_b_glu[j],
                          s5_w_out[j])
        elif kind == 1:
            h = _conv_mixer(h, norm_pre[i, 1], norm_post[i, 1], cv_w_in[j], cv_b_in[j], cv_dw[j], cv_dw_b[j],
                            cv_ln_g[j], cv_ln_b[j], cv_w_out[j], cv_b_out[j])
        elif kind == 2:
            h = _gmlp_mixer(h, norm_pre[i, 1], norm_post[i, 1], gm_w_in[j], gm_b_in[j], gm_ln_g[j], gm_ln_b[j],
                            gm_w_s[j], gm_b_s[j], gm_w_out[j], gm_b_out[j])
        else:
            h = _attention_mixer(h, norm_pre[i, 1], norm_post[i, 1], at_w_qkv[j], at_w_out[j], rel_bias)
        h, weights = _ffn(h, norm_pre[i, 2], norm_post[i, 2], weights,
                          (stacks, i + 1, 0) if i + 1 < depth else None)
    return h.reshape(bsz, seq, d_model)
```

```python
import functools
import math

import numpy as np
import jax
import jax.numpy as jnp
from jax import lax
from jax.experimental import pallas as pl
from jax.experimental.pallas import tpu as pltpu

F32 = jnp.float32
BF16 = jnp.bfloat16

EPS = 1e-6
LANES = 128
SUBLANES = 8
MXU_WIDTH = 256
S5_GROUP = 16
S5_STATE = 64
S5_GROUPS_PER_TILE = 8
CONV_W = 31
CONV_HALO = 32
GM_CHUNK = 128
GM_HEADS = 8
HEAD_DIM = 64
PATTERNS = ((128, 1), (512, 4), (2048, 16))
BLOCK = 128
NUM_BUCKETS = 32
MAX_DISTANCE = 2048
NEG_BIG = -1e30

VMEM_LIMIT_BYTES = 56 * 1024 * 1024


def _params(*sem):
    return pltpu.CompilerParams(dimension_semantics=sem, vmem_limit_bytes=VMEM_LIMIT_BYTES)


def _rms(x, g):
    return x * lax.rsqrt(jnp.mean(x * x, axis=-1, keepdims=True) + EPS) * g


def _layernorm(x, g, b):
    mu = jnp.mean(x, axis=-1, keepdims=True)
    xc = x - mu
    var = jnp.mean(xc * xc, axis=-1, keepdims=True)
    return xc * lax.rsqrt(var + EPS) * g + b


def _dot(a, b):
    return jnp.dot(a, b, preferred_element_type=F32)


def _row(v):
    return v.reshape(1, -1)


def _ffn_body(x_ref, gpre_ref, gpost_ref, w1_ref, w3_ref, w2_ref, *refs, fc):
    n_cast = (len(refs) - 2) // 2
    src_refs, o_ref, dst_refs, g_scr = refs[:n_cast], refs[n_cast], refs[n_cast + 1:-1], refs[-1]
    for src, dst in zip(src_refs, dst_refs):
        dst[...] = src[...].astype(dst.dtype)
    n_sub, sub, F = g_scr.shape
    for s in range(n_sub):
        rs = slice(s * sub, (s + 1) * sub)
        xn = _rms(x_ref[rs, :], gpre_ref[...]).astype(BF16)
        for c in range(F // fc):
            cs = slice(c * fc, (c + 1) * fc)
            h1 = _dot(xn, w1_ref[:, cs])
            h3 = _dot(xn, w3_ref[:, cs])
            g_scr[s, :, cs] = ((h1 * jax.nn.sigmoid(h1)) * h3).astype(BF16)
        h = _dot(g_scr[s], w2_ref[...])
        o_ref[rs, :] = x_ref[rs, :] + 0.5 * _rms(h, gpost_ref[...])


def _resident(shape):
    return pl.BlockSpec(shape, lambda i: (0,) * len(shape), pipeline_mode=pl.Buffered(1))


def _ffn(x, gpre, gpost, weights, casts=(), tm=1024, sub=512, fc=256):
    L, D = x.shape
    w1, w3, w2 = weights
    F = w1.shape[-1]
    n_steps = L // tm
    in_specs = [pl.BlockSpec((tm, D), lambda i: (i, 0)), _resident((1, D)), _resident((1, D)),
                _resident((D, F)), _resident((D, F)), _resident((F, D))]
    out_specs = [pl.BlockSpec((tm, D), lambda i: (i, 0))]
    out_shape = [jax.ShapeDtypeStruct((L, D), F32)]
    operands = [x, _row(gpre), _row(gpost), w1, w3, w2]
    for w, lead in casts:
        rows, cols = w.shape[-2] // n_steps, w.shape[-1]
        in_specs.append(pl.BlockSpec((None,) * len(lead) + (rows, cols), lambda i, lead=lead: lead + (i, 0)))
        out_specs.append(pl.BlockSpec((rows, cols), lambda i: (i, 0)))
        out_shape.append(jax.ShapeDtypeStruct(w.shape[-2:], BF16))
        operands.append(w)
    outs = pl.pallas_call(
        functools.partial(_ffn_body, fc=fc),
        grid=(n_steps,),
        in_specs=in_specs,
        out_specs=out_specs,
        out_shape=out_shape,
        scratch_shapes=[pltpu.VMEM((tm // sub, sub, F), BF16)],
        compiler_params=_params("arbitrary"),
        name="ffn",
    )(*operands)
    return outs[0], list(outs[1:])


def _s5_body(x_ref, gpre_ref, gpost_ref, win_ref, are_ref, aim_ref, ldt_ref, bdre_ref, bdim_ref,
             cdre_ref, cdim_ref, d_ref, wglu_ref, bglu_ref, wout_ref, o_ref,
             abar_re, abar_im, bb_re, bb_im, st_re, st_im, bu_re, bu_im, u_scr, y_scr):
    tt = x_ref.shape[0]
    n_tiles, uw, sw = bdre_ref.shape
    nj = sw // LANES

    @pl.when(pl.program_id(0) == 0)
    def _():
        ar = are_ref[...]
        ai = aim_ref[...]
        dt = jnp.exp(ldt_ref[...])
        mag = jnp.exp(dt * ar)
        abr = mag * jnp.cos(dt * ai)
        abi = mag * jnp.sin(dt * ai)
        den = ar * ar + ai * ai
        nr = abr - 1.0
        fre = (nr * ar + abi * ai) / den
        fim = (abi * ar - nr * ai) / den
        abar_re[...] = abr
        abar_im[...] = abi
        for k in range(n_tiles):
            fr = fre[k:k + 1, :]
            fi = fim[k:k + 1, :]
            bb_re[k] = (fr * bdre_ref[k] - fi * bdim_ref[k]).astype(BF16)
            bb_im[k] = (fr * bdim_ref[k] + fi * bdre_ref[k]).astype(BF16)
        st_re[...] = jnp.zeros_like(st_re)
        st_im[...] = jnp.zeros_like(st_im)

    n_sub = bu_re.shape[0]
    sub = tt // n_sub
    for s in range(n_sub):
        rs = slice(s * sub, (s + 1) * sub)
        xn = _rms(x_ref[rs, :], gpre_ref[...]).astype(BF16)
        u = _dot(xn, win_ref[...])
        u_scr[rs, :] = u
        ub = u.astype(BF16)
        for k in range(n_tiles):
            uk = ub[:, k * uw:(k + 1) * uw]
            br = _dot(uk, bb_re[k])
            bi = _dot(uk, bb_im[k])
            for j in range(nj):
                bu_re[s, j, pl.ds(k, sub, stride=n_tiles), :] = br[:, j * LANES:(j + 1) * LANES]
                bu_im[s, j, pl.ds(k, sub, stride=n_tiles), :] = bi[:, j * LANES:(j + 1) * LANES]

    a_r = [abar_re[:, j * LANES:(j + 1) * LANES] for j in range(nj)]
    a_i = [abar_im[:, j * LANES:(j + 1) * LANES] for j in range(nj)]
    state = [(st_re[j], st_im[j]) for j in range(nj)]
    for s in range(n_sub):
        for t in range(sub):
            rows = slice(t * n_tiles, (t + 1) * n_tiles)
            for j in range(nj):
                sr, si = state[j]
                nr = a_r[j] * sr - a_i[j] * si + bu_re[s, j, rows, :]
                ni = a_r[j] * si + a_i[j] * sr + bu_im[s, j, rows, :]
                bu_re[s, j, rows, :] = nr
                bu_im[s, j, rows, :] = ni
                state[j] = (nr, ni)
    for j in range(nj):
        st_re[j], st_im[j] = state[j]

    for s in range(n_sub):
        rs = slice(s * sub, (s + 1) * sub)
        for k in range(n_tiles):
            sre = jnp.concatenate([bu_re[s, j, pl.ds(k, sub, stride=n_tiles), :] for j in range(nj)], axis=-1)
            sim = jnp.concatenate([bu_im[s, j, pl.ds(k, sub, stride=n_tiles), :] for j in range(nj)], axis=-1)
            y_scr[rs, k * uw:(k + 1) * uw] = (_dot(sre.astype(BF16), cdre_ref[k])
                                              - _dot(sim.astype(BF16), cdim_ref[k]))
        y = jax.nn.gelu(y_scr[rs, :] + d_ref[...] * u_scr[rs, :])
        z = y * jax.nn.sigmoid(_dot(y.astype(BF16), wglu_ref[...]) + bglu_ref[...])
        h = _dot(z.astype(BF16), wout_ref[...])
        o_ref[rs, :] = x_ref[rs, :] + _rms(h, gpost_ref[...])


def _s5_block_diag(b, c):
    G, P, HG = b.shape
    gt = S5_GROUPS_PER_TILE
    eye = jnp.eye(gt, dtype=b.dtype)
    bt = b.reshape(G // gt, gt, P, HG).transpose(0, 1, 3, 2)
    bd = jnp.einsum('kghp,gj->kghjp', bt, eye).reshape(G // gt, gt * HG, gt * P)
    ct = c.reshape(G // gt, gt, HG, P)
    cd = jnp.einsum('kghp,gj->kjpgh', ct, eye).reshape(G // gt, gt * P, gt * HG)
    return bd, cd


def _s5_mixer(x, gpre, gpost, w_in, a_re, a_im, log_dt, b_re, b_im, c_re, c_im, d_skip, w_glu, b_glu,
              w_out, tt=512, sub=256):
    L, D = x.shape
    G, P = a_re.shape
    bd_re, cd_re = _s5_block_diag(b_re, c_re)
    bd_im, cd_im = _s5_block_diag(b_im, c_im)
    n_tiles, uw, sw = bd_re.shape
    assert n_tiles == SUBLANES and uw == LANES and sw % LANES == 0
    nj = sw // LANES
    per_tile = lambda v: v.reshape(n_tiles, sw)
    return pl.pallas_call(
        _s5_body,
        grid=(L // tt,),
        in_specs=[
            pl.BlockSpec((tt, D), lambda i: (i, 0)),
            _resident((1, D)), _resident((1, D)), _resident((D, D)),
            _resident((n_tiles, sw)), _resident((n_tiles, sw)), _resident((n_tiles, sw)),
            _resident((n_tiles, uw, sw)), _resident((n_tiles, uw, sw)),
            _resident((n_tiles, sw, uw)), _resident((n_tiles, sw, uw)),
            _resident((1, D)), _resident((D, D)), _resident((1, D)), _resident((D, D)),
        ],
        out_specs=pl.BlockSpec((tt, D), lambda i: (i, 0)),
        out_shape=jax.ShapeDtypeStruct((L, D), F32),
        scratch_shapes=[
            pltpu.VMEM((n_tiles, sw), F32), pltpu.VMEM((n_tiles, sw), F32),
            pltpu.VMEM((n_tiles, uw, sw), BF16), pltpu.VMEM((n_tiles, uw, sw), BF16),
            pltpu.VMEM((nj, n_tiles, LANES), F32), pltpu.VMEM((nj, n_tiles, LANES), F32),
            pltpu.VMEM((tt // sub, nj, sub * n_tiles, LANES), F32),
            pltpu.VMEM((tt // sub, nj, sub * n_tiles, LANES), F32),
            pltpu.VMEM((tt, D), F32), pltpu.VMEM((tt, D), F32),
        ],
        compiler_params=_params("arbitrary"),
        name="s5_mixer",
    )(x, _row(gpre), _row(gpost), w_in.astype(BF16),
      per_tile(a_re), per_tile(a_im), per_tile(jnp.broadcast_to(log_dt[:, None], (G, P))),
      bd_re, bd_im, cd_re.astype(BF16), cd_im.astype(BF16),
      _row(d_skip), w_glu.astype(BF16), _row(b_glu), w_out.astype(BF16))


def _conv_body(x_ref, gpre_ref, gpost_ref, win_ref, bin_ref, dw_ref, dwb_ref,
               lng_ref, lnb_ref, wout_ref, bout_ref, o_ref, ext, conv_scr):
    tm, D = x_ref.shape
    halo = ext.shape[0] - tm
    n_taps = dw_ref.shape[0]

    @pl.when(pl.program_id(0) == 0)
    def _():
        ext[0:halo, :] = jnp.zeros((halo, D), F32)

    x = x_ref[...]
    xn = _rms(x, gpre_ref[...]).astype(BF16)
    a = _dot(xn, win_ref[:, :D]) + bin_ref[:, :D]
    g = _dot(xn, win_ref[:, D:]) + bin_ref[:, D:]
    ext[halo:halo + tm, :] = a * jax.nn.sigmoid(g)

    base = halo - (n_taps - 1)
    for c in range(D // LANES):
        cs = slice(c * LANES, (c + 1) * LANES)
        acc = None
        for s in range(SUBLANES):
            rows = tm if s == 0 else tm + SUBLANES
            q = None
            for a in range((base + n_taps - 1) // SUBLANES + 1):
                k = SUBLANES * a + s - base
                if 0 <= k < n_taps:
                    term = dw_ref[k:k + 1, cs] * ext[SUBLANES * a:SUBLANES * a + rows, cs]
                    q = term if q is None else q + term
            part = q if s == 0 else q[s:s + tm]
            acc = part if acc is None else acc + part
        conv_scr[:, cs] = acc
    ext[0:halo, :] = ext[tm:tm + halo, :]

    y = _layernorm(conv_scr[...] + dwb_ref[...], lng_ref[...], lnb_ref[...])
    y = y * jax.nn.sigmoid(y)
    h = _dot(y.astype(BF16), wout_ref[...]) + bout_ref[...]
    o_ref[...] = x + _rms(h, gpost_ref[...])


def _conv_mixer(x, gpre, gpost, w_in, b_in, dw, dw_b, ln_g, ln_b, w_out, b_out, tm=256):
    L, D = x.shape
    return pl.pallas_call(
        _conv_body,
        grid=(L // tm,),
        in_specs=[
            pl.BlockSpec((tm, D), lambda i: (i, 0)),
            _resident((1, D)), _resident((1, D)), _resident((D, 2 * D)), _resident((1, 2 * D)),
            _resident(dw.shape), _resident((1, D)), _resident((1, D)), _resident((1, D)),
            _resident((D, D)), _resident((1, D)),
        ],
        out_specs=pl.BlockSpec((tm, D), lambda i: (i, 0)),
        out_shape=jax.ShapeDtypeStruct((L, D), F32),
        scratch_shapes=[pltpu.VMEM((tm + CONV_HALO, D), F32), pltpu.VMEM((tm, D), F32)],
        compiler_params=_params("arbitrary"),
        name="conv_mixer",
    )(x, _row(gpre), _row(gpost), w_in.astype(BF16), _row(b_in), dw, _row(dw_b), _row(ln_g), _row(ln_b),
      w_out.astype(BF16), _row(b_out))


def _gmlp_body(x_ref, gpre_ref, gpost_ref, win_ref, bin_ref, lng_ref, lnb_ref, ws_ref, bs_ref,
               wout_ref, bout_ref, o_ref, z_scr, su_scr, *, fc):
    n_sub, sub, _ = su_scr.shape
    E = lng_ref.shape[1]
    n_heads, chunk, _ = ws_ref.shape
    hw = E // n_heads
    causal = (lax.broadcasted_iota(jnp.int32, (chunk, chunk), 1)
              <= lax.broadcasted_iota(jnp.int32, (chunk, chunk), 0))
    wm = [jnp.where(causal, ws_ref[h], 0.0).astype(BF16) for h in range(n_heads)]
    for i in range(n_sub):
        rows = slice(i * sub, (i + 1) * sub)
        x = x_ref[rows, :]
        xn = _rms(x, gpre_ref[...]).astype(BF16)
        for c in range(2 * E // fc):
            cs = slice(c * fc, (c + 1) * fc)
            z_scr[i, :, cs] = jax.nn.gelu(_dot(xn, win_ref[:, cs]) + bin_ref[:, cs])
        vb = _layernorm(z_scr[i, :, E:], lng_ref[...], lnb_ref[...]).astype(BF16)
        for h in range(n_heads):
            bias = bs_ref[:, h:h + 1]
            hs = slice(h * hw, (h + 1) * hw)
            for c in range(sub // chunk):
                rs = slice(c * chunk, (c + 1) * chunk)
                s = _dot(wm[h], vb[rs, hs]) + bias
                su_scr[i, rs, hs] = (z_scr[i, rs, hs] * s).astype(BF16)
        hout = _dot(su_scr[i], wout_ref[...]) + bout_ref[...]
        o_ref[rows, :] = x + _rms(hout, gpost_ref[...])


def _gmlp_mixer(x, gpre, gpost, w_in, b_in, ln_g, ln_b, w_s, b_s, w_out, b_out, tm=512, sub=512, fc=512):
    L, D = x.shape
    E = ln_g.shape[0]
    return pl.pallas_call(
        functools.partial(_gmlp_body, fc=fc),
        grid=(L // tm,),
        in_specs=[
            pl.BlockSpec((tm, D), lambda i: (i, 0)),
            _resident((1, D)), _resident((1, D)), _resident((D, 2 * E)), _resident((1, 2 * E)),
            _resident((1, E)), _resident((1, E)),
            _resident(w_s.shape), _resident((GM_CHUNK, GM_HEADS)),
            _resident((E, D)), _resident((1, D)),
        ],
        out_specs=pl.BlockSpec((tm, D), lambda i: (i, 0)),
        out_shape=jax.ShapeDtypeStruct((L, D), F32),
        scratch_shapes=[pltpu.VMEM((tm // sub, sub, 2 * E), F32), pltpu.VMEM((tm // sub, sub, E), BF16)],
        compiler_params=_params("parallel"),
        name="gmlp_mixer",
    )(x, _row(gpre), _row(gpost), w_in.astype(BF16), _row(b_in), _row(ln_g), _row(ln_b),
      w_s, b_s.T, w_out.astype(BF16), _row(b_out))


def _regroup_body(x_ref, g_ref, *refs):
    *o_refs, xt_scr = refs
    xn = _rms(x_ref[...], g_ref[...])
    for c in range(xt_scr.shape[0]):
        xt_scr[c] = xn[:, c * LANES:(c + 1) * LANES]
    for o_ref in o_refs:
        d, rows, _ = o_ref.shape
        if d == 1:
            o_ref[0] = xn.astype(o_ref.dtype)
            continue
        for r in range(d):
            o_ref[r] = jnp.concatenate([xt_scr[c, pl.ds(r, rows, stride=d), :] for c in range(xt_scr.shape[0])],
                                       axis=-1).astype(o_ref.dtype)


def _regroup_rows(x, g, dilations, tm=512):
    L, D = x.shape
    return pl.pallas_call(
        _regroup_body,
        grid=(L // tm,),
        in_specs=[pl.BlockSpec((tm, D), lambda i: (i, 0)), _resident((1, D))],
        out_specs=[pl.BlockSpec((d, tm // d, D), lambda i: (0, i, 0)) for d in dilations],
        out_shape=[jax.ShapeDtypeStruct((d, L // d, D), BF16) for d in dilations],
        scratch_shapes=[pltpu.VMEM((D // LANES, tm, LANES), F32)],
        compiler_params=_params("parallel"),
        name="regroup_rows",
    )(x, _row(g))


def _t5_buckets(dilation):
    delta = (np.arange(BLOCK)[:, None] + BLOCK) - np.arange(2 * BLOCK)[None, :]
    dist = np.maximum(delta, 0) * dilation
    max_exact = NUM_BUCKETS // 2
    distf = np.maximum(dist, 1).astype(np.float32)
    large = max_exact + (np.log(distf / np.float32(max_exact)) / np.float32(math.log(MAX_DISTANCE / max_exact))
                         * np.float32(NUM_BUCKETS - max_exact)).astype(np.int32)
    large = np.minimum(large, NUM_BUCKETS - 1)
    return np.where(dist < max_exact, dist, large).astype(np.int32)


def _attn_body(tab_ref, bkt_ref, xc_ref, xn_ref, w_ref, o_ref, lse_ref,
               bias_scr, q_cur, k_win, v_win, q_nxt, k_nxt, v_nxt, s_scr, p_scr, *, scale):
    blk = bias_scr.shape[2]
    n_heads = bias_scr.shape[1]
    D = q_cur.shape[1]
    hd = D // n_heads
    qb = q_cur.shape[0] // blk
    n = pl.program_id(1)

    def project(x_ref, q_dst, k_dst, v_dst, k_off):
        xb = x_ref[...]
        q_dst[...] = (_dot(xb, w_ref[:, 0:D]) * scale).astype(BF16)
        k_dst[k_off:k_off + qb * blk, :] = _dot(xb, w_ref[:, D:2 * D]).astype(BF16)
        v_dst[k_off:k_off + qb * blk, :] = _dot(xb, w_ref[:, 2 * D:3 * D]).astype(BF16)

    @pl.when(n == 0)
    def _():
        k_win[0:blk, :] = jnp.zeros((blk, D), BF16)
        v_win[0:blk, :] = jnp.zeros((blk, D), BF16)

    @pl.when((pl.program_id(0) == 0) & (n == 0))
    def _():
        project(xc_ref, q_cur, k_win, v_win, blk)

    @pl.when((pl.program_id(0) == 0) & (n == 0))
    def _():
        bkt = bkt_ref[...]
        qi = lax.broadcasted_iota(jnp.int32, bkt.shape, 0)
        ki = lax.broadcasted_iota(jnp.int32, bkt.shape, 1)
        delta = qi + blk - ki
        in_band = (delta >= 0) & (delta <= blk)
        for h in range(n_heads):
            acc = jnp.zeros(bkt.shape, F32)
            for b in range(tab_ref.shape[0]):
                acc = jnp.where(bkt == b, tab_ref[b, h], acc)
            acc = jnp.where(in_band, acc, NEG_BIG)
            bias_scr[1, h] = acc
            bias_scr[0, h] = jnp.where(ki >= blk, acc, NEG_BIG)

    dims = (((1,), (1,)), ((), ()))
    per_tile = LANES // hd
    lane = lax.broadcasted_iota(jnp.int32, (1, LANES), 1)
    own = [(lane >= i * hd) & (lane < (i + 1) * hd) for i in range(per_tile)]
    n_tiles = n_heads // per_tile

    pending = [(dst, which * D + c * MXU_WIDTH, c) for which, dst in enumerate((q_nxt, k_nxt, v_nxt))
               for c in range(D // MXU_WIDTH)]
    n_chunks = len(pending)
    n_slots = qb * n_heads

    def project_chunks(slot_idx):
        due = n_chunks * (slot_idx + 1) // n_slots
        while n_chunks - len(pending) < due:
            dst, col, c = pending.pop(0)
            res = _dot(xn_ref[...], w_ref[:, col:col + MXU_WIDTH])
            if dst is q_nxt:
                res = res * scale
            dst[:, c * MXU_WIDTH:(c + 1) * MXU_WIDTH] = res.astype(BF16)

    for b in range(qb):
        rows = slice(b * blk, (b + 1) * blk)
        keys = slice(b * blk, (b + 2) * blk)
        slot = jnp.minimum(n, 1) if b == 0 else 1
        for t in range(n_tiles):
            cs = slice(t * LANES, (t + 1) * LANES)
            qt = q_cur[rows, cs]
            kt = k_win[keys, cs]
            for i in range(per_tile):
                h = t * per_tile + i
                qh = jnp.where(own[i], qt, jnp.zeros_like(qt))
                s_scr[b, h] = lax.dot_general(qh, kt, dims, preferred_element_type=F32) + bias_scr[slot, h]

        lse_b = jnp.zeros((blk, LANES), F32)
        for h in range(n_heads):
            s = s_scr[b, h]
            m = jnp.max(s, axis=-1, keepdims=True)
            e = jnp.exp(s - m)
            den = jnp.sum(e, axis=-1, keepdims=True)
            p_scr[b, h] = (e * (1.0 / den)).astype(BF16)
            lse_b = jnp.where(lane == h, m + jnp.log(den), lse_b)
            project_chunks(b * n_heads + h)
        lse_ref[rows, :] = lse_b

        for t in range(n_tiles):
            cs = slice(t * LANES, (t + 1) * LANES)
            vt = v_win[keys, cs]
            acc = jnp.zeros((blk, LANES), F32)
            for i in range(per_tile):
                acc = acc + _dot(p_scr[b, t * per_tile + i], jnp.where(own[i], vt, jnp.zeros_like(vt)))
            o_ref[rows, cs] = acc

    k_win[0:blk, :] = k_win[qb * blk:(qb + 1) * blk, :]
    v_win[0:blk, :] = v_win[qb * blk:(qb + 1) * blk, :]
    k_win[blk:(qb + 1) * blk, :] = k_nxt[...]
    v_win[blk:(qb + 1) * blk, :] = v_nxt[...]
    q_cur[...] = q_nxt[...]


def _dilated_attention(xn, w_qkv, pattern, table, dilation, qb=2):
    d, rows, D = xn.shape
    n_heads = D // HEAD_DIM
    tq = qb * BLOCK
    n_steps = rows // tq
    out = lambda w: pl.BlockSpec((None, tq, w), lambda r, n: (r, n, 0))

    def following(r, n):
        nxt = jnp.minimum(r * n_steps + n + 1, d * n_steps - 1)
        return nxt // n_steps, nxt % n_steps, 0

    return pl.pallas_call(
        functools.partial(_attn_body, scale=HEAD_DIM ** -0.5),
        grid=(d, n_steps),
        in_specs=[
            pl.BlockSpec(memory_space=pltpu.SMEM),
            pl.BlockSpec((BLOCK, 2 * BLOCK), lambda r, n: (0, 0), pipeline_mode=pl.Buffered(1)),
            pl.BlockSpec((None, tq, D), lambda r, n: (0, 0, 0)),
            pl.BlockSpec((None, tq, D), following),
            pl.BlockSpec((D, 3 * D), lambda r, n: (0, pattern), pipeline_mode=pl.Buffered(1)),
        ],
        out_specs=[out(D), out(LANES)],
        out_shape=[jax.ShapeDtypeStruct((d, rows, D), F32), jax.ShapeDtypeStruct((d, rows, LANES), F32)],
        scratch_shapes=[pltpu.VMEM((2, n_heads, BLOCK, 2 * BLOCK), F32),
                        pltpu.VMEM((tq, D), BF16),
                        pltpu.VMEM((tq + BLOCK, D), BF16), pltpu.VMEM((tq + BLOCK, D), BF16),
                        pltpu.VMEM((tq, D), BF16), pltpu.VMEM((tq, D), BF16), pltpu.VMEM((tq, D), BF16),
                        pltpu.VMEM((qb, n_heads, BLOCK, 2 * BLOCK), F32),
                        pltpu.VMEM((qb, n_heads, BLOCK, 2 * BLOCK), BF16)],
        compiler_params=_params("arbitrary", "arbitrary"),
        name=f"dilated_attn_d{dilation}",
    )(table, jnp.asarray(_t5_buckets(dilation)), xn, xn, w_qkv)


def _attn_out_body(x_ref, gpost_ref, o0_ref, o1_ref, o2_ref, l0_ref, l1_ref, l2_ref, wout_ref, out_ref, *scr):
    def natural(ref, s):
        d, rows, width = ref.shape
        if d == 1:
            return ref[0]
        n_lt = width // LANES
        for c in range(n_lt):
            for r in range(d):
                s[c, pl.ds(r, rows, stride=d), :] = ref[r, :, c * LANES:(c + 1) * LANES]
        return jnp.concatenate([s[c] for c in range(n_lt)], axis=-1)

    D = x_ref.shape[1]
    hd = D // (D // HEAD_DIM)
    spread = (lax.broadcasted_iota(jnp.int32, (LANES, D), 1) // hd
              == lax.broadcasted_iota(jnp.int32, (LANES, D), 0)).astype(BF16)

    def per_column(w):
        hi = w.astype(BF16)
        lo = (w - hi.astype(F32)).astype(BF16)
        return _dot(hi, spread) + _dot(lo, spread)

    l0, l1, l2 = natural(l0_ref, None), natural(l1_ref, scr[0]), natural(l2_ref, scr[1])
    m = jnp.maximum(jnp.maximum(l0, l1), l2)
    e0, e1, e2 = jnp.exp(l0 - m), jnp.exp(l1 - m), jnp.exp(l2 - m)
    inv = 1.0 / (e0 + e1 + e2)
    o = (per_column(e0 * inv) * natural(o0_ref, None) + per_column(e1 * inv) * natural(o1_ref, scr[2])
         + per_column(e2 * inv) * natural(o2_ref, scr[3]))
    h = _dot(o.astype(BF16), wout_ref[...])
    out_ref[...] = x_ref[...] + _rms(h, gpost_ref[...])


def _attention_mixer(x, gpre, gpost, w_qkv, w_out, rel_bias, tm=512):
    L, D = x.shape
    n_heads = D // HEAD_DIM
    outs, lses = [], []
    w_qkv = w_qkv.astype(BF16)
    xns = _regroup_rows(x, gpre, [d for _, d in PATTERNS])
    for g, (window, dilation) in enumerate(PATTERNS):
        assert window // dilation == BLOCK and L % (BLOCK * dilation) == 0
        o, lse = _dilated_attention(xns[g], w_qkv, g, rel_bias[:, g * n_heads:(g + 1) * n_heads], dilation)
        outs.append(o)
        lses.append(lse)
    tile = pl.BlockSpec((tm, D), lambda i: (i, 0))
    grouped = lambda w: [pl.BlockSpec((d, tm // d, w), lambda i: (0, i, 0)) for _, d in PATTERNS]
    return pl.pallas_call(
        _attn_out_body,
        grid=(L // tm,),
        in_specs=[tile, _resident((1, D))] + grouped(D) + grouped(LANES) + [_resident((D, D))],
        out_specs=tile,
        out_shape=jax.ShapeDtypeStruct((L, D), F32),
        scratch_shapes=[pltpu.VMEM((1, tm, LANES), F32)] * 2 + [pltpu.VMEM((D // LANES, tm, LANES), F32)] * 2,
        compiler_params=_params("parallel"),
        name="attn_out",
    )(x, _row(gpost), *outs, *lses, w_out.astype(BF16))


def kernel(x, norm_pre, norm_post, ffn_w1, ffn_w3, ffn_w2, rel_bias, s5_w_in, s5_a_re, s5_a_im, s5_log_dt, s5_b_re, s5_b_im, s5_c_re, s5_c_im, s5_d, s5_w_glu, s5_b_glu, s5_w_out, cv_w_in, cv_b_in, cv_dw, cv_dw_b, cv_ln_g, cv_ln_b, cv_w_out, cv_b_out, gm_w_in, gm_b_in, gm_ln_g, gm_ln_b, gm_w_s, gm_b_s, gm_w_out, gm_b_out, at_w_qkv, at_w_out):
    bsz, seq, d_model = x.shape
    assert bsz == 1, "sequence mixers carry state along the row axis of one sequence"
    depth = norm_pre.shape[0]
    n_mixers = 4
    h = x.reshape(seq, d_model)
    stacks = (ffn_w1, ffn_w3, ffn_w2)
    mixer_mats = ((s5_w_in, s5_w_glu, s5_w_out), (cv_w_in, cv_w_out), (gm_w_in, gm_w_out), (at_w_qkv, at_w_out))
    weights = tuple(w[0, 0].astype(BF16) for w in stacks)
    for i in range(depth):
        kind, j = i % n_mixers, i // n_mixers
        casts = [(w, (i, 1)) for w in stacks] + [(w, (j,)) for w in mixer_mats[kind]]
        h, cast = _ffn(h, norm_pre[i, 0], norm_post[i, 0], weights, casts)
        weights, mats = tuple(cast[:3]), cast[3:]
        if kind == 0:
            h = _s5_mixer(h, norm_pre[i, 1], norm_post[i, 1], mats[0], s5_a_re[j], s5_a_im[j], s5_log_dt[j],
                          s5_b_re[j], s5_b_im[j], s5_c_re[j], s5_c_im[j], s5_d[j], mats[1], s5_b_glu[j], mats[2])
        elif kind == 1:
            h = _conv_mixer(h, norm_pre[i, 1], norm_post[i, 1], mats[0], cv_b_in[j], cv_dw[j], cv_dw_b[j],
                            cv_ln_g[j], cv_ln_b[j], mats[1], cv_b_out[j])
        elif kind == 2:
            h = _gmlp_mixer(h, norm_pre[i, 1], norm_post[i, 1], mats[0], gm_b_in[j], gm_ln_g[j], gm_ln_b[j],
                            gm_w_s[j], gm_b_s[j], mats[1], gm_b_out[j])
        else:
            h = _attention_mixer(h, norm_pre[i, 1], norm_post[i, 1], mats[0], mats[1], rel_bias)
        casts = [(w, (i + 1, 0)) for w in stacks] if i + 1 < depth else []
        h, cast = _ffn(h, norm_pre[i, 2], norm_post[i, 2], weights, casts)
        weights = tuple(cast)
    return h.reshape(bsz, seq, d_model)
```

```python
import functools
import math

import numpy as np
import jax
import jax.numpy as jnp
from jax import lax
from jax.experimental import pallas as pl
from jax.experimental.pallas import tpu as pltpu

F32 = jnp.float32
BF16 = jnp.bfloat16

EPS = 1e-6
LANES = 128
SUBLANES = 8
MXU_WIDTH = 256
S5_GROUP = 16
S5_STATE = 64
S5_GROUPS_PER_TILE = 8
CONV_W = 31
CONV_HALO = 32
GM_CHUNK = 128
GM_HEADS = 8
HEAD_DIM = 64
PATTERNS = ((128, 1), (512, 4), (2048, 16))
BLOCK = 128
NUM_BUCKETS = 32
MAX_DISTANCE = 2048
NEG_BIG = -1e30

VMEM_LIMIT_BYTES = 56 * 1024 * 1024


def _params(*sem):
    return pltpu.CompilerParams(dimension_semantics=sem, vmem_limit_bytes=VMEM_LIMIT_BYTES)


def _rms(x, g):
    return x * lax.rsqrt(jnp.mean(x * x, axis=-1, keepdims=True) + EPS) * g


def _layernorm(x, g, b):
    mu = jnp.mean(x, axis=-1, keepdims=True)
    xc = x - mu
    var = jnp.mean(xc * xc, axis=-1, keepdims=True)
    return xc * lax.rsqrt(var + EPS) * g + b


def _dot(a, b):
    return jnp.dot(a, b, preferred_element_type=F32)


def _row(v):
    return v.reshape(1, -1)


def _ffn_body(x_ref, gpre_ref, gpost_ref, w1_ref, w3_ref, w2_ref, *refs, fc):
    n_cast = (len(refs) - 2) // 2
    src_refs, o_ref, dst_refs, g_scr = refs[:n_cast], refs[n_cast], refs[n_cast + 1:-1], refs[-1]
    for src, dst in zip(src_refs, dst_refs):
        dst[...] = src[...].astype(dst.dtype)
    n_sub, sub, F = g_scr.shape
    for s in range(n_sub):
        rs = slice(s * sub, (s + 1) * sub)
        xn = _rms(x_ref[rs, :], gpre_ref[...]).astype(BF16)
        for c in range(F // fc):
            cs = slice(c * fc, (c + 1) * fc)
            h1 = _dot(xn, w1_ref[:, cs])
            h3 = _dot(xn, w3_ref[:, cs])
            g_scr[s, :, cs] = ((h1 * jax.nn.sigmoid(h1)) * h3).astype(BF16)
        h = _dot(g_scr[s], w2_ref[...])
        o_ref[rs, :] = x_ref[rs, :] + 0.5 * _rms(h, gpost_ref[...])


def _resident(shape):
    return pl.BlockSpec(shape, lambda i: (0,) * len(shape), pipeline_mode=pl.Buffered(1))


def _ffn(x, gpre, gpost, weights, casts=(), tm=1024, sub=512, fc=256):
    L, D = x.shape
    w1, w3, w2 = weights
    F = w1.shape[-1]
    n_steps = L // tm
    in_specs = [pl.BlockSpec((tm, D), lambda i: (i, 0)), _resident((1, D)), _resident((1, D)),
                _resident((D, F)), _resident((D, F)), _resident((F, D))]
    out_specs = [pl.BlockSpec((tm, D), lambda i: (i, 0))]
    out_shape = [jax.ShapeDtypeStruct((L, D), F32)]
    operands = [x, _row(gpre), _row(gpost), w1, w3, w2]
    for w, lead in casts:
        rows, cols = w.shape[-2] // n_steps, w.shape[-1]
        in_specs.append(pl.BlockSpec((None,) * len(lead) + (rows, cols), lambda i, lead=lead: lead + (i, 0)))
        out_specs.append(pl.BlockSpec((rows, cols), lambda i: (i, 0)))
        out_shape.append(jax.ShapeDtypeStruct(w.shape[-2:], BF16))
        operands.append(w)
    outs = pl.pallas_call(
        functools.partial(_ffn_body, fc=fc),
        grid=(n_steps,),
        in_specs=in_specs,
        out_specs=out_specs,
        out_shape=out_shape,
        scratch_shapes=[pltpu.VMEM((tm // sub, sub, F), BF16)],
        compiler_params=_params("arbitrary"),
        name="ffn",
    )(*operands)
    return outs[0], list(outs[1:])


def _s5_body(x_ref, gpre_ref, gpost_ref, win_ref, are_ref, aim_ref, ldt_ref, bdre_ref, bdim_ref,
             cdre_ref, cdim_ref, d_ref, wglu_ref, bglu_ref, wout_ref, o_ref,
             abar_re, abar_im, bb_re, bb_im, st_re, st_im, bu_re, bu_im, u_scr, y_scr):
    tt = x_ref.shape[0]
    n_tiles, uw, sw = bdre_ref.shape
    nj = sw // LANES

    @pl.when(pl.program_id(0) == 0)
    def _():
        ar = are_ref[...]
        ai = aim_ref[...]
        dt = jnp.exp(ldt_ref[...])
        mag = jnp.exp(dt * ar)
        abr = mag * jnp.cos(dt * ai)
        abi = mag * jnp.sin(dt * ai)
        den = ar * ar + ai * ai
        nr = abr - 1.0
        fre = (nr * ar + abi * ai) / den
        fim = (abi * ar - nr * ai) / den
        abar_re[...] = abr
        abar_im[...] = abi
        for k in range(n_tiles):
            fr = fre[k:k + 1, :]
            fi = fim[k:k + 1, :]
            bb_re[k] = (fr * bdre_ref[k] - fi * bdim_ref[k]).astype(BF16)
            bb_im[k] = (fr * bdim_ref[k] + fi * bdre_ref[k]).astype(BF16)
        st_re[...] = jnp.zeros_like(st_re)
        st_im[...] = jnp.zeros_like(st_im)

    n_sub = bu_re.shape[0]
    sub = tt // n_sub
    for s in range(n_sub):
        rs = slice(s * sub, (s + 1) * sub)
        xn = _rms(x_ref[rs, :], gpre_ref[...]).astype(BF16)
        u = _dot(xn, win_ref[...])
        u_scr[rs, :] = u
        ub = u.astype(BF16)
        for k in range(n_tiles):
            uk = ub[:, k * uw:(k + 1) * uw]
            br = _dot(uk, bb_re[k])
            bi = _dot(uk, bb_im[k])
            for j in range(nj):
                bu_re[s, j, pl.ds(k, sub, stride=n_tiles), :] = br[:, j * LANES:(j + 1) * LANES]
                bu_im[s, j, pl.ds(k, sub, stride=n_tiles), :] = bi[:, j * LANES:(j + 1) * LANES]

    a_r = [abar_re[:, j * LANES:(j + 1) * LANES] for j in range(nj)]
    a_i = [abar_im[:, j * LANES:(j + 1) * LANES] for j in range(nj)]
    state = [(st_re[j], st_im[j]) for j in range(nj)]
    for s in range(n_sub):
        for t in range(sub):
            rows = slice(t * n_tiles, (t + 1) * n_tiles)
            for j in range(nj):
                sr, si = state[j]
                nr = a_r[j] * sr - a_i[j] * si + bu_re[s, j, rows, :]
                ni = a_r[j] * si + a_i[j] * sr + bu_im[s, j, rows, :]
                bu_re[s, j, rows, :] = nr
                bu_im[s, j, rows, :] = ni
                state[j] = (nr, ni)
    for j in range(nj):
        st_re[j], st_im[j] = state[j]

    for s in range(n_sub):
        rs = slice(s * sub, (s + 1) * sub)
        for k in range(n_tiles):
            sre = jnp.concatenate([bu_re[s, j, pl.ds(k, sub, stride=n_tiles), :] for j in range(nj)], axis=-1)
            sim = jnp.concatenate([bu_im[s, j, pl.ds(k, sub, stride=n_tiles), :] for j in range(nj)], axis=-1)
            y_scr[rs, k * uw:(k + 1) * uw] = (_dot(sre.astype(BF16), cdre_ref[k])
                                              - _dot(sim.astype(BF16), cdim_ref[k]))
        y = jax.nn.gelu(y_scr[rs, :] + d_ref[...] * u_scr[rs, :])
        z = y * jax.nn.sigmoid(_dot(y.astype(BF16), wglu_ref[...]) + bglu_ref[...])
        h = _dot(z.astype(BF16), wout_ref[...])
        o_ref[rs, :] = x_ref[rs, :] + _rms(h, gpost_ref[...])


def _s5_block_diag(b, c):
    G, P, HG = b.shape
    gt = S5_GROUPS_PER_TILE
    eye = jnp.eye(gt, dtype=b.dtype)
    bt = b.reshape(G // gt, gt, P, HG).transpose(0, 1, 3, 2)
    bd = jnp.einsum('kghp,gj->kghjp', bt, eye).reshape(G // gt, gt * HG, gt * P)
    ct = c.reshape(G // gt, gt, HG, P)
    cd = jnp.einsum('kghp,gj->kjpgh', ct, eye).reshape(G // gt, gt * P, gt * HG)
    return bd, cd


def _s5_mixer(x, gpre, gpost, w_in, a_re, a_im, log_dt, b_re, b_im, c_re, c_im, d_skip, w_glu, b_glu,
              w_out, tt=512, sub=512):
    L, D = x.shape
    G, P = a_re.shape
    bd_re, cd_re = _s5_block_diag(b_re, c_re)
    bd_im, cd_im = _s5_block_diag(b_im, c_im)
    n_tiles, uw, sw = bd_re.shape
    assert n_tiles == SUBLANES and uw == LANES and sw % LANES == 0
    nj = sw // LANES
    per_tile = lambda v: v.reshape(n_tiles, sw)
    return pl.pallas_call(
        _s5_body,
        grid=(L // tt,),
        in_specs=[
            pl.BlockSpec((tt, D), lambda i: (i, 0)),
            _resident((1, D)), _resident((1, D)), _resident((D, D)),
            _resident((n_tiles, sw)), _resident((n_tiles, sw)), _resident((n_tiles, sw)),
            _resident((n_tiles, uw, sw)), _resident((n_tiles, uw, sw)),
            _resident((n_tiles, sw, uw)), _resident((n_tiles, sw, uw)),
            _resident((1, D)), _resident((D, D)), _resident((1, D)), _resident((D, D)),
        ],
        out_specs=pl.BlockSpec((tt, D), lambda i: (i, 0)),
        out_shape=jax.ShapeDtypeStruct((L, D), F32),
        scratch_shapes=[
            pltpu.VMEM((n_tiles, sw), F32), pltpu.VMEM((n_tiles, sw), F32),
            pltpu.VMEM((n_tiles, uw, sw), BF16), pltpu.VMEM((n_tiles, uw, sw), BF16),
            pltpu.VMEM((nj, n_tiles, LANES), F32), pltpu.VMEM((nj, n_tiles, LANES), F32),
            pltpu.VMEM((tt // sub, nj, sub * n_tiles, LANES), F32),
            pltpu.VMEM((tt // sub, nj, sub * n_tiles, LANES), F32),
            pltpu.VMEM((tt, D), F32), pltpu.VMEM((tt, D), F32),
        ],
        compiler_params=_params("arbitrary"),
        name="s5_mixer",
    )(x, _row(gpre), _row(gpost), w_in.astype(BF16),
      per_tile(a_re), per_tile(a_im), per_tile(jnp.broadcast_to(log_dt[:, None], (G, P))),
      bd_re, bd_im, cd_re.astype(BF16), cd_im.astype(BF16),
      _row(d_skip), w_glu.astype(BF16), _row(b_glu), w_out.astype(BF16))


def _conv_body(x_ref, gpre_ref, gpost_ref, win_ref, bin_ref, dw_ref, dwb_ref,
               lng_ref, lnb_ref, wout_ref, bout_ref, o_ref, ext, conv_scr):
    tm, D = x_ref.shape
    halo = ext.shape[0] - tm
    n_taps = dw_ref.shape[0]

    @pl.when(pl.program_id(0) == 0)
    def _():
        ext[0:halo, :] = jnp.zeros((halo, D), F32)

    x = x_ref[...]
    xn = _rms(x, gpre_ref[...]).astype(BF16)
    a = _dot(xn, win_ref[:, :D]) + bin_ref[:, :D]
    g = _dot(xn, win_ref[:, D:]) + bin_ref[:, D:]
    ext[halo:halo + tm, :] = a * jax.nn.sigmoid(g)

    base = halo - (n_taps - 1)
    for c in range(D // LANES):
        cs = slice(c * LANES, (c + 1) * LANES)
        acc = None
        for s in range(SUBLANES):
            rows = tm if s == 0 else tm + SUBLANES
            q = None
            for a in range((base + n_taps - 1) // SUBLANES + 1):
                k = SUBLANES * a + s - base
                if 0 <= k < n_taps:
                    term = dw_ref[k:k + 1, cs] * ext[SUBLANES * a:SUBLANES * a + rows, cs]
                    q = term if q is None else q + term
            part = q if s == 0 else q[s:s + tm]
            acc = part if acc is None else acc + part
        conv_scr[:, cs] = acc
    ext[0:halo, :] = ext[tm:tm + halo, :]

    y = _layernorm(conv_scr[...] + dwb_ref[...], lng_ref[...], lnb_ref[...])
    y = y * jax.nn.sigmoid(y)
    h = _dot(y.astype(BF16), wout_ref[...]) + bout_ref[...]
    o_ref[...] = x + _rms(h, gpost_ref[...])


def _conv_mixer(x, gpre, gpost, w_in, b_in, dw, dw_b, ln_g, ln_b, w_out, b_out, tm=512):
    L, D = x.shape
    return pl.pallas_call(
        _conv_body,
        grid=(L // tm,),
        in_specs=[
            pl.BlockSpec((tm, D), lambda i: (i, 0)),
            _resident((1, D)), _resident((1, D)), _resident((D, 2 * D)), _resident((1, 2 * D)),
            _resident(dw.shape), _resident((1, D)), _resident((1, D)), _resident((1, D)),
            _resident((D, D)), _resident((1, D)),
        ],
        out_specs=pl.BlockSpec((tm, D), lambda i: (i, 0)),
        out_shape=jax.ShapeDtypeStruct((L, D), F32),
        scratch_shapes=[pltpu.VMEM((tm + CONV_HALO, D), F32), pltpu.VMEM((tm, D), F32)],
        compiler_params=_params("arbitrary"),
        name="conv_mixer",
    )(x, _row(gpre), _row(gpost), w_in.astype(BF16), _row(b_in), dw, _row(dw_b), _row(ln_g), _row(ln_b),
      w_out.astype(BF16), _row(b_out))


def _gmlp_body(x_ref, gpre_ref, gpost_ref, win_ref, bin_ref, lng_ref, lnb_ref, ws_ref, bs_ref,
               wout_ref, bout_ref, o_ref, z_scr, su_scr, *, fc):
    n_sub, sub, _ = su_scr.shape
    E = lng_ref.shape[1]
    n_heads, chunk, _ = ws_ref.shape
    hw = E // n_heads
    causal = (lax.broadcasted_iota(jnp.int32, (chunk, chunk), 1)
              <= lax.broadcasted_iota(jnp.int32, (chunk, chunk), 0))
    wm = [jnp.where(causal, ws_ref[h], 0.0).astype(BF16) for h in range(n_heads)]
    for i in range(n_sub):
        rows = slice(i * sub, (i + 1) * sub)
        x = x_ref[rows, :]
        xn = _rms(x, gpre_ref[...]).astype(BF16)
        for c in range(2 * E // fc):
            cs = slice(c * fc, (c + 1) * fc)
            z_scr[i, :, cs] = jax.nn.gelu(_dot(xn, win_ref[:, cs]) + bin_ref[:, cs])
        vb = _layernorm(z_scr[i, :, E:], lng_ref[...], lnb_ref[...]).astype(BF16)
        for h in range(n_heads):
            bias = bs_ref[:, h:h + 1]
            hs = slice(h * hw, (h + 1) * hw)
            for c in range(sub // chunk):
                rs = slice(c * chunk, (c + 1) * chunk)
                s = _dot(wm[h], vb[rs, hs]) + bias
                su_scr[i, rs, hs] = (z_scr[i, rs, hs] * s).astype(BF16)
        hout = _dot(su_scr[i], wout_ref[...]) + bout_ref[...]
        o_ref[rows, :] = x + _rms(hout, gpost_ref[...])


def _gmlp_mixer(x, gpre, gpost, w_in, b_in, ln_g, ln_b, w_s, b_s, w_out, b_out, tm=512, sub=512, fc=512):
    L, D = x.shape
    E = ln_g.shape[0]
    return pl.pallas_call(
        functools.partial(_gmlp_body, fc=fc),
        grid=(L // tm,),
        in_specs=[
            pl.BlockSpec((tm, D), lambda i: (i, 0)),
            _resident((1, D)), _resident((1, D)), _resident((D, 2 * E)), _resident((1, 2 * E)),
            _resident((1, E)), _resident((1, E)),
            _resident(w_s.shape), _resident((GM_CHUNK, GM_HEADS)),
            _resident((E, D)), _resident((1, D)),
        ],
        out_specs=pl.BlockSpec((tm, D), lambda i: (i, 0)),
        out_shape=jax.ShapeDtypeStruct((L, D), F32),
        scratch_shapes=[pltpu.VMEM((tm // sub, sub, 2 * E), F32), pltpu.VMEM((tm // sub, sub, E), BF16)],
        compiler_params=_params("parallel"),
        name="gmlp_mixer",
    )(x, _row(gpre), _row(gpost), w_in.astype(BF16), _row(b_in), _row(ln_g), _row(ln_b),
      w_s, b_s.T, w_out.astype(BF16), _row(b_out))


def _regroup_body(x_ref, g_ref, *refs):
    *o_refs, xt_scr = refs
    xn = _rms(x_ref[...], g_ref[...])
    for c in range(xt_scr.shape[0]):
        xt_scr[c] = xn[:, c * LANES:(c + 1) * LANES]
    for o_ref in o_refs:
        d, rows, _ = o_ref.shape
        if d == 1:
            o_ref[0] = xn.astype(o_ref.dtype)
            continue
        for r in range(d):
            o_ref[r] = jnp.concatenate([xt_scr[c, pl.ds(r, rows, stride=d), :] for c in range(xt_scr.shape[0])],
                                       axis=-1).astype(o_ref.dtype)


def _regroup_rows(x, g, dilations, tm=512):
    L, D = x.shape
    return pl.pallas_call(
        _regroup_body,
        grid=(L // tm,),
        in_specs=[pl.BlockSpec((tm, D), lambda i: (i, 0)), _resident((1, D))],
        out_specs=[pl.BlockSpec((d, tm // d, D), lambda i: (0, i, 0)) for d in dilations],
        out_shape=[jax.ShapeDtypeStruct((d, L // d, D), BF16) for d in dilations],
        scratch_shapes=[pltpu.VMEM((D // LANES, tm, LANES), F32)],
        compiler_params=_params("parallel"),
        name="regroup_rows",
    )(x, _row(g))


def _t5_buckets(dilation):
    delta = (np.arange(BLOCK)[:, None] + BLOCK) - np.arange(2 * BLOCK)[None, :]
    dist = np.maximum(delta, 0) * dilation
    max_exact = NUM_BUCKETS // 2
    distf = np.maximum(dist, 1).astype(np.float32)
    large = max_exact + (np.log(distf / np.float32(max_exact)) / np.float32(math.log(MAX_DISTANCE / max_exact))
                         * np.float32(NUM_BUCKETS - max_exact)).astype(np.int32)
    large = np.minimum(large, NUM_BUCKETS - 1)
    return np.where(dist < max_exact, dist, large).astype(np.int32)


def _attn_body(tab_ref, bkt_ref, xc_ref, xn_ref, w_ref, o_ref, lse_ref,
               bias_scr, q_cur, k_win, v_win, q_nxt, k_nxt, v_nxt, s_scr, p_scr, *, scale):
    blk = bias_scr.shape[2]
    n_heads = bias_scr.shape[1]
    D = q_cur.shape[1]
    hd = D // n_heads
    qb = q_cur.shape[0] // blk
    n = pl.program_id(1)

    def project(x_ref, q_dst, k_dst, v_dst, k_off):
        xb = x_ref[...]
        q_dst[...] = (_dot(xb, w_ref[:, 0:D]) * scale).astype(BF16)
        k_dst[k_off:k_off + qb * blk, :] = _dot(xb, w_ref[:, D:2 * D]).astype(BF16)
        v_dst[k_off:k_off + qb * blk, :] = _dot(xb, w_ref[:, 2 * D:3 * D]).astype(BF16)

    @pl.when(n == 0)
    def _():
        k_win[0:blk, :] = jnp.zeros((blk, D), BF16)
        v_win[0:blk, :] = jnp.zeros((blk, D), BF16)

    @pl.when((pl.program_id(0) == 0) & (n == 0))
    def _():
        project(xc_ref, q_cur, k_win, v_win, blk)

    @pl.when((pl.program_id(0) == 0) & (n == 0))
    def _():
        bkt = bkt_ref[...]
        qi = lax.broadcasted_iota(jnp.int32, bkt.shape, 0)
        ki = lax.broadcasted_iota(jnp.int32, bkt.shape, 1)
        delta = qi + blk - ki
        in_band = (delta >= 0) & (delta <= blk)
        for h in range(n_heads):
            acc = jnp.zeros(bkt.shape, F32)
            for b in range(tab_ref.shape[0]):
                acc = jnp.where(bkt == b, tab_ref[b, h], acc)
            acc = jnp.where(in_band, acc, NEG_BIG)
            bias_scr[1, h] = acc
            bias_scr[0, h] = jnp.where(ki >= blk, acc, NEG_BIG)

    dims = (((1,), (1,)), ((), ()))
    per_tile = LANES // hd
    lane = lax.broadcasted_iota(jnp.int32, (1, LANES), 1)
    own = [(lane >= i * hd) & (lane < (i + 1) * hd) for i in range(per_tile)]
    n_tiles = n_heads // per_tile

    pending = [(dst, which * D + c * MXU_WIDTH, c) for which, dst in enumerate((q_nxt, k_nxt, v_nxt))
               for c in range(D // MXU_WIDTH)]
    n_chunks = len(pending)
    n_slots = qb * n_heads

    def project_chunks(slot_idx):
        due = n_chunks * (slot_idx + 1) // n_slots
        while n_chunks - len(pending) < due:
            dst, col, c = pending.pop(0)
            res = _dot(xn_ref[...], w_ref[:, col:col + MXU_WIDTH])
            if dst is q_nxt:
                res = res * scale
            dst[:, c * MXU_WIDTH:(c + 1) * MXU_WIDTH] = res.astype(BF16)

    for b in range(qb):
        rows = slice(b * blk, (b + 1) * blk)
        keys = slice(b * blk, (b + 2) * blk)
        slot = jnp.minimum(n, 1) if b == 0 else 1
        for t in range(n_tiles):
            cs = slice(t * LANES, (t + 1) * LANES)
            qt = q_cur[rows, cs]
            kt = k_win[keys, cs]
            for i in range(per_tile):
                h = t * per_tile + i
                qh = jnp.where(own[i], qt, jnp.zeros_like(qt))
                s_scr[b, h] = lax.dot_general(qh, kt, dims, preferred_element_type=F32) + bias_scr[slot, h]

        lse_b = jnp.zeros((blk, LANES), F32)
        for h in range(n_heads):
            s = s_scr[b, h]
            m = jnp.max(s, axis=-1, keepdims=True)
            e = jnp.exp(s - m)
            den = jnp.sum(e, axis=-1, keepdims=True)
            p_scr[b, h] = (e * (1.0 / den)).astype(BF16)
            lse_b = jnp.where(lane == h, m + jnp.log(den), lse_b)
            project_chunks(b * n_heads + h)
        lse_ref[rows, :] = lse_b

        for t in range(n_tiles):
            cs = slice(t * LANES, (t + 1) * LANES)
            vt = v_win[keys, cs]
            acc = jnp.zeros((blk, LANES), F32)
            for i in range(per_tile):
                acc = acc + _dot(p_scr[b, t * per_tile + i], jnp.where(own[i], vt, jnp.zeros_like(vt)))
            o_ref[rows, cs] = acc

    k_win[0:blk, :] = k_win[qb * blk:(qb + 1) * blk, :]
    v_win[0:blk, :] = v_win[qb * blk:(qb + 1) * blk, :]
    k_win[blk:(qb + 1) * blk, :] = k_nxt[...]
    v_win[blk:(qb + 1) * blk, :] = v_nxt[...]
    q_cur[...] = q_nxt[...]


def _dilated_attention(xn, w_qkv, pattern, table, dilation, qb=2):
    d, rows, D = xn.shape
    n_heads = D // HEAD_DIM
    tq = qb * BLOCK
    n_steps = rows // tq
    out = lambda w: pl.BlockSpec((None, tq, w), lambda r, n: (r, n, 0))

    def following(r, n):
        nxt = jnp.minimum(r * n_steps + n + 1, d * n_steps - 1)
        return nxt // n_steps, nxt % n_steps, 0

    return pl.pallas_call(
        functools.partial(_attn_body, scale=HEAD_DIM ** -0.5),
        grid=(d, n_steps),
        in_specs=[
            pl.BlockSpec(memory_space=pltpu.SMEM),
            pl.BlockSpec((BLOCK, 2 * BLOCK), lambda r, n: (0, 0), pipeline_mode=pl.Buffered(1)),
            pl.BlockSpec((None, tq, D), lambda r, n: (0, 0, 0)),
            pl.BlockSpec((None, tq, D), following),
            pl.BlockSpec((D, 3 * D), lambda r, n: (0, pattern), pipeline_mode=pl.Buffered(1)),
        ],
        out_specs=[out(D), out(LANES)],
        out_shape=[jax.ShapeDtypeStruct((d, rows, D), F32), jax.ShapeDtypeStruct((d, rows, LANES), F32)],
        scratch_shapes=[pltpu.VMEM((2, n_heads, BLOCK, 2 * BLOCK), F32),
                        pltpu.VMEM((tq, D), BF16),
                        pltpu.VMEM((tq + BLOCK, D), BF16), pltpu.VMEM((tq + BLOCK, D), BF16),
                        pltpu.VMEM((tq, D), BF16), pltpu.VMEM((tq, D), BF16), pltpu.VMEM((tq, D), BF16),
                        pltpu.VMEM((qb, n_heads, BLOCK, 2 * BLOCK), F32),
                        pltpu.VMEM((qb, n_heads, BLOCK, 2 * BLOCK), BF16)],
        compiler_params=_params("arbitrary", "arbitrary"),
        name=f"dilated_attn_d{dilation}",
    )(table, jnp.asarray(_t5_buckets(dilation)), xn, xn, w_qkv)


def _attn_out_body(x_ref, gpost_ref, o0_ref, o1_ref, o2_ref, l0_ref, l1_ref, l2_ref, wout_ref, out_ref, *scr):
    def natural(ref, s):
        d, rows, width = ref.shape
        if d == 1:
            return ref[0]
        n_lt = width // LANES
        for c in range(n_lt):
            for r in range(d):
                s[c, pl.ds(r, rows, stride=d), :] = ref[r, :, c * LANES:(c + 1) * LANES]
        return jnp.concatenate([s[c] for c in range(n_lt)], axis=-1)

    D = x_ref.shape[1]
    hd = D // (D // HEAD_DIM)
    spread = (lax.broadcasted_iota(jnp.int32, (LANES, D), 1) // hd
              == lax.broadcasted_iota(jnp.int32, (LANES, D), 0)).astype(BF16)

    def per_column(w):
        hi = w.astype(BF16)
        lo = (w - hi.astype(F32)).astype(BF16)
        return _dot(hi, spread) + _dot(lo, spread)

    l0, l1, l2 = natural(l0_ref, None), natural(l1_ref, scr[0]), natural(l2_ref, scr[1])
    m = jnp.maximum(jnp.maximum(l0, l1), l2)
    e0, e1, e2 = jnp.exp(l0 - m), jnp.exp(l1 - m), jnp.exp(l2 - m)
    inv = 1.0 / (e0 + e1 + e2)
    o = (per_column(e0 * inv) * natural(o0_ref, None) + per_column(e1 * inv) * natural(o1_ref, scr[2])
         + per_column(e2 * inv) * natural(o2_ref, scr[3]))
    h = _dot(o.astype(BF16), wout_ref[...])
    out_ref[...] = x_ref[...] + _rms(h, gpost_ref[...])


def _attention_mixer(x, gpre, gpost, w_qkv, w_out, rel_bias, tm=512):
    L, D = x.shape
    n_heads = D // HEAD_DIM
    outs, lses = [], []
    w_qkv = w_qkv.astype(BF16)
    xns = _regroup_rows(x, gpre, [d for _, d in PATTERNS])
    for g, (window, dilation) in enumerate(PATTERNS):
        assert window // dilation == BLOCK and L % (BLOCK * dilation) == 0
        o, lse = _dilated_attention(xns[g], w_qkv, g, rel_bias[:, g * n_heads:(g + 1) * n_heads], dilation)
        outs.append(o)
        lses.append(lse)
    tile = pl.BlockSpec((tm, D), lambda i: (i, 0))
    grouped = lambda w: [pl.BlockSpec((d, tm // d, w), lambda i: (0, i, 0)) for _, d in PATTERNS]
    return pl.pallas_call(
        _attn_out_body,
        grid=(L // tm,),
        in_specs=[tile, _resident((1, D))] + grouped(D) + grouped(LANES) + [_resident((D, D))],
        out_specs=tile,
        out_shape=jax.ShapeDtypeStruct((L, D), F32),
        scratch_shapes=[pltpu.VMEM((1, tm, LANES), F32)] * 2 + [pltpu.VMEM((D // LANES, tm, LANES), F32)] * 2,
        compiler_params=_params("parallel"),
        name="attn_out",
    )(x, _row(gpost), *outs, *lses, w_out.astype(BF16))


def kernel(x, norm_pre, norm_post, ffn_w1, ffn_w3, ffn_w2, rel_bias, s5_w_in, s5_a_re, s5_a_im, s5_log_dt, s5_b_re, s5_b_im, s5_c_re, s5_c_im, s5_d, s5_w_glu, s5_b_glu, s5_w_out, cv_w_in, cv_b_in, cv_dw, cv_dw_b, cv_ln_g, cv_ln_b, cv_w_out, cv_b_out, gm_w_in, gm_b_in, gm_ln_g, gm_ln_b, gm_w_s, gm_b_s, gm_w_out, gm_b_out, at_w_qkv, at_w_out):
    bsz, seq, d_model = x.shape
    assert bsz == 1, "sequence mixers carry state along the row axis of one sequence"
    depth = norm_pre.shape[0]
    n_mixers = 4
    h = x.reshape(seq, d_model)
    stacks = (ffn_w1, ffn_w3, ffn_w2)
    mixer_mats = ((s5_w_in, s5_w_glu, s5_w_out), (cv_w_in, cv_w_out), (gm_w_in, gm_w_out), (at_w_qkv, at_w_out))
    weights = tuple(w[0, 0].astype(BF16) for w in stacks)
    for i in range(depth):
        kind, j = i % n_mixers, i // n_mixers
        casts = [(w, (i, 1)) for w in stacks] + [(w, (j,)) for w in mixer_mats[kind]]
        h, cast = _ffn(h, norm_pre[i, 0], norm_post[i, 0], weights, casts)
        weights, mats = tuple(cast[:3]), cast[3:]
        if kind == 0:
            h = _s5_mixer(h, norm_pre[i, 1], norm_post[i, 1], mats[0], s5_a_re[j], s5_a_im[j], s5_log_dt[j],
                          s5_b_re[j], s5_b_im[j], s5_c_re[j], s5_c_im[j], s5_d[j], mats[1], s5_b_glu[j], mats[2])
        elif kind == 1:
            h = _conv_mixer(h, norm_pre[i, 1], norm_post[i, 1], mats[0], cv_b_in[j], cv_dw[j], cv_dw_b[j],
                            cv_ln_g[j], cv_ln_b[j], mats[1], cv_b_out[j])
        elif kind == 2:
            h = _gmlp_mixer(h, norm_pre[i, 1], norm_post[i, 1], mats[0], gm_b_in[j], gm_ln_g[j], gm_ln_b[j],
                            gm_w_s[j], gm_b_s[j], mats[1], gm_b_out[j])
        else:
            h = _attention_mixer(h, norm_pre[i, 1], norm_post[i, 1], mats[0], mats[1], rel_bias)
        casts = [(w, (i + 1, 0)) for w in stacks] if i + 1 < depth else []
        h, cast = _ffn(h, norm_pre[i, 2], norm_post[i, 2], weights, casts)
        weights = tuple(cast)
    return h.reshape(bsz, seq, d_model)
```

```python
import functools
import math

import numpy as np
import jax
import jax.numpy as jnp
from jax import lax
from jax.experimental import pallas as pl
from jax.experimental.pallas import tpu as pltpu

F32 = jnp.float32
BF16 = jnp.bfloat16

EPS = 1e-6
LANES = 128
SUBLANES = 8
MXU_WIDTH = 256
S5_GROUP = 16
S5_STATE = 64
S5_GROUPS_PER_TILE = 8
CONV_W = 31
CONV_HALO = 32
GM_CHUNK = 128
GM_HEADS = 8
HEAD_DIM = 64
PATTERNS = ((128, 1), (512, 4), (2048, 16))
BLOCK = 128
NUM_BUCKETS = 32
MAX_DISTANCE = 2048
NEG_BIG = -1e30

VMEM_LIMIT_BYTES = 56 * 1024 * 1024


def _params(*sem):
    return pltpu.CompilerParams(dimension_semantics=sem, vmem_limit_bytes=VMEM_LIMIT_BYTES)


def _rms(x, g):
    return x * lax.rsqrt(jnp.mean(x * x, axis=-1, keepdims=True) + EPS) * g


def _layernorm(x, g, b):
    mu = jnp.mean(x, axis=-1, keepdims=True)
    xc = x - mu
    var = jnp.mean(xc * xc, axis=-1, keepdims=True)
    return xc * lax.rsqrt(var + EPS) * g + b


def _dot(a, b):
    return jnp.dot(a, b, preferred_element_type=F32)


def _row(v):
    return v.reshape(1, -1)


def _ffn_body(x_ref, gpre_ref, gpost_ref, w1_ref, w3_ref, w2_ref, *refs, fc):
    n_cast = (len(refs) - 2) // 2
    src_refs, o_ref, dst_refs, g_scr = refs[:n_cast], refs[n_cast], refs[n_cast + 1:-1], refs[-1]
    for src, dst in zip(src_refs, dst_refs):
        dst[...] = src[...].astype(dst.dtype)
    n_sub, sub, F = g_scr.shape
    for s in range(n_sub):
        rs = slice(s * sub, (s + 1) * sub)
        xn = _rms(x_ref[rs, :], gpre_ref[...]).astype(BF16)
        for c in range(F // fc):
            cs = slice(c * fc, (c + 1) * fc)
            h1 = _dot(xn, w1_ref[:, cs])
            h3 = _dot(xn, w3_ref[:, cs])
            g_scr[s, :, cs] = ((h1 * jax.nn.sigmoid(h1)) * h3).astype(BF16)
        h = _dot(g_scr[s], w2_ref[...])
        o_ref[rs, :] = x_ref[rs, :] + 0.5 * _rms(h, gpost_ref[...])


def _resident(shape):
    return pl.BlockSpec(shape, lambda i: (0,) * len(shape), pipeline_mode=pl.Buffered(1))


def _ffn(x, gpre, gpost, weights, casts=(), tm=1024, sub=512, fc=256):
    L, D = x.shape
    w1, w3, w2 = weights
    F = w1.shape[-1]
    n_steps = L // tm
    in_specs = [pl.BlockSpec((tm, D), lambda i: (i, 0)), _resident((1, D)), _resident((1, D)),
                _resident((D, F)), _resident((D, F)), _resident((F, D))]
    out_specs = [pl.BlockSpec((tm, D), lambda i: (i, 0))]
    out_shape = [jax.ShapeDtypeStruct((L, D), F32)]
    operands = [x, _row(gpre), _row(gpost), w1, w3, w2]
    for w, lead in casts:
        rows, cols = w.shape[-2] // n_steps, w.shape[-1]
        in_specs.append(pl.BlockSpec((None,) * len(lead) + (rows, cols), lambda i, lead=lead: lead + (i, 0)))
        out_specs.append(pl.BlockSpec((rows, cols), lambda i: (i, 0)))
        out_shape.append(jax.ShapeDtypeStruct(w.shape[-2:], BF16))
        operands.append(w)
    outs = pl.pallas_call(
        functools.partial(_ffn_body, fc=fc),
        grid=(n_steps,),
        in_specs=in_specs,
        out_specs=out_specs,
        out_shape=out_shape,
        scratch_shapes=[pltpu.VMEM((tm // sub, sub, F), BF16)],
        compiler_params=_params("arbitrary"),
        name="ffn",
    )(*operands)
    return outs[0], list(outs[1:])


def _s5_body(x_ref, gpre_ref, gpost_ref, win_ref, are_ref, aim_ref, ldt_ref, bdre_ref, bdim_ref,
             cdre_ref, cdim_ref, d_ref, wglu_ref, bglu_ref, wout_ref, o_ref,
             abar_re, abar_im, bb_re, bb_im, st_re, st_im, bu_re, bu_im, u_scr, y_scr):
    tt = x_ref.shape[0]
    n_tiles, uw, sw = bdre_ref.shape
    nj = sw // LANES

    @pl.when(pl.program_id(0) == 0)
    def _():
        ar = are_ref[...]
        ai = aim_ref[...]
        dt = jnp.exp(ldt_ref[...])
        mag = jnp.exp(dt * ar)
        abr = mag * jnp.cos(dt * ai)
        abi = mag * jnp.sin(dt * ai)
        den = ar * ar + ai * ai
        nr = abr - 1.0
        fre = (nr * ar + abi * ai) / den
        fim = (abi * ar - nr * ai) / den
        abar_re[...] = abr
        abar_im[...] = abi
        for k in range(n_tiles):
            fr = fre[k:k + 1, :]
            fi = fim[k:k + 1, :]
            bb_re[k] = (fr * bdre_ref[k] - fi * bdim_ref[k]).astype(BF16)
            bb_im[k] = (fr * bdim_ref[k] + fi * bdre_ref[k]).astype(BF16)
        st_re[...] = jnp.zeros_like(st_re)
        st_im[...] = jnp.zeros_like(st_im)

    n_sub = bu_re.shape[0]
    sub = tt // n_sub
    for s in range(n_sub):
        rs = slice(s * sub, (s + 1) * sub)
        xn = _rms(x_ref[rs, :], gpre_ref[...]).astype(BF16)
        u = _dot(xn, win_ref[...])
        u_scr[rs, :] = u
        ub = u.astype(BF16)
        for k in range(n_tiles):
            uk = ub[:, k * uw:(k + 1) * uw]
            br = _dot(uk, bb_re[k])
            bi = _dot(uk, bb_im[k])
            for j in range(nj):
                bu_re[s, j, pl.ds(k, sub, stride=n_tiles), :] = br[:, j * LANES:(j + 1) * LANES]
                bu_im[s, j, pl.ds(k, sub, stride=n_tiles), :] = bi[:, j * LANES:(j + 1) * LANES]

    a_r = [abar_re[:, j * LANES:(j + 1) * LANES] for j in range(nj)]
    a_i = [abar_im[:, j * LANES:(j + 1) * LANES] for j in range(nj)]
    state = [(st_re[j], st_im[j]) for j in range(nj)]
    for s in range(n_sub):
        for t in range(sub):
            rows = slice(t * n_tiles, (t + 1) * n_tiles)
            for j in range(nj):
                sr, si = state[j]
                nr = a_r[j] * sr - a_i[j] * si + bu_re[s, j, rows, :]
                ni = a_r[j] * si + a_i[j] * sr + bu_im[s, j, rows, :]
                bu_re[s, j, rows, :] = nr
                bu_im[s, j, rows, :] = ni
                state[j] = (nr, ni)
    for j in range(nj):
        st_re[j], st_im[j] = state[j]

    for s in range(n_sub):
        rs = slice(s * sub, (s + 1) * sub)
        for k in range(n_tiles):
            sre = jnp.concatenate([bu_re[s, j, pl.ds(k, sub, stride=n_tiles), :] for j in range(nj)], axis=-1)
            sim = jnp.concatenate([bu_im[s, j, pl.ds(k, sub, stride=n_tiles), :] for j in range(nj)], axis=-1)
            y_scr[rs, k * uw:(k + 1) * uw] = (_dot(sre.astype(BF16), cdre_ref[k])
                                              - _dot(sim.astype(BF16), cdim_ref[k]))
        y = jax.nn.gelu(y_scr[rs, :] + d_ref[...] * u_scr[rs, :])
        z = y * jax.nn.sigmoid(_dot(y.astype(BF16), wglu_ref[...]) + bglu_ref[...])
        h = _dot(z.astype(BF16), wout_ref[...])
        o_ref[rs, :] = x_ref[rs, :] + _rms(h, gpost_ref[...])


def _s5_block_diag(b, c):
    G, P, HG = b.shape
    gt = S5_GROUPS_PER_TILE
    eye = jnp.eye(gt, dtype=b.dtype)
    bt = b.reshape(G // gt, gt, P, HG).transpose(0, 1, 3, 2)
    bd = jnp.einsum('kghp,gj->kghjp', bt, eye).reshape(G // gt, gt * HG, gt * P)
    ct = c.reshape(G // gt, gt, HG, P)
    cd = jnp.einsum('kghp,gj->kjpgh', ct, eye).reshape(G // gt, gt * P, gt * HG)
    return bd, cd


def _s5_mixer(x, gpre, gpost, w_in, a_re, a_im, log_dt, b_re, b_im, c_re, c_im, d_skip, w_glu, b_glu,
              w_out, tt=512, sub=512):
    L, D = x.shape
    G, P = a_re.shape
    bd_re, cd_re = _s5_block_diag(b_re, c_re)
    bd_im, cd_im = _s5_block_diag(b_im, c_im)
    n_tiles, uw, sw = bd_re.shape
    assert n_tiles == SUBLANES and uw == LANES and sw % LANES == 0
    nj = sw // LANES
    per_tile = lambda v: v.reshape(n_tiles, sw)
    return pl.pallas_call(
        _s5_body,
        grid=(L // tt,),
        in_specs=[
            pl.BlockSpec((tt, D), lambda i: (i, 0)),
            _resident((1, D)), _resident((1, D)), _resident((D, D)),
            _resident((n_tiles, sw)), _resident((n_tiles, sw)), _resident((n_tiles, sw)),
            _resident((n_tiles, uw, sw)), _resident((n_tiles, uw, sw)),
            _resident((n_tiles, sw, uw)), _resident((n_tiles, sw, uw)),
            _resident((1, D)), _resident((D, D)), _resident((1, D)), _resident((D, D)),
        ],
        out_specs=pl.BlockSpec((tt, D), lambda i: (i, 0)),
        out_shape=jax.ShapeDtypeStruct((L, D), F32),
        scratch_shapes=[
            pltpu.VMEM((n_tiles, sw), F32), pltpu.VMEM((n_tiles, sw), F32),
            pltpu.VMEM((n_tiles, uw, sw), BF16), pltpu.VMEM((n_tiles, uw, sw), BF16),
            pltpu.VMEM((nj, n_tiles, LANES), F32), pltpu.VMEM((nj, n_tiles, LANES), F32),
            pltpu.VMEM((tt // sub, nj, sub * n_tiles, LANES), F32),
            pltpu.VMEM((tt // sub, nj, sub * n_tiles, LANES), F32),
            pltpu.VMEM((tt, D), F32), pltpu.VMEM((tt, D), F32),
        ],
        compiler_params=_params("arbitrary"),
        name="s5_mixer",
    )(x, _row(gpre), _row(gpost), w_in.astype(BF16),
      per_tile(a_re), per_tile(a_im), per_tile(jnp.broadcast_to(log_dt[:, None], (G, P))),
      bd_re, bd_im, cd_re.astype(BF16), cd_im.astype(BF16),
      _row(d_skip), w_glu.astype(BF16), _row(b_glu), w_out.astype(BF16))


def _conv_body(x_ref, gpre_ref, gpost_ref, win_ref, bin_ref, dw_ref, dwb_ref,
               lng_ref, lnb_ref, wout_ref, bout_ref, o_ref, ext, conv_scr):
    tm, D = x_ref.shape
    halo = ext.shape[0] - tm
    n_taps = dw_ref.shape[0]

    @pl.when(pl.program_id(0) == 0)
    def _():
        ext[0:halo, :] = jnp.zeros((halo, D), F32)

    x = x_ref[...]
    xn = _rms(x, gpre_ref[...]).astype(BF16)
    a = _dot(xn, win_ref[:, :D]) + bin_ref[:, :D]
    g = _dot(xn, win_ref[:, D:]) + bin_ref[:, D:]
    ext[halo:halo + tm, :] = a * jax.nn.sigmoid(g)

    base = halo - (n_taps - 1)
    for c in range(D // LANES):
        cs = slice(c * LANES, (c + 1) * LANES)
        acc = None
        for s in range(SUBLANES):
            rows = tm if s == 0 else tm + SUBLANES
            q = None
            for a in range((base + n_taps - 1) // SUBLANES + 1):
                k = SUBLANES * a + s - base
                if 0 <= k < n_taps:
                    term = dw_ref[k:k + 1, cs] * ext[SUBLANES * a:SUBLANES * a + rows, cs]
                    q = term if q is None else q + term
            part = q if s == 0 else q[s:s + tm]
            acc = part if acc is None else acc + part
        conv_scr[:, cs] = acc
    ext[0:halo, :] = ext[tm:tm + halo, :]

    y = _layernorm(conv_scr[...] + dwb_ref[...], lng_ref[...], lnb_ref[...])
    y = y * jax.nn.sigmoid(y)
    h = _dot(y.astype(BF16), wout_ref[...]) + bout_ref[...]
    o_ref[...] = x + _rms(h, gpost_ref[...])


def _conv_mixer(x, gpre, gpost, w_in, b_in, dw, dw_b, ln_g, ln_b, w_out, b_out, tm=512):
    L, D = x.shape
    return pl.pallas_call(
        _conv_body,
        grid=(L // tm,),
        in_specs=[
            pl.BlockSpec((tm, D), lambda i: (i, 0)),
            _resident((1, D)), _resident((1, D)), _resident((D, 2 * D)), _resident((1, 2 * D)),
            _resident(dw.shape), _resident((1, D)), _resident((1, D)), _resident((1, D)),
            _resident((D, D)), _resident((1, D)),
        ],
        out_specs=pl.BlockSpec((tm, D), lambda i: (i, 0)),
        out_shape=jax.ShapeDtypeStruct((L, D), F32),
        scratch_shapes=[pltpu.VMEM((tm + CONV_HALO, D), F32), pltpu.VMEM((tm, D), F32)],
        compiler_params=_params("arbitrary"),
        name="conv_mixer",
    )(x, _row(gpre), _row(gpost), w_in.astype(BF16), _row(b_in), dw, _row(dw_b), _row(ln_g), _row(ln_b),
      w_out.astype(BF16), _row(b_out))


def _gmlp_body(x_ref, gpre_ref, gpost_ref, win_ref, bin_ref, lng_ref, lnb_ref, ws_ref, bs_ref,
               wout_ref, bout_ref, o_ref, z_scr, su_scr, *, fc):
    n_sub, sub, _ = su_scr.shape
    E = lng_ref.shape[1]
    n_heads, chunk, _ = ws_ref.shape
    hw = E // n_heads
    causal = (lax.broadcasted_iota(jnp.int32, (chunk, chunk), 1)
              <= lax.broadcasted_iota(jnp.int32, (chunk, chunk), 0))
    wm = [jnp.where(causal, ws_ref[h], 0.0).astype(BF16) for h in range(n_heads)]
    for i in range(n_sub):
        rows = slice(i * sub, (i + 1) * sub)
        x = x_ref[rows, :]
        xn = _rms(x, gpre_ref[...]).astype(BF16)
        for c in range(2 * E // fc):
            cs = slice(c * fc, (c + 1) * fc)
            z_scr[i, :, cs] = jax.nn.gelu(_dot(xn, win_ref[:, cs]) + bin_ref[:, cs])
        vb = _layernorm(z_scr[i, :, E:], lng_ref[...], lnb_ref[...]).astype(BF16)
        for h in range(n_heads):
            bias = bs_ref[:, h:h + 1]
            hs = slice(h * hw, (h + 1) * hw)
            for c in range(sub // chunk):
                rs = slice(c * chunk, (c + 1) * chunk)
                s = _dot(wm[h], vb[rs, hs]) + bias
                su_scr[i, rs, hs] = (z_scr[i, rs, hs] * s).astype(BF16)
        hout = _dot(su_scr[i], wout_ref[...]) + bout_ref[...]
        o_ref[rows, :] = x + _rms(hout, gpost_ref[...])


def _gmlp_mixer(x, gpre, gpost, w_in, b_in, ln_g, ln_b, w_s, b_s, w_out, b_out, tm=512, sub=512, fc=512):
    L, D = x.shape
    E = ln_g.shape[0]
    return pl.pallas_call(
        functools.partial(_gmlp_body, fc=fc),
        grid=(L // tm,),
        in_specs=[
            pl.BlockSpec((tm, D), lambda i: (i, 0)),
            _resident((1, D)), _resident((1, D)), _resident((D, 2 * E)), _resident((1, 2 * E)),
            _resident((1, E)), _resident((1, E)),
            _resident(w_s.shape), _resident((GM_CHUNK, GM_HEADS)),
            _resident((E, D)), _resident((1, D)),
        ],
        out_specs=pl.BlockSpec((tm, D), lambda i: (i, 0)),
        out_shape=jax.ShapeDtypeStruct((L, D), F32),
        scratch_shapes=[pltpu.VMEM((tm // sub, sub, 2 * E), F32), pltpu.VMEM((tm // sub, sub, E), BF16)],
        compiler_params=_params("parallel"),
        name="gmlp_mixer",
    )(x, _row(gpre), _row(gpost), w_in.astype(BF16), _row(b_in), _row(ln_g), _row(ln_b),
      w_s, b_s.T, w_out.astype(BF16), _row(b_out))


def _regroup_body(x_ref, g_ref, *refs):
    *o_refs, xt_scr = refs
    xn = _rms(x_ref[...], g_ref[...])
    for c in range(xt_scr.shape[0]):
        xt_scr[c] = xn[:, c * LANES:(c + 1) * LANES]
    for o_ref in o_refs:
        d, rows, _ = o_ref.shape
        if d == 1:
            o_ref[0] = xn.astype(o_ref.dtype)
            continue
        for r in range(d):
            o_ref[r] = jnp.concatenate([xt_scr[c, pl.ds(r, rows, stride=d), :] for c in range(xt_scr.shape[0])],
                                       axis=-1).astype(o_ref.dtype)


def _regroup_rows(x, g, dilations, tm=512):
    L, D = x.shape
    return pl.pallas_call(
        _regroup_body,
        grid=(L // tm,),
        in_specs=[pl.BlockSpec((tm, D), lambda i: (i, 0)), _resident((1, D))],
        out_specs=[pl.BlockSpec((d, tm // d, D), lambda i: (0, i, 0)) for d in dilations],
        out_shape=[jax.ShapeDtypeStruct((d, L // d, D), BF16) for d in dilations],
        scratch_shapes=[pltpu.VMEM((D // LANES, tm, LANES), F32)],
        compiler_params=_params("parallel"),
        name="regroup_rows",
    )(x, _row(g))


def _t5_buckets(dilation):
    delta = (np.arange(BLOCK)[:, None] + BLOCK) - np.arange(2 * BLOCK)[None, :]
    dist = np.maximum(delta, 0) * dilation
    max_exact = NUM_BUCKETS // 2
    distf = np.maximum(dist, 1).astype(np.float32)
    large = max_exact + (np.log(distf / np.float32(max_exact)) / np.float32(math.log(MAX_DISTANCE / max_exact))
                         * np.float32(NUM_BUCKETS - max_exact)).astype(np.int32)
    large = np.minimum(large, NUM_BUCKETS - 1)
    return np.where(dist < max_exact, dist, large).astype(np.int32)


def _attn_body(tab_ref, bkt_ref, xc_ref, xn_ref, w_ref, o_ref, lse_ref,
               bias_scr, q_cur, k_win, v_win, q_nxt, k_nxt, v_nxt, s_scr, p_scr, *, scale):
    blk = bias_scr.shape[2]
    n_heads = bias_scr.shape[1]
    D = q_cur.shape[1]
    hd = D // n_heads
    qb = q_cur.shape[0] // blk
    n = pl.program_id(1)

    def project(x_ref, q_dst, k_dst, v_dst, k_off):
        xb = x_ref[...]
        q_dst[...] = (_dot(xb, w_ref[:, 0:D]) * scale).astype(BF16)
        k_dst[k_off:k_off + qb * blk, :] = _dot(xb, w_ref[:, D:2 * D]).astype(BF16)
        v_dst[k_off:k_off + qb * blk, :] = _dot(xb, w_ref[:, 2 * D:3 * D]).astype(BF16)

    @pl.when(n == 0)
    def _():
        k_win[0:blk, :] = jnp.zeros((blk, D), BF16)
        v_win[0:blk, :] = jnp.zeros((blk, D), BF16)

    @pl.when((pl.program_id(0) == 0) & (n == 0))
    def _():
        project(xc_ref, q_cur, k_win, v_win, blk)

    @pl.when((pl.program_id(0) == 0) & (n == 0))
    def _():
        bkt = bkt_ref[...]
        qi = lax.broadcasted_iota(jnp.int32, bkt.shape, 0)
        ki = lax.broadcasted_iota(jnp.int32, bkt.shape, 1)
        delta = qi + blk - ki
        in_band = (delta >= 0) & (delta <= blk)
        for h in range(n_heads):
            acc = jnp.zeros(bkt.shape, F32)
            for b in range(tab_ref.shape[0]):
                acc = jnp.where(bkt == b, tab_ref[b, h], acc)
            acc = jnp.where(in_band, acc, NEG_BIG)
            bias_scr[1, h] = acc
            bias_scr[0, h] = jnp.where(ki >= blk, acc, NEG_BIG)

    dims = (((1,), (1,)), ((), ()))
    per_tile = LANES // hd
    lane = lax.broadcasted_iota(jnp.int32, (1, LANES), 1)
    own = [(lane >= i * hd) & (lane < (i + 1) * hd) for i in range(per_tile)]
    n_tiles = n_heads // per_tile

    pending = [(dst, which * D + c * MXU_WIDTH, c) for which, dst in enumerate((q_nxt, k_nxt, v_nxt))
               for c in range(D // MXU_WIDTH)]
    n_chunks = len(pending)
    n_slots = qb * n_heads

    def project_chunks(slot_idx):
        due = n_chunks * (slot_idx + 1) // n_slots
        while n_chunks - len(pending) < due:
            dst, col, c = pending.pop(0)
            res = _dot(xn_ref[...], w_ref[:, col:col + MXU_WIDTH])
            if dst is q_nxt:
                res = res * scale
            dst[:, c * MXU_WIDTH:(c + 1) * MXU_WIDTH] = res.astype(BF16)

    for b in range(qb):
        rows = slice(b * blk, (b + 1) * blk)
        keys = slice(b * blk, (b + 2) * blk)
        slot = jnp.minimum(n, 1) if b == 0 else 1
        for t in range(n_tiles):
            cs = slice(t * LANES, (t + 1) * LANES)
            qt = q_cur[rows, cs]
            kt = k_win[keys, cs]
            for i in range(per_tile):
                h = t * per_tile + i
                qh = jnp.where(own[i], qt, jnp.zeros_like(qt))
                s_scr[b, h] = lax.dot_general(qh, kt, dims, preferred_element_type=F32) + bias_scr[slot, h]

        lse_b = jnp.zeros((blk, LANES), F32)
        for h in range(n_heads):
            s = s_scr[b, h]
            m = jnp.max(s, axis=-1, keepdims=True)
            e = jnp.exp(s - m)
            den = jnp.sum(e, axis=-1, keepdims=True)
            p_scr[b, h] = (e * (1.0 / den)).astype(BF16)
            lse_b = jnp.where(lane == h, m + jnp.log(den), lse_b)
            project_chunks(b * n_heads + h)
        lse_ref[rows, :] = lse_b

        for t in range(n_tiles):
            cs = slice(t * LANES, (t + 1) * LANES)
            vt = v_win[keys, cs]
            acc = jnp.zeros((blk, LANES), F32)
            for i in range(per_tile):
                acc = acc + _dot(p_scr[b, t * per_tile + i], jnp.where(own[i], vt, jnp.zeros_like(vt)))
            o_ref[rows, cs] = acc

    k_win[0:blk, :] = k_win[qb * blk:(qb + 1) * blk, :]
    v_win[0:blk, :] = v_win[qb * blk:(qb + 1) * blk, :]
    k_win[blk:(qb + 1) * blk, :] = k_nxt[...]
    v_win[blk:(qb + 1) * blk, :] = v_nxt[...]
    q_cur[...] = q_nxt[...]


def _dilated_attention(xn, w_qkv, pattern, table, dilation, qb=2):
    d, rows, D = xn.shape
    n_heads = D // HEAD_DIM
    tq = qb * BLOCK
    n_steps = rows // tq
    out = lambda w: pl.BlockSpec((None, tq, w), lambda r, n: (r, n, 0))

    def following(r, n):
        nxt = jnp.minimum(r * n_steps + n + 1, d * n_steps - 1)
        return nxt // n_steps, nxt % n_steps, 0

    return pl.pallas_call(
        functools.partial(_attn_body, scale=HEAD_DIM ** -0.5),
        grid=(d, n_steps),
        in_specs=[
            pl.BlockSpec(memory_space=pltpu.SMEM),
            pl.BlockSpec((BLOCK, 2 * BLOCK), lambda r, n: (0, 0), pipeline_mode=pl.Buffered(1)),
            pl.BlockSpec((None, tq, D), lambda r, n: (0, 0, 0)),
            pl.BlockSpec((None, tq, D), following),
            pl.BlockSpec((D, 3 * D), lambda r, n: (0, pattern), pipeline_mode=pl.Buffered(1)),
        ],
        out_specs=[out(D), out(LANES)],
        out_shape=[jax.ShapeDtypeStruct((d, rows, D), F32), jax.ShapeDtypeStruct((d, rows, LANES), F32)],
        scratch_shapes=[pltpu.VMEM((2, n_heads, BLOCK, 2 * BLOCK), F32),
                        pltpu.VMEM((tq, D), BF16),
                        pltpu.VMEM((tq + BLOCK, D), BF16), pltpu.VMEM((tq + BLOCK, D), BF16),
                        pltpu.VMEM((tq, D), BF16), pltpu.VMEM((tq, D), BF16), pltpu.VMEM((tq, D), BF16),
                        pltpu.VMEM((qb, n_heads, BLOCK, 2 * BLOCK), F32),
                        pltpu.VMEM((qb, n_heads, BLOCK, 2 * BLOCK), BF16)],
        compiler_params=_params("arbitrary", "arbitrary"),
        name=f"dilated_attn_d{dilation}",
    )(table, jnp.asarray(_t5_buckets(dilation)), xn, xn, w_qkv)


def _attn_out_body(x_ref, gpost_ref, o0_ref, o1_ref, o2_ref, l0_ref, l1_ref, l2_ref, wout_ref, out_ref, *scr):
    def natural(ref, s):
        d, rows, width = ref.shape
        if d == 1:
            return ref[0]
        n_lt = width // LANES
        for c in range(n_lt):
            for r in range(d):
                s[c, pl.ds(r, rows, stride=d), :] = ref[r, :, c * LANES:(c + 1) * LANES]
        return jnp.concatenate([s[c] for c in range(n_lt)], axis=-1)

    D = x_ref.shape[1]
    n_heads = D // HEAD_DIM
    l0, l1, l2 = natural(l0_ref, None), natural(l1_ref, scr[0]), natural(l2_ref, scr[1])
    m = jnp.maximum(jnp.maximum(l0, l1), l2)
    e0, e1, e2 = jnp.exp(l0 - m), jnp.exp(l1 - m), jnp.exp(l2 - m)
    inv = 1.0 / (e0 + e1 + e2)
    lane = lax.broadcasted_iota(jnp.int32, (1, LANES), 1)
    packed = jnp.zeros(e0.shape, F32)
    for g, e in enumerate((e0, e1, e2)):
        w = jnp.where(lane < n_heads, e * inv, 0.0)
        hi = w.astype(BF16).astype(F32)
        lo = w - hi
        for part, piece in enumerate((hi, lo)):
            shift = (2 * g + part) * n_heads
            packed = packed + (pltpu.roll(piece, shift, 1) if shift else piece)
    src = lax.broadcasted_iota(jnp.int32, (LANES, 3 * D), 0)
    dst = lax.broadcasted_iota(jnp.int32, (LANES, 3 * D), 1)
    spread = ((src // (2 * n_heads) == dst // D) & (src % n_heads == (dst % D) // HEAD_DIM)
              & (src < 6 * n_heads)).astype(BF16)
    wts = _dot(packed.astype(BF16), spread)
    o = (wts[:, 0:D] * natural(o0_ref, None) + wts[:, D:2 * D] * natural(o1_ref, scr[2])
         + wts[:, 2 * D:3 * D] * natural(o2_ref, scr[3]))
    h = _dot(o.astype(BF16), wout_ref[...])
    out_ref[...] = x_ref[...] + _rms(h, gpost_ref[...])


def _attention_mixer(x, gpre, gpost, w_qkv, w_out, rel_bias, tm=512):
    L, D = x.shape
    n_heads = D // HEAD_DIM
    outs, lses = [], []
    w_qkv = w_qkv.astype(BF16)
    xns = _regroup_rows(x, gpre, [d for _, d in PATTERNS])
    for g, (window, dilation) in enumerate(PATTERNS):
        assert window // dilation == BLOCK and L % (BLOCK * dilation) == 0
        o, lse = _dilated_attention(xns[g], w_qkv, g, rel_bias[:, g * n_heads:(g + 1) * n_heads], dilation)
        outs.append(o)
        lses.append(lse)
    tile = pl.BlockSpec((tm, D), lambda i: (i, 0))
    grouped = lambda w: [pl.BlockSpec((d, tm // d, w), lambda i: (0, i, 0)) for _, d in PATTERNS]
    return pl.pallas_call(
        _attn_out_body,
        grid=(L // tm,),
        in_specs=[tile, _resident((1, D))] + grouped(D) + grouped(LANES) + [_resident((D, D))],
        out_specs=tile,
        out_shape=jax.ShapeDtypeStruct((L, D), F32),
        scratch_shapes=[pltpu.VMEM((1, tm, LANES), F32)] * 2 + [pltpu.VMEM((D // LANES, tm, LANES), F32)] * 2,
        compiler_params=_params("parallel"),
        name="attn_out",
    )(x, _row(gpost), *outs, *lses, w_out.astype(BF16))


def kernel(x, norm_pre, norm_post, ffn_w1, ffn_w3, ffn_w2, rel_bias, s5_w_in, s5_a_re, s5_a_im, s5_log_dt, s5_b_re, s5_b_im, s5_c_re, s5_c_im, s5_d, s5_w_glu, s5_b_glu, s5_w_out, cv_w_in, cv_b_in, cv_dw, cv_dw_b, cv_ln_g, cv_ln_b, cv_w_out, cv_b_out, gm_w_in, gm_b_in, gm_ln_g, gm_ln_b, gm_w_s, gm_b_s, gm_w_out, gm_b_out, at_w_qkv, at_w_out):
    bsz, seq, d_model = x.shape
    assert bsz == 1, "sequence mixers carry state along the row axis of one sequence"
    depth = norm_pre.shape[0]
    n_mixers = 4
    h = x.reshape(seq, d_model)
    stacks = (ffn_w1, ffn_w3, ffn_w2)
    mixer_mats = ((s5_w_in, s5_w_glu, s5_w_out), (cv_w_in, cv_w_out), (gm_w_in, gm_w_out), (at_w_qkv, at_w_out))
    weights = tuple(w[0, 0].astype(BF16) for w in stacks)
    for i in range(depth):
        kind, j = i % n_mixers, i // n_mixers
        casts = [(w, (i, 1)) for w in stacks] + [(w, (j,)) for w in mixer_mats[kind]]
        h, cast = _ffn(h, norm_pre[i, 0], norm_post[i, 0], weights, casts)
        weights, mats = tuple(cast[:3]), cast[3:]
        if kind == 0:
            h = _s5_mixer(h, norm_pre[i, 1], norm_post[i, 1], mats[0], s5_a_re[j], s5_a_im[j], s5_log_dt[j],
                          s5_b_re[j], s5_b_im[j], s5_c_re[j], s5_c_im[j], s5_d[j], mats[1], s5_b_glu[j], mats[2])
        elif kind == 1:
            h = _conv_mixer(h, norm_pre[i, 1], norm_post[i, 1], mats[0], cv_b_in[j], cv_dw[j], cv_dw_b[j],
                            cv_ln_g[j], cv_ln_b[j], mats[1], cv_b_out[j])
        elif kind == 2:
            h = _gmlp_mixer(h, norm_pre[i, 1], norm_post[i, 1], mats[0], gm_b_in[j], gm_ln_g[j], gm_ln_b[j],
                            gm_w_s[j], gm_b_s[j], mats[1], gm_b_out[j])
        else:
            h = _attention_mixer(h, norm_pre[i, 1], norm_post[i, 1], mats[0], mats[1], rel_bias)
        casts = [(w, (i + 1, 0)) for w in stacks] if i + 1 < depth else []
        h, cast = _ffn(h, norm_pre[i, 2], norm_post[i, 2], weights, casts)
        weights = tuple(cast)
    return h.reshape(bsz, seq, d_model)
```

```python
import functools
import math

import numpy as np
import jax
import jax.numpy as jnp
from jax import lax
from jax.experimental import pallas as pl
from jax.experimental.pallas import tpu as pltpu

F32 = jnp.float32
BF16 = jnp.bfloat16

EPS = 1e-6
LANES = 128
SUBLANES = 8
MXU_WIDTH = 256
S5_GROUP = 16
S5_STATE = 64
S5_GROUPS_PER_TILE = 8
CONV_W = 31
CONV_HALO = 32
GM_CHUNK = 128
GM_HEADS = 8
HEAD_DIM = 64
PATTERNS = ((128, 1), (512, 4), (2048, 16))
BLOCK = 128
NUM_BUCKETS = 32
MAX_DISTANCE = 2048
NEG_BIG = -1e30

VMEM_LIMIT_BYTES = 56 * 1024 * 1024


def _params(*sem):
    return pltpu.CompilerParams(dimension_semantics=sem, vmem_limit_bytes=VMEM_LIMIT_BYTES)


def _rms(x, g):
    return x * lax.rsqrt(jnp.mean(x * x, axis=-1, keepdims=True) + EPS) * g


def _layernorm(x, g, b):
    mu = jnp.mean(x, axis=-1, keepdims=True)
    xc = x - mu
    var = jnp.mean(xc * xc, axis=-1, keepdims=True)
    return xc * lax.rsqrt(var + EPS) * g + b


def _dot(a, b):
    return jnp.dot(a, b, preferred_element_type=F32)


def _row(v):
    return v.reshape(1, -1)


def _ffn_body(x_ref, gpre_ref, gpost_ref, w1_ref, w3_ref, w2_ref, *refs, fc):
    n_cast = (len(refs) - 2) // 2
    src_refs, o_ref, dst_refs, g_scr = refs[:n_cast], refs[n_cast], refs[n_cast + 1:-1], refs[-1]
    for src, dst in zip(src_refs, dst_refs):
        dst[...] = src[...].astype(dst.dtype)
    n_sub, sub, F = g_scr.shape
    for s in range(n_sub):
        rs = slice(s * sub, (s + 1) * sub)
        xn = _rms(x_ref[rs, :], gpre_ref[...]).astype(BF16)
        for c in range(F // fc):
            cs = slice(c * fc, (c + 1) * fc)
            h1 = _dot(xn, w1_ref[:, cs])
            h3 = _dot(xn, w3_ref[:, cs])
            g_scr[s, :, cs] = ((h1 * jax.nn.sigmoid(h1)) * h3).astype(BF16)
        h = _dot(g_scr[s], w2_ref[...])
        o_ref[rs, :] = x_ref[rs, :] + 0.5 * _rms(h, gpost_ref[...])


def _resident(shape):
    return pl.BlockSpec(shape, lambda i: (0,) * len(shape), pipeline_mode=pl.Buffered(1))


def _ffn(x, gpre, gpost, weights, casts=(), tm=1024, sub=512, fc=256):
    L, D = x.shape
    w1, w3, w2 = weights
    F = w1.shape[-1]
    n_steps = L // tm
    in_specs = [pl.BlockSpec((tm, D), lambda i: (i, 0)), _resident((1, D)), _resident((1, D)),
                _resident((D, F)), _resident((D, F)), _resident((F, D))]
    out_specs = [pl.BlockSpec((tm, D), lambda i: (i, 0))]
    out_shape = [jax.ShapeDtypeStruct((L, D), F32)]
    operands = [x, _row(gpre), _row(gpost), w1, w3, w2]
    for w, lead in casts:
        rows, cols = w.shape[-2] // n_steps, w.shape[-1]
        in_specs.append(pl.BlockSpec((None,) * len(lead) + (rows, cols), lambda i, lead=lead: lead + (i, 0)))
        out_specs.append(pl.BlockSpec((rows, cols), lambda i: (i, 0)))
        out_shape.append(jax.ShapeDtypeStruct(w.shape[-2:], BF16))
        operands.append(w)
    outs = pl.pallas_call(
        functools.partial(_ffn_body, fc=fc),
        grid=(n_steps,),
        in_specs=in_specs,
        out_specs=out_specs,
        out_shape=out_shape,
        scratch_shapes=[pltpu.VMEM((tm // sub, sub, F), BF16)],
        compiler_params=_params("arbitrary"),
        name="ffn",
    )(*operands)
    return outs[0], list(outs[1:])


def _s5_body(x_ref, gpre_ref, gpost_ref, win_ref, are_ref, aim_ref, ldt_ref, bdre_ref, bdim_ref,
             cdre_ref, cdim_ref, d_ref, wglu_ref, bglu_ref, wout_ref, o_ref,
             abar_re, abar_im, bb_re, bb_im, st_re, st_im, bu_re, bu_im, u_scr, y_scr):
    tt = x_ref.shape[0]
    n_tiles, uw, sw = bdre_ref.shape
    nj = sw // LANES

    @pl.when(pl.program_id(0) == 0)
    def _():
        ar = are_ref[...]
        ai = aim_ref[...]
        dt = jnp.exp(ldt_ref[...])
        mag = jnp.exp(dt * ar)
        abr = mag * jnp.cos(dt * ai)
        abi = mag * jnp.sin(dt * ai)
        den = ar * ar + ai * ai
        nr = abr - 1.0
        fre = (nr * ar + abi * ai) / den
        fim = (abi * ar - nr * ai) / den
        abar_re[...] = abr
        abar_im[...] = abi
        for k in range(n_tiles):
            fr = fre[k:k + 1, :]
            fi = fim[k:k + 1, :]
            bb_re[k] = (fr * bdre_ref[k] - fi * bdim_ref[k]).astype(BF16)
            bb_im[k] = (fr * bdim_ref[k] + fi * bdre_ref[k]).astype(BF16)
        st_re[...] = jnp.zeros_like(st_re)
        st_im[...] = jnp.zeros_like(st_im)

    n_sub = bu_re.shape[0]
    sub = tt // n_sub
    for s in range(n_sub):
        rs = slice(s * sub, (s + 1) * sub)
        xn = _rms(x_ref[rs, :], gpre_ref[...]).astype(BF16)
        u = _dot(xn, win_ref[...])
        u_scr[rs, :] = u
        ub = u.astype(BF16)
        for k in range(n_tiles):
            uk = ub[:, k * uw:(k + 1) * uw]
            br = _dot(uk, bb_re[k])
            bi = _dot(uk, bb_im[k])
            for j in range(nj):
                bu_re[s, j, pl.ds(k, sub, stride=n_tiles), :] = br[:, j * LANES:(j + 1) * LANES]
                bu_im[s, j, pl.ds(k, sub, stride=n_tiles), :] = bi[:, j * LANES:(j + 1) * LANES]

    a_r = [abar_re[:, j * LANES:(j + 1) * LANES] for j in range(nj)]
    a_i = [abar_im[:, j * LANES:(j + 1) * LANES] for j in range(nj)]
    state = [(st_re[j], st_im[j]) for j in range(nj)]
    for s in range(n_sub):
        for t in range(sub):
            rows = slice(t * n_tiles, (t + 1) * n_tiles)
            for j in range(nj):
                sr, si = state[j]
                nr = a_r[j] * sr - a_i[j] * si + bu_re[s, j, rows, :]
                ni = a_r[j] * si + a_i[j] * sr + bu_im[s, j, rows, :]
                bu_re[s, j, rows, :] = nr
                bu_im[s, j, rows, :] = ni
                state[j] = (nr, ni)
    for j in range(nj):
        st_re[j], st_im[j] = state[j]

    for s in range(n_sub):
        rs = slice(s * sub, (s + 1) * sub)
        for k in range(n_tiles):
            sre = jnp.concatenate([bu_re[s, j, pl.ds(k, sub, stride=n_tiles), :] for j in range(nj)], axis=-1)
            sim = jnp.concatenate([bu_im[s, j, pl.ds(k, sub, stride=n_tiles), :] for j in range(nj)], axis=-1)
            y_scr[rs, k * uw:(k + 1) * uw] = (_dot(sre.astype(BF16), cdre_ref[k])
                                              - _dot(sim.astype(BF16), cdim_ref[k]))
        y = jax.nn.gelu(y_scr[rs, :] + d_ref[...] * u_scr[rs, :])
        z = y * jax.nn.sigmoid(_dot(y.astype(BF16), wglu_ref[...]) + bglu_ref[...])
        h = _dot(z.astype(BF16), wout_ref[...])
        o_ref[rs, :] = x_ref[rs, :] + _rms(h, gpost_ref[...])


def _s5_block_diag(b, c):
    G, P, HG = b.shape
    gt = S5_GROUPS_PER_TILE
    eye = jnp.eye(gt, dtype=b.dtype)
    bt = b.reshape(G // gt, gt, P, HG).transpose(0, 1, 3, 2)
    bd = jnp.einsum('kghp,gj->kghjp', bt, eye).reshape(G // gt, gt * HG, gt * P)
    ct = c.reshape(G // gt, gt, HG, P)
    cd = jnp.einsum('kghp,gj->kjpgh', ct, eye).reshape(G // gt, gt * P, gt * HG)
    return bd, cd


def _s5_mixer(x, gpre, gpost, w_in, a_re, a_im, log_dt, b_re, b_im, c_re, c_im, d_skip, w_glu, b_glu,
              w_out, tt=512, sub=512):
    L, D = x.shape
    G, P = a_re.shape
    bd_re, cd_re = _s5_block_diag(b_re, c_re)
    bd_im, cd_im = _s5_block_diag(b_im, c_im)
    n_tiles, uw, sw = bd_re.shape
    assert n_tiles == SUBLANES and uw == LANES and sw % LANES == 0
    nj = sw // LANES
    per_tile = lambda v: v.reshape(n_tiles, sw)
    return pl.pallas_call(
        _s5_body,
        grid=(L // tt,),
        in_specs=[
            pl.BlockSpec((tt, D), lambda i: (i, 0)),
            _resident((1, D)), _resident((1, D)), _resident((D, D)),
            _resident((n_tiles, sw)), _resident((n_tiles, sw)), _resident((n_tiles, sw)),
            _resident((n_tiles, uw, sw)), _resident((n_tiles, uw, sw)),
            _resident((n_tiles, sw, uw)), _resident((n_tiles, sw, uw)),
            _resident((1, D)), _resident((D, D)), _resident((1, D)), _resident((D, D)),
        ],
        out_specs=pl.BlockSpec((tt, D), lambda i: (i, 0)),
        out_shape=jax.ShapeDtypeStruct((L, D), F32),
        scratch_shapes=[
            pltpu.VMEM((n_tiles, sw), F32), pltpu.VMEM((n_tiles, sw), F32),
            pltpu.VMEM((n_tiles, uw, sw), BF16), pltpu.VMEM((n_tiles, uw, sw), BF16),
            pltpu.VMEM((nj, n_tiles, LANES), F32), pltpu.VMEM((nj, n_tiles, LANES), F32),
            pltpu.VMEM((tt // sub, nj, sub * n_tiles, LANES), F32),
            pltpu.VMEM((tt // sub, nj, sub * n_tiles, LANES), F32),
            pltpu.VMEM((tt, D), F32), pltpu.VMEM((tt, D), F32),
        ],
        compiler_params=_params("arbitrary"),
        name="s5_mixer",
    )(x, _row(gpre), _row(gpost), w_in.astype(BF16),
      per_tile(a_re), per_tile(a_im), per_tile(jnp.broadcast_to(log_dt[:, None], (G, P))),
      bd_re, bd_im, cd_re.astype(BF16), cd_im.astype(BF16),
      _row(d_skip), w_glu.astype(BF16), _row(b_glu), w_out.astype(BF16))


def _conv_body(x_ref, gpre_ref, gpost_ref, win_ref, bin_ref, dw_ref, dwb_ref,
               lng_ref, lnb_ref, wout_ref, bout_ref, o_ref, ext, conv_scr):
    tm, D = x_ref.shape
    halo = ext.shape[0] - tm
    n_taps = dw_ref.shape[0]

    @pl.when(pl.program_id(0) == 0)
    def _():
        ext[0:halo, :] = jnp.zeros((halo, D), F32)

    x = x_ref[...]
    xn = _rms(x, gpre_ref[...]).astype(BF16)
    a = _dot(xn, win_ref[:, :D]) + bin_ref[:, :D]
    g = _dot(xn, win_ref[:, D:]) + bin_ref[:, D:]
    ext[halo:halo + tm, :] = a * jax.nn.sigmoid(g)

    base = halo - (n_taps - 1)
    for c in range(D // LANES):
        cs = slice(c * LANES, (c + 1) * LANES)
        acc = None
        for s in range(SUBLANES):
            rows = tm if s == 0 else tm + SUBLANES
            q = None
            for a in range((base + n_taps - 1) // SUBLANES + 1):
                k = SUBLANES * a + s - base
                if 0 <= k < n_taps:
                    term = dw_ref[k:k + 1, cs] * ext[SUBLANES * a:SUBLANES * a + rows, cs]
                    q = term if q is None else q + term
            part = q if s == 0 else q[s:s + tm]
            acc = part if acc is None else acc + part
        conv_scr[:, cs] = acc
    ext[0:halo, :] = ext[tm:tm + halo, :]

    y = _layernorm(conv_scr[...] + dwb_ref[...], lng_ref[...], lnb_ref[...])
    y = y * jax.nn.sigmoid(y)
    h = _dot(y.astype(BF16), wout_ref[...]) + bout_ref[...]
    o_ref[...] = x + _rms(h, gpost_ref[...])


def _conv_mixer(x, gpre, gpost, w_in, b_in, dw, dw_b, ln_g, ln_b, w_out, b_out, tm=512):
    L, D = x.shape
    return pl.pallas_call(
        _conv_body,
        grid=(L // tm,),
        in_specs=[
            pl.BlockSpec((tm, D), lambda i: (i, 0)),
            _resident((1, D)), _resident((1, D)), _resident((D, 2 * D)), _resident((1, 2 * D)),
            _resident(dw.shape), _resident((1, D)), _resident((1, D)), _resident((1, D)),
            _resident((D, D)), _resident((1, D)),
        ],
        out_specs=pl.BlockSpec((tm, D), lambda i: (i, 0)),
        out_shape=jax.ShapeDtypeStruct((L, D), F32),
        scratch_shapes=[pltpu.VMEM((tm + CONV_HALO, D), F32), pltpu.VMEM((tm, D), F32)],
        compiler_params=_params("arbitrary"),
        name="conv_mixer",
    )(x, _row(gpre), _row(gpost), w_in.astype(BF16), _row(b_in), dw, _row(dw_b), _row(ln_g), _row(ln_b),
      w_out.astype(BF16), _row(b_out))


def _interleave(a, b):
    out, ia, ib = [], 0, 0
    while ia < len(a) or ib < len(b):
        if ib >= len(b) or (ia < len(a) and ia * len(b) <= ib * len(a)):
            out.append(a[ia])
            ia += 1
        else:
            out.append(b[ib])
            ib += 1
    return out


def _conv_ffn_body(x_ref, cpre_ref, cpost_ref, win_ref, bin_ref, dw_ref, dwb_ref, lng_ref, lnb_ref,
                   wout_ref, bout_ref, fpre_ref, fpost_ref, w1_ref, w3_ref, w2_ref, o_ref,
                   ext, conv_scr, xmid, xprev, xn_c, xn_f, yb, hc_scr, g_scr, hf_scr, *, fc):
    tm, D = x_ref.shape
    halo = ext.shape[0] - tm
    n_taps = dw_ref.shape[0]
    F = g_scr.shape[1]
    col_chunks = [slice(c * MXU_WIDTH, (c + 1) * MXU_WIDTH) for c in range(D // MXU_WIDTH)]

    @pl.when(pl.program_id(0) == 0)
    def _():
        ext[0:halo, :] = jnp.zeros((halo, D), F32)
        xmid[...] = jnp.zeros_like(xmid)

    def ffn_start():
        xprev[...] = xmid[...]
        xn_f[...] = _rms(xprev[...], fpre_ref[...]).astype(BF16)

    def ffn_hidden(cs):
        h1 = _dot(xn_f[...], w1_ref[:, cs])
        h3 = _dot(xn_f[...], w3_ref[:, cs])
        g_scr[:, cs] = ((h1 * jax.nn.sigmoid(h1)) * h3).astype(BF16)

    def ffn_down(cs):
        hf_scr[:, cs] = _dot(g_scr[...], w2_ref[:, cs])

    def ffn_finish():
        o_ref[...] = xprev[...] + 0.5 * _rms(hf_scr[...], fpost_ref[...])

    ffn = ([ffn_start] + [functools.partial(ffn_hidden, slice(c * fc, (c + 1) * fc)) for c in range(F // fc)]
           + [functools.partial(ffn_down, cs) for cs in col_chunks] + [ffn_finish])

    def conv_start():
        xn_c[...] = _rms(x_ref[...], cpre_ref[...]).astype(BF16)

    def conv_glu(cs):
        a = _dot(xn_c[...], win_ref[:, cs]) + bin_ref[:, cs]
        gs = slice(D + cs.start, D + cs.stop)
        g = _dot(xn_c[...], win_ref[:, gs]) + bin_ref[:, gs]
        ext[halo:halo + tm, cs] = a * jax.nn.sigmoid(g)

    base = halo - (n_taps - 1)

    def conv_taps(c):
        cs = slice(c * LANES, (c + 1) * LANES)
        acc = None
        for s in range(SUBLANES):
            rows = tm if s == 0 else tm + SUBLANES
            q = None
            for a in range((base + n_taps - 1) // SUBLANES + 1):
                k = SUBLANES * a + s - base
                if 0 <= k < n_taps:
                    term = dw_ref[k:k + 1, cs] * ext[SUBLANES * a:SUBLANES * a + rows, cs]
                    q = term if q is None else q + term
            part = q if s == 0 else q[s:s + tm]
            acc = part if acc is None else acc + part
        conv_scr[:, cs] = acc

    def conv_norm():
        ext[0:halo, :] = ext[tm:tm + halo, :]
        y = _layernorm(conv_scr[...] + dwb_ref[...], lng_ref[...], lnb_ref[...])
        yb[...] = (y * jax.nn.sigmoid(y)).astype(BF16)

    def conv_out(cs):
        hc_scr[:, cs] = _dot(yb[...], wout_ref[:, cs]) + bout_ref[:, cs]

    def conv_finish():
        xmid[...] = x_ref[...] + _rms(hc_scr[...], cpost_ref[...])

    conv = ([conv_start] + [functools.partial(conv_glu, cs) for cs in col_chunks]
            + [functools.partial(conv_taps, c) for c in range(D // LANES)] + [conv_norm]
            + [functools.partial(conv_out, cs) for cs in col_chunks] + [conv_finish])

    for piece in _interleave(ffn, conv):
        piece()


def _conv_mixer_ffn(x, cpre, cpost, w_in, b_in, dw, dw_b, ln_g, ln_b, w_out, b_out, fpre, fpost, weights,
                    tm=512, fc=256):
    L, D = x.shape
    w1, w3, w2 = weights
    F = w1.shape[-1]
    n = L // tm
    return pl.pallas_call(
        functools.partial(_conv_ffn_body, fc=fc),
        grid=(n + 1,),
        in_specs=[
            pl.BlockSpec((tm, D), lambda i: (jnp.minimum(i, n - 1), 0)),
            _resident((1, D)), _resident((1, D)), _resident((D, 2 * D)), _resident((1, 2 * D)),
            _resident(dw.shape), _resident((1, D)), _resident((1, D)), _resident((1, D)),
            _resident((D, D)), _resident((1, D)),
            _resident((1, D)), _resident((1, D)), _resident((D, F)), _resident((D, F)), _resident((F, D)),
        ],
        out_specs=pl.BlockSpec((tm, D), lambda i: (jnp.maximum(i - 1, 0), 0)),
        out_shape=jax.ShapeDtypeStruct((L, D), F32),
        scratch_shapes=[pltpu.VMEM((tm + CONV_HALO, D), F32), pltpu.VMEM((tm, D), F32),
                        pltpu.VMEM((tm, D), F32), pltpu.VMEM((tm, D), F32),
                        pltpu.VMEM((tm, D), BF16), pltpu.VMEM((tm, D), BF16), pltpu.VMEM((tm, D), BF16),
                        pltpu.VMEM((tm, D), F32), pltpu.VMEM((tm, F), BF16), pltpu.VMEM((tm, D), F32)],
        compiler_params=_params("arbitrary"),
        name="conv_mixer_ffn",
    )(x, _row(cpre), _row(cpost), w_in, _row(b_in), dw, _row(dw_b), _row(ln_g), _row(ln_b), w_out, _row(b_out),
      _row(fpre), _row(fpost), w1, w3, w2)


def _gmlp_body(x_ref, gpre_ref, gpost_ref, win_ref, bin_ref, lng_ref, lnb_ref, ws_ref, bs_ref,
               wout_ref, bout_ref, o_ref, z_scr, su_scr, *, fc):
    n_sub, sub, _ = su_scr.shape
    E = lng_ref.shape[1]
    n_heads, chunk, _ = ws_ref.shape
    hw = E // n_heads
    causal = (lax.broadcasted_iota(jnp.int32, (chunk, chunk), 1)
              <= lax.broadcasted_iota(jnp.int32, (chunk, chunk), 0))
    wm = [jnp.where(causal, ws_ref[h], 0.0).astype(BF16) for h in range(n_heads)]
    for i in range(n_sub):
        rows = slice(i * sub, (i + 1) * sub)
        x = x_ref[rows, :]
        xn = _rms(x, gpre_ref[...]).astype(BF16)
        for c in range(2 * E // fc):
            cs = slice(c * fc, (c + 1) * fc)
            z_scr[i, :, cs] = jax.nn.gelu(_dot(xn, win_ref[:, cs]) + bin_ref[:, cs])
        vb = _layernorm(z_scr[i, :, E:], lng_ref[...], lnb_ref[...]).astype(BF16)
        for h in range(n_heads):
            bias = bs_ref[:, h:h + 1]
            hs = slice(h * hw, (h + 1) * hw)
            for c in range(sub // chunk):
                rs = slice(c * chunk, (c + 1) * chunk)
                s = _dot(wm[h], vb[rs, hs]) + bias
                su_scr[i, rs, hs] = (z_scr[i, rs, hs] * s).astype(BF16)
        hout = _dot(su_scr[i], wout_ref[...]) + bout_ref[...]
        o_ref[rows, :] = x + _rms(hout, gpost_ref[...])


def _gmlp_mixer(x, gpre, gpost, w_in, b_in, ln_g, ln_b, w_s, b_s, w_out, b_out, tm=512, sub=512, fc=512):
    L, D = x.shape
    E = ln_g.shape[0]
    return pl.pallas_call(
        functools.partial(_gmlp_body, fc=fc),
        grid=(L // tm,),
        in_specs=[
            pl.BlockSpec((tm, D), lambda i: (i, 0)),
            _resident((1, D)), _resident((1, D)), _resident((D, 2 * E)), _resident((1, 2 * E)),
            _resident((1, E)), _resident((1, E)),
            _resident(w_s.shape), _resident((GM_CHUNK, GM_HEADS)),
            _resident((E, D)), _resident((1, D)),
        ],
        out_specs=pl.BlockSpec((tm, D), lambda i: (i, 0)),
        out_shape=jax.ShapeDtypeStruct((L, D), F32),
        scratch_shapes=[pltpu.VMEM((tm // sub, sub, 2 * E), F32), pltpu.VMEM((tm // sub, sub, E), BF16)],
        compiler_params=_params("parallel"),
        name="gmlp_mixer",
    )(x, _row(gpre), _row(gpost), w_in.astype(BF16), _row(b_in), _row(ln_g), _row(ln_b),
      w_s, b_s.T, w_out.astype(BF16), _row(b_out))


def _regroup_body(x_ref, g_ref, *refs):
    *o_refs, xt_scr = refs
    xn = _rms(x_ref[...], g_ref[...])
    for c in range(xt_scr.shape[0]):
        xt_scr[c] = xn[:, c * LANES:(c + 1) * LANES]
    for o_ref in o_refs:
        d, rows, _ = o_ref.shape
        if d == 1:
            o_ref[0] = xn.astype(o_ref.dtype)
            continue
        for r in range(d):
            o_ref[r] = jnp.concatenate([xt_scr[c, pl.ds(r, rows, stride=d), :] for c in range(xt_scr.shape[0])],
                                       axis=-1).astype(o_ref.dtype)


def _regroup_rows(x, g, dilations, tm=512):
    L, D = x.shape
    return pl.pallas_call(
        _regroup_body,
        grid=(L // tm,),
        in_specs=[pl.BlockSpec((tm, D), lambda i: (i, 0)), _resident((1, D))],
        out_specs=[pl.BlockSpec((d, tm // d, D), lambda i: (0, i, 0)) for d in dilations],
        out_shape=[jax.ShapeDtypeStruct((d, L // d, D), BF16) for d in dilations],
        scratch_shapes=[pltpu.VMEM((D // LANES, tm, LANES), F32)],
        compiler_params=_params("parallel"),
        name="regroup_rows",
    )(x, _row(g))


def _t5_buckets(dilation):
    delta = (np.arange(BLOCK)[:, None] + BLOCK) - np.arange(2 * BLOCK)[None, :]
    dist = np.maximum(delta, 0) * dilation
    max_exact = NUM_BUCKETS // 2
    distf = np.maximum(dist, 1).astype(np.float32)
    large = max_exact + (np.log(distf / np.float32(max_exact)) / np.float32(math.log(MAX_DISTANCE / max_exact))
                         * np.float32(NUM_BUCKETS - max_exact)).astype(np.int32)
    large = np.minimum(large, NUM_BUCKETS - 1)
    return np.where(dist < max_exact, dist, large).astype(np.int32)


def _attn_body(tab_ref, bkt_ref, xc_ref, xn_ref, w_ref, o_ref, lse_ref,
               bias_scr, q_cur, k_win, v_win, q_nxt, k_nxt, v_nxt, s_scr, p_scr, *, scale):
    blk = bias_scr.shape[2]
    n_heads = bias_scr.shape[1]
    D = q_cur.shape[1]
    hd = D // n_heads
    qb = q_cur.shape[0] // blk
    n = pl.program_id(1)

    def project(x_ref, q_dst, k_dst, v_dst, k_off):
        xb = x_ref[...]
        q_dst[...] = (_dot(xb, w_ref[:, 0:D]) * scale).astype(BF16)
        k_dst[k_off:k_off + qb * blk, :] = _dot(xb, w_ref[:, D:2 * D]).astype(BF16)
        v_dst[k_off:k_off + qb * blk, :] = _dot(xb, w_ref[:, 2 * D:3 * D]).astype(BF16)

    @pl.when(n == 0)
    def _():
        k_win[0:blk, :] = jnp.zeros((blk, D), BF16)
        v_win[0:blk, :] = jnp.zeros((blk, D), BF16)

    @pl.when((pl.program_id(0) == 0) & (n == 0))
    def _():
        project(xc_ref, q_cur, k_win, v_win, blk)

    @pl.when((pl.program_id(0) == 0) & (n == 0))
    def _():
        bkt = bkt_ref[...]
        qi = lax.broadcasted_iota(jnp.int32, bkt.shape, 0)
        ki = lax.broadcasted_iota(jnp.int32, bkt.shape, 1)
        delta = qi + blk - ki
        in_band = (delta >= 0) & (delta <= blk)
        for h in range(n_heads):
            acc = jnp.zeros(bkt.shape, F32)
            for b in range(tab_ref.shape[0]):
                acc = jnp.where(bkt == b, tab_ref[b, h], acc)
            acc = jnp.where(in_band, acc, NEG_BIG)
            bias_scr[1, h] = acc
            bias_scr[0, h] = jnp.where(ki >= blk, acc, NEG_BIG)

    dims = (((1,), (1,)), ((), ()))
    per_tile = LANES // hd
    lane = lax.broadcasted_iota(jnp.int32, (1, LANES), 1)
    own = [(lane >= i * hd) & (lane < (i + 1) * hd) for i in range(per_tile)]
    n_tiles = n_heads // per_tile

    pending = [(dst, which * D + c * MXU_WIDTH, c) for which, dst in enumerate((q_nxt, k_nxt, v_nxt))
               for c in range(D // MXU_WIDTH)]
    n_chunks = len(pending)
    n_slots = qb * n_heads

    def project_chunks(slot_idx):
        due = n_chunks * (slot_idx + 1) // n_slots
        while n_chunks - len(pending) < due:
            dst, col, c = pending.pop(0)
            res = _dot(xn_ref[...], w_ref[:, col:col + MXU_WIDTH])
            if dst is q_nxt:
                res = res * scale
            dst[:, c * MXU_WIDTH:(c + 1) * MXU_WIDTH] = res.astype(BF16)

    for b in range(qb):
        rows = slice(b * blk, (b + 1) * blk)
        keys = slice(b * blk, (b + 2) * blk)
        slot = jnp.minimum(n, 1) if b == 0 else 1
        for t in range(n_tiles):
            cs = slice(t * LANES, (t + 1) * LANES)
            qt = q_cur[rows, cs]
            kt = k_win[keys, cs]
            for i in range(per_tile):
                h = t * per_tile + i
                qh = jnp.where(own[i], qt, jnp.zeros_like(qt))
                s_scr[b, h] = lax.dot_general(qh, kt, dims, preferred_element_type=F32) + bias_scr[slot, h]

        lse_b = jnp.zeros((blk, LANES), F32)
        for h in range(n_heads):
            s = s_scr[b, h]
            m = jnp.max(s, axis=-1, keepdims=True)
            e = jnp.exp(s - m)
            den = jnp.sum(e, axis=-1, keepdims=True)
            p_scr[b, h] = (e * (1.0 / den)).astype(BF16)
            lse_b = jnp.where(lane == h, m + jnp.log(den), lse_b)
            project_chunks(b * n_heads + h)
        lse_ref[rows, :] = lse_b

        for t in range(n_tiles):
            cs = slice(t * LANES, (t + 1) * LANES)
            vt = v_win[keys, cs]
            acc = jnp.zeros((blk, LANES), F32)
            for i in range(per_tile):
                acc = acc + _dot(p_scr[b, t * per_tile + i], jnp.where(own[i], vt, jnp.zeros_like(vt)))
            o_ref[rows, cs] = acc

    k_win[0:blk, :] = k_win[qb * blk:(qb + 1) * blk, :]
    v_win[0:blk, :] = v_win[qb * blk:(qb + 1) * blk, :]
    k_win[blk:(qb + 1) * blk, :] = k_nxt[...]
    v_win[blk:(qb + 1) * blk, :] = v_nxt[...]
    q_cur[...] = q_nxt[...]


def _dilated_attention(xn, w_qkv, pattern, table, dilation, qb=2):
    d, rows, D = xn.shape
    n_heads = D // HEAD_DIM
    tq = qb * BLOCK
    n_steps = rows // tq
    out = lambda w: pl.BlockSpec((None, tq, w), lambda r, n: (r, n, 0))

    def following(r, n):
        nxt = jnp.minimum(r * n_steps + n + 1, d * n_steps - 1)
        return nxt // n_steps, nxt % n_steps, 0

    return pl.pallas_call(
        functools.partial(_attn_body, scale=HEAD_DIM ** -0.5),
        grid=(d, n_steps),
        in_specs=[
            pl.BlockSpec(memory_space=pltpu.SMEM),
            pl.BlockSpec((BLOCK, 2 * BLOCK), lambda r, n: (0, 0), pipeline_mode=pl.Buffered(1)),
            pl.BlockSpec((None, tq, D), lambda r, n: (0, 0, 0)),
            pl.BlockSpec((None, tq, D), following),
            pl.BlockSpec((D, 3 * D), lambda r, n: (0, pattern), pipeline_mode=pl.Buffered(1)),
        ],
        out_specs=[out(D), out(LANES)],
        out_shape=[jax.ShapeDtypeStruct((d, rows, D), F32), jax.ShapeDtypeStruct((d, rows, LANES), F32)],
        scratch_shapes=[pltpu.VMEM((2, n_heads, BLOCK, 2 * BLOCK), F32),
                        pltpu.VMEM((tq, D), BF16),
                        pltpu.VMEM((tq + BLOCK, D), BF16), pltpu.VMEM((tq + BLOCK, D), BF16),
                        pltpu.VMEM((tq, D), BF16), pltpu.VMEM((tq, D), BF16), pltpu.VMEM((tq, D), BF16),
                        pltpu.VMEM((qb, n_heads, BLOCK, 2 * BLOCK), F32),
                        pltpu.VMEM((qb, n_heads, BLOCK, 2 * BLOCK), BF16)],
        compiler_params=_params("arbitrary", "arbitrary"),
        name=f"dilated_attn_d{dilation}",
    )(table, jnp.asarray(_t5_buckets(dilation)), xn, xn, w_qkv)


def _attn_out_body(x_ref, gpost_ref, o0_ref, o1_ref, o2_ref, l0_ref, l1_ref, l2_ref, wout_ref, out_ref, *scr):
    def natural(ref, s):
        d, rows, width = ref.shape
        if d == 1:
            return ref[0]
        n_lt = width // LANES
        for c in range(n_lt):
            for r in range(d):
                s[c, pl.ds(r, rows, stride=d), :] = ref[r, :, c * LANES:(c + 1) * LANES]
        return jnp.concatenate([s[c] for c in range(n_lt)], axis=-1)

    D = x_ref.shape[1]
    n_heads = D // HEAD_DIM
    l0, l1, l2 = natural(l0_ref, None), natural(l1_ref, scr[0]), natural(l2_ref, scr[1])
    m = jnp.maximum(jnp.maximum(l0, l1), l2)
    e0, e1, e2 = jnp.exp(l0 - m), jnp.exp(l1 - m), jnp.exp(l2 - m)
    inv = 1.0 / (e0 + e1 + e2)
    lane = lax.broadcasted_iota(jnp.int32, (1, LANES), 1)
    packed = jnp.zeros(e0.shape, F32)
    for g, e in enumerate((e0, e1, e2)):
        w = jnp.where(lane < n_heads, e * inv, 0.0)
        hi = w.astype(BF16).astype(F32)
        lo = w - hi
        for part, piece in enumerate((hi, lo)):
            shift = (2 * g + part) * n_heads
            packed = packed + (pltpu.roll(piece, shift, 1) if shift else piece)
    src = lax.broadcasted_iota(jnp.int32, (LANES, 3 * D), 0)
    dst = lax.broadcasted_iota(jnp.int32, (LANES, 3 * D), 1)
    spread = ((src // (2 * n_heads) == dst // D) & (src % n_heads == (dst % D) // HEAD_DIM)
              & (src < 6 * n_heads)).astype(BF16)
    wts = _dot(packed.astype(BF16), spread)
    o = (wts[:, 0:D] * natural(o0_ref, None) + wts[:, D:2 * D] * natural(o1_ref, scr[2])
         + wts[:, 2 * D:3 * D] * natural(o2_ref, scr[3]))
    h = _dot(o.astype(BF16), wout_ref[...])
    out_ref[...] = x_ref[...] + _rms(h, gpost_ref[...])


def _attention_mixer(x, gpre, gpost, w_qkv, w_out, rel_bias, tm=512):
    L, D = x.shape
    n_heads = D // HEAD_DIM
    outs, lses = [], []
    w_qkv = w_qkv.astype(BF16)
    xns = _regroup_rows(x, gpre, [d for _, d in PATTERNS])
    for g, (window, dilation) in enumerate(PATTERNS):
        assert window // dilation == BLOCK and L % (BLOCK * dilation) == 0
        o, lse = _dilated_attention(xns[g], w_qkv, g, rel_bias[:, g * n_heads:(g + 1) * n_heads], dilation)
        outs.append(o)
        lses.append(lse)
    tile = pl.BlockSpec((tm, D), lambda i: (i, 0))
    grouped = lambda w: [pl.BlockSpec((d, tm // d, w), lambda i: (0, i, 0)) for _, d in PATTERNS]
    return pl.pallas_call(
        _attn_out_body,
        grid=(L // tm,),
        in_specs=[tile, _resident((1, D))] + grouped(D) + grouped(LANES) + [_resident((D, D))],
        out_specs=tile,
        out_shape=jax.ShapeDtypeStruct((L, D), F32),
        scratch_shapes=[pltpu.VMEM((1, tm, LANES), F32)] * 2 + [pltpu.VMEM((D // LANES, tm, LANES), F32)] * 2,
        compiler_params=_params("parallel"),
        name="attn_out",
    )(x, _row(gpost), *outs, *lses, w_out.astype(BF16))


def kernel(x, norm_pre, norm_post, ffn_w1, ffn_w3, ffn_w2, rel_bias, s5_w_in, s5_a_re, s5_a_im, s5_log_dt, s5_b_re, s5_b_im, s5_c_re, s5_c_im, s5_d, s5_w_glu, s5_b_glu, s5_w_out, cv_w_in, cv_b_in, cv_dw, cv_dw_b, cv_ln_g, cv_ln_b, cv_w_out, cv_b_out, gm_w_in, gm_b_in, gm_ln_g, gm_ln_b, gm_w_s, gm_b_s, gm_w_out, gm_b_out, at_w_qkv, at_w_out):
    bsz, seq, d_model = x.shape
    assert bsz == 1, "sequence mixers carry state along the row axis of one sequence"
    depth = norm_pre.shape[0]
    n_mixers = 4
    h = x.reshape(seq, d_model)
    stacks = (ffn_w1, ffn_w3, ffn_w2)
    mixer_mats = ((s5_w_in, s5_w_glu, s5_w_out), (cv_w_in, cv_w_out), (gm_w_in, gm_w_out), (at_w_qkv, at_w_out))
    weights = tuple(w[0, 0].astype(BF16) for w in stacks)
    for i in range(depth):
        kind, j = i % n_mixers, i // n_mixers
        n_mats = len(mixer_mats[kind])
        next_casts = [(w, (i + 1, 0)) for w in stacks] if i + 1 < depth else []
        casts = [(w, (i, 1)) for w in stacks] + [(w, (j,)) for w in mixer_mats[kind]]
        if kind == 1:
            casts += next_casts
        h, cast = _ffn(h, norm_pre[i, 0], norm_post[i, 0], weights, casts)
        weights, mats = tuple(cast[:3]), cast[3:3 + n_mats]
        if kind == 1:
            h = _conv_mixer_ffn(h, norm_pre[i, 1], norm_post[i, 1], mats[0], cv_b_in[j], cv_dw[j], cv_dw_b[j],
                                cv_ln_g[j], cv_ln_b[j], mats[1], cv_b_out[j], norm_pre[i, 2], norm_post[i, 2], weights)
            weights = tuple(cast[3 + n_mats:])
            continue
        if kind == 0:
            h = _s5_mixer(h, norm_pre[i, 1], norm_post[i, 1], mats[0], s5_a_re[j], s5_a_im[j], s5_log_dt[j],
                          s5_b_re[j], s5_b_im[j], s5_c_re[j], s5_c_im[j], s5_d[j], mats[1], s5_b_glu[j], mats[2])
        elif kind == 2:
            h = _gmlp_mixer(h, norm_pre[i, 1], norm_post[i, 1], mats[0], gm_b_in[j], gm_ln_g[j], gm_ln_b[j],
                            gm_w_s[j], gm_b_s[j], mats[1], gm_b_out[j])
        else:
            h = _attention_mixer(h, norm_pre[i, 1], norm_post[i, 1], mats[0], mats[1], rel_bias)
        h, cast = _ffn(h, norm_pre[i, 2], norm_post[i, 2], weights, next_casts)
        weights = tuple(cast)
    return h.reshape(bsz, seq, d_model)
```

```python
import functools
import math

import numpy as np
import jax
import jax.numpy as jnp
from jax import lax
from jax.experimental import pallas as pl
from jax.experimental.pallas import tpu as pltpu

F32 = jnp.float32
BF16 = jnp.bfloat16

EPS = 1e-6
LANES = 128
SUBLANES = 8
MXU_WIDTH = 256
S5_GROUP = 16
S5_STATE = 64
S5_GROUPS_PER_TILE = 8
CONV_W = 31
CONV_HALO = 32
GM_CHUNK = 128
GM_HEADS = 8
HEAD_DIM = 64
PATTERNS = ((128, 1), (512, 4), (2048, 16))
BLOCK = 128
NUM_BUCKETS = 32
MAX_DISTANCE = 2048
NEG_BIG = -1e30

VMEM_LIMIT_BYTES = 56 * 1024 * 1024


def _params(*sem):
    return pltpu.CompilerParams(dimension_semantics=sem, vmem_limit_bytes=VMEM_LIMIT_BYTES)


def _rms(x, g):
    return x * lax.rsqrt(jnp.mean(x * x, axis=-1, keepdims=True) + EPS) * g


def _layernorm(x, g, b):
    mu = jnp.mean(x, axis=-1, keepdims=True)
    xc = x - mu
    var = jnp.mean(xc * xc, axis=-1, keepdims=True)
    return xc * lax.rsqrt(var + EPS) * g + b


def _dot(a, b):
    return jnp.dot(a, b, preferred_element_type=F32)


def _row(v):
    return v.reshape(1, -1)


def _ffn_body(x_ref, gpre_ref, gpost_ref, w1_ref, w3_ref, w2_ref, *refs, fc):
    n_cast = (len(refs) - 2) // 2
    src_refs, o_ref, dst_refs, g_scr = refs[:n_cast], refs[n_cast], refs[n_cast + 1:-1], refs[-1]
    for src, dst in zip(src_refs, dst_refs):
        dst[...] = src[...].astype(dst.dtype)
    n_sub, sub, F = g_scr.shape
    for s in range(n_sub):
        rs = slice(s * sub, (s + 1) * sub)
        xn = _rms(x_ref[rs, :], gpre_ref[...]).astype(BF16)
        for c in range(F // fc):
            cs = slice(c * fc, (c + 1) * fc)
            h1 = _dot(xn, w1_ref[:, cs])
            h3 = _dot(xn, w3_ref[:, cs])
            g_scr[s, :, cs] = ((h1 * jax.nn.sigmoid(h1)) * h3).astype(BF16)
        h = _dot(g_scr[s], w2_ref[...])
        o_ref[rs, :] = x_ref[rs, :] + 0.5 * _rms(h, gpost_ref[...])


def _resident(shape):
    return pl.BlockSpec(shape, lambda i: (0,) * len(shape), pipeline_mode=pl.Buffered(1))


def _ffn(x, gpre, gpost, weights, casts=(), tm=1024, sub=512, fc=256):
    L, D = x.shape
    w1, w3, w2 = weights
    F = w1.shape[-1]
    n_steps = L // tm
    in_specs = [pl.BlockSpec((tm, D), lambda i: (i, 0)), _resident((1, D)), _resident((1, D)),
                _resident((D, F)), _resident((D, F)), _resident((F, D))]
    out_specs = [pl.BlockSpec((tm, D), lambda i: (i, 0))]
    out_shape = [jax.ShapeDtypeStruct((L, D), F32)]
    operands = [x, _row(gpre), _row(gpost), w1, w3, w2]
    for w, lead in casts:
        rows, cols = w.shape[-2] // n_steps, w.shape[-1]
        in_specs.append(pl.BlockSpec((None,) * len(lead) + (rows, cols), lambda i, lead=lead: lead + (i, 0)))
        out_specs.append(pl.BlockSpec((rows, cols), lambda i: (i, 0)))
        out_shape.append(jax.ShapeDtypeStruct(w.shape[-2:], BF16))
        operands.append(w)
    outs = pl.pallas_call(
        functools.partial(_ffn_body, fc=fc),
        grid=(n_steps,),
        in_specs=in_specs,
        out_specs=out_specs,
        out_shape=out_shape,
        scratch_shapes=[pltpu.VMEM((tm // sub, sub, F), BF16)],
        compiler_params=_params("arbitrary"),
        name="ffn",
    )(*operands)
    return outs[0], list(outs[1:])


def _s5_body(x_ref, gpre_ref, gpost_ref, win_ref, are_ref, aim_ref, ldt_ref, bdre_ref, bdim_ref,
             cdre_ref, cdim_ref, d_ref, wglu_ref, bglu_ref, wout_ref, o_ref,
             abar_re, abar_im, bb_re, bb_im, st_re, st_im, bu_re, bu_im, u_scr, y_scr):
    tt = x_ref.shape[0]
    n_tiles, uw, sw = bdre_ref.shape
    nj = sw // LANES

    @pl.when(pl.program_id(0) == 0)
    def _():
        ar = are_ref[...]
        ai = aim_ref[...]
        dt = jnp.exp(ldt_ref[...])
        mag = jnp.exp(dt * ar)
        abr = mag * jnp.cos(dt * ai)
        abi = mag * jnp.sin(dt * ai)
        den = ar * ar + ai * ai
        nr = abr - 1.0
        fre = (nr * ar + abi * ai) / den
        fim = (abi * ar - nr * ai) / den
        abar_re[...] = abr
        abar_im[...] = abi
        for k in range(n_tiles):
            fr = fre[k:k + 1, :]
            fi = fim[k:k + 1, :]
            bb_re[k] = (fr * bdre_ref[k] - fi * bdim_ref[k]).astype(BF16)
            bb_im[k] = (fr * bdim_ref[k] + fi * bdre_ref[k]).astype(BF16)
        st_re[...] = jnp.zeros_like(st_re)
        st_im[...] = jnp.zeros_like(st_im)

    n_sub = bu_re.shape[0]
    sub = tt // n_sub
    for s in range(n_sub):
        rs = slice(s * sub, (s + 1) * sub)
        xn = _rms(x_ref[rs, :], gpre_ref[...]).astype(BF16)
        u = _dot(xn, win_ref[...])
        u_scr[rs, :] = u
        ub = u.astype(BF16)
        for k in range(n_tiles):
            uk = ub[:, k * uw:(k + 1) * uw]
            br = _dot(uk, bb_re[k])
            bi = _dot(uk, bb_im[k])
            for j in range(nj):
                bu_re[s, j, pl.ds(k, sub, stride=n_tiles), :] = br[:, j * LANES:(j + 1) * LANES]
                bu_im[s, j, pl.ds(k, sub, stride=n_tiles), :] = bi[:, j * LANES:(j + 1) * LANES]

    a_r = [abar_re[:, j * LANES:(j + 1) * LANES] for j in range(nj)]
    a_i = [abar_im[:, j * LANES:(j + 1) * LANES] for j in range(nj)]
    state = [(st_re[j], st_im[j]) for j in range(nj)]
    for s in range(n_sub):
        for t in range(sub):
            rows = slice(t * n_tiles, (t + 1) * n_tiles)
            for j in range(nj):
                sr, si = state[j]
                nr = a_r[j] * sr - a_i[j] * si + bu_re[s, j, rows, :]
                ni = a_r[j] * si + a_i[j] * sr + bu_im[s, j, rows, :]
                bu_re[s, j, rows, :] = nr
                bu_im[s, j, rows, :] = ni
                state[j] = (nr, ni)
    for j in range(nj):
        st_re[j], st_im[j] = state[j]

    for s in range(n_sub):
        rs = slice(s * sub, (s + 1) * sub)
        for k in range(n_tiles):
            sre = jnp.concatenate([bu_re[s, j, pl.ds(k, sub, stride=n_tiles), :] for j in range(nj)], axis=-1)
            sim = jnp.concatenate([bu_im[s, j, pl.ds(k, sub, stride=n_tiles), :] for j in range(nj)], axis=-1)
            y_scr[rs, k * uw:(k + 1) * uw] = (_dot(sre.astype(BF16), cdre_ref[k])
                                              - _dot(sim.astype(BF16), cdim_ref[k]))
        y = jax.nn.gelu(y_scr[rs, :] + d_ref[...] * u_scr[rs, :])
        z = y * jax.nn.sigmoid(_dot(y.astype(BF16), wglu_ref[...]) + bglu_ref[...])
        h = _dot(z.astype(BF16), wout_ref[...])
        o_ref[rs, :] = x_ref[rs, :] + _rms(h, gpost_ref[...])


def _s5_block_diag(b, c):
    G, P, HG = b.shape
    gt = S5_GROUPS_PER_TILE
    eye = jnp.eye(gt, dtype=b.dtype)
    bt = b.reshape(G // gt, gt, P, HG).transpose(0, 1, 3, 2)
    bd = jnp.einsum('kghp,gj->kghjp', bt, eye).reshape(G // gt, gt * HG, gt * P)
    ct = c.reshape(G // gt, gt, HG, P)
    cd = jnp.einsum('kghp,gj->kjpgh', ct, eye).reshape(G // gt, gt * P, gt * HG)
    return bd, cd


def _s5_mixer(x, gpre, gpost, w_in, a_re, a_im, log_dt, b_re, b_im, c_re, c_im, d_skip, w_glu, b_glu,
              w_out, tt=512, sub=512):
    L, D = x.shape
    G, P = a_re.shape
    bd_re, cd_re = _s5_block_diag(b_re, c_re)
    bd_im, cd_im = _s5_block_diag(b_im, c_im)
    n_tiles, uw, sw = bd_re.shape
    assert n_tiles == SUBLANES and uw == LANES and sw % LANES == 0
    nj = sw // LANES
    per_tile = lambda v: v.reshape(n_tiles, sw)
    return pl.pallas_call(
        _s5_body,
        grid=(L // tt,),
        in_specs=[
            pl.BlockSpec((tt, D), lambda i: (i, 0)),
            _resident((1, D)), _resident((1, D)), _resident((D, D)),
            _resident((n_tiles, sw)), _resident((n_tiles, sw)), _resident((n_tiles, sw)),
            _resident((n_tiles, uw, sw)), _resident((n_tiles, uw, sw)),
            _resident((n_tiles, sw, uw)), _resident((n_tiles, sw, uw)),
            _resident((1, D)), _resident((D, D)), _resident((1, D)), _resident((D, D)),
        ],
        out_specs=pl.BlockSpec((tt, D), lambda i: (i, 0)),
        out_shape=jax.ShapeDtypeStruct((L, D), F32),
        scratch_shapes=[
            pltpu.VMEM((n_tiles, sw), F32), pltpu.VMEM((n_tiles, sw), F32),
            pltpu.VMEM((n_tiles, uw, sw), BF16), pltpu.VMEM((n_tiles, uw, sw), BF16),
            pltpu.VMEM((nj, n_tiles, LANES), F32), pltpu.VMEM((nj, n_tiles, LANES), F32),
            pltpu.VMEM((tt // sub, nj, sub * n_tiles, LANES), F32),
            pltpu.VMEM((tt // sub, nj, sub * n_tiles, LANES), F32),
            pltpu.VMEM((tt, D), F32), pltpu.VMEM((tt, D), F32),
        ],
        compiler_params=_params("arbitrary"),
        name="s5_mixer",
    )(x, _row(gpre), _row(gpost), w_in.astype(BF16),
      per_tile(a_re), per_tile(a_im), per_tile(jnp.broadcast_to(log_dt[:, None], (G, P))),
      bd_re, bd_im, cd_re.astype(BF16), cd_im.astype(BF16),
      _row(d_skip), w_glu.astype(BF16), _row(b_glu), w_out.astype(BF16))


def _interleave(a, b):
    out, ia, ib = [], 0, 0
    while ia < len(a) or ib < len(b):
        if ib >= len(b) or (ia < len(a) and ia * len(b) <= ib * len(a)):
            out.append(a[ia])
            ia += 1
        else:
            out.append(b[ib])
            ib += 1
    return out


def _conv_ffn_body(x_ref, cpre_ref, cpost_ref, win_ref, bin_ref, dw_ref, dwb_ref, lng_ref, lnb_ref,
                   wout_ref, bout_ref, fpre_ref, fpost_ref, w1_ref, w3_ref, w2_ref, o_ref,
                   ext, conv_scr, xmid, xprev, xn_c, xn_f, yb, hc_scr, g_scr, hf_scr, *, fc):
    tm, D = x_ref.shape
    halo = ext.shape[0] - tm
    n_taps = dw_ref.shape[0]
    F = g_scr.shape[1]
    col_chunks = [slice(c * MXU_WIDTH, (c + 1) * MXU_WIDTH) for c in range(D // MXU_WIDTH)]

    @pl.when(pl.program_id(0) == 0)
    def _():
        ext[0:halo, :] = jnp.zeros((halo, D), F32)
        xmid[...] = jnp.zeros_like(xmid)

    def ffn_start():
        xprev[...] = xmid[...]
        xn_f[...] = _rms(xprev[...], fpre_ref[...]).astype(BF16)

    def ffn_hidden(cs):
        h1 = _dot(xn_f[...], w1_ref[:, cs])
        h3 = _dot(xn_f[...], w3_ref[:, cs])
        g_scr[:, cs] = ((h1 * jax.nn.sigmoid(h1)) * h3).astype(BF16)

    def ffn_down(cs):
        hf_scr[:, cs] = _dot(g_scr[...], w2_ref[:, cs])

    def ffn_finish():
        o_ref[...] = xprev[...] + 0.5 * _rms(hf_scr[...], fpost_ref[...])

    ffn = ([ffn_start] + [functools.partial(ffn_hidden, slice(c * fc, (c + 1) * fc)) for c in range(F // fc)]
           + [functools.partial(ffn_down, cs) for cs in col_chunks] + [ffn_finish])

    def conv_start():
        xn_c[...] = _rms(x_ref[...], cpre_ref[...]).astype(BF16)

    def conv_glu(cs):
        a = _dot(xn_c[...], win_ref[:, cs]) + bin_ref[:, cs]
        gs = slice(D + cs.start, D + cs.stop)
        g = _dot(xn_c[...], win_ref[:, gs]) + bin_ref[:, gs]
        ext[halo:halo + tm, cs] = a * jax.nn.sigmoid(g)

    base = halo - (n_taps - 1)

    def conv_taps(c):
        cs = slice(c * LANES, (c + 1) * LANES)
        acc = None
        for s in range(SUBLANES):
            rows = tm if s == 0 else tm + SUBLANES
            q = None
            for a in range((base + n_taps - 1) // SUBLANES + 1):
                k = SUBLANES * a + s - base
                if 0 <= k < n_taps:
                    term = dw_ref[k:k + 1, cs] * ext[SUBLANES * a:SUBLANES * a + rows, cs]
                    q = term if q is None else q + term
            part = q if s == 0 else q[s:s + tm]
            acc = part if acc is None else acc + part
        conv_scr[:, cs] = acc

    def conv_norm():
        ext[0:halo, :] = ext[tm:tm + halo, :]
        y = _layernorm(conv_scr[...] + dwb_ref[...], lng_ref[...], lnb_ref[...])
        yb[...] = (y * jax.nn.sigmoid(y)).astype(BF16)

    def conv_out(cs):
        hc_scr[:, cs] = _dot(yb[...], wout_ref[:, cs]) + bout_ref[:, cs]

    def conv_finish():
        xmid[...] = x_ref[...] + _rms(hc_scr[...], cpost_ref[...])

    conv = ([conv_start] + [functools.partial(conv_glu, cs) for cs in col_chunks]
            + [functools.partial(conv_taps, c) for c in range(D // LANES)] + [conv_norm]
            + [functools.partial(conv_out, cs) for cs in col_chunks] + [conv_finish])

    for piece in _interleave(ffn, conv):
        piece()


def _conv_mixer_ffn(x, cpre, cpost, w_in, b_in, dw, dw_b, ln_g, ln_b, w_out, b_out, fpre, fpost, weights,
                    tm=512, fc=256):
    L, D = x.shape
    w1, w3, w2 = weights
    F = w1.shape[-1]
    n = L // tm
    return pl.pallas_call(
        functools.partial(_conv_ffn_body, fc=fc),
        grid=(n + 1,),
        in_specs=[
            pl.BlockSpec((tm, D), lambda i: (jnp.minimum(i, n - 1), 0)),
            _resident((1, D)), _resident((1, D)), _resident((D, 2 * D)), _resident((1, 2 * D)),
            _resident(dw.shape), _resident((1, D)), _resident((1, D)), _resident((1, D)),
            _resident((D, D)), _resident((1, D)),
            _resident((1, D)), _resident((1, D)), _resident((D, F)), _resident((D, F)), _resident((F, D)),
        ],
        out_specs=pl.BlockSpec((tm, D), lambda i: (jnp.maximum(i - 1, 0), 0)),
        out_shape=jax.ShapeDtypeStruct((L, D), F32),
        scratch_shapes=[pltpu.VMEM((tm + CONV_HALO, D), F32), pltpu.VMEM((tm, D), F32),
                        pltpu.VMEM((tm, D), F32), pltpu.VMEM((tm, D), F32),
                        pltpu.VMEM((tm, D), BF16), pltpu.VMEM((tm, D), BF16), pltpu.VMEM((tm, D), BF16),
                        pltpu.VMEM((tm, D), F32), pltpu.VMEM((tm, F), BF16), pltpu.VMEM((tm, D), F32)],
        compiler_params=_params("arbitrary"),
        name="conv_mixer_ffn",
    )(x, _row(cpre), _row(cpost), w_in, _row(b_in), dw, _row(dw_b), _row(ln_g), _row(ln_b), w_out, _row(b_out),
      _row(fpre), _row(fpost), w1, w3, w2)


def _gmlp_body(x_ref, gpre_ref, gpost_ref, win_ref, bin_ref, lng_ref, lnb_ref, ws_ref, bs_ref,
               wout_ref, bout_ref, o_ref, z_scr, su_scr, *, fc):
    n_sub, sub, _ = su_scr.shape
    E = lng_ref.shape[1]
    n_heads, chunk, _ = ws_ref.shape
    hw = E // n_heads
    causal = (lax.broadcasted_iota(jnp.int32, (chunk, chunk), 1)
              <= lax.broadcasted_iota(jnp.int32, (chunk, chunk), 0))
    wm = [jnp.where(causal, ws_ref[h], 0.0).astype(BF16) for h in range(n_heads)]
    for i in range(n_sub):
        rows = slice(i * sub, (i + 1) * sub)
        x = x_ref[rows, :]
        xn = _rms(x, gpre_ref[...]).astype(BF16)
        for c in range(2 * E // fc):
            cs = slice(c * fc, (c + 1) * fc)
            z_scr[i, :, cs] = jax.nn.gelu(_dot(xn, win_ref[:, cs]) + bin_ref[:, cs])
        vb = _layernorm(z_scr[i, :, E:], lng_ref[...], lnb_ref[...]).astype(BF16)
        for h in range(n_heads):
            bias = bs_ref[:, h:h + 1]
            hs = slice(h * hw, (h + 1) * hw)
            for c in range(sub // chunk):
                rs = slice(c * chunk, (c + 1) * chunk)
                s = _dot(wm[h], vb[rs, hs]) + bias
                su_scr[i, rs, hs] = (z_scr[i, rs, hs] * s).astype(BF16)
        hout = _dot(su_scr[i], wout_ref[...]) + bout_ref[...]
        o_ref[rows, :] = x + _rms(hout, gpost_ref[...])


def _gmlp_mixer(x, gpre, gpost, w_in, b_in, ln_g, ln_b, w_s, b_s, w_out, b_out, tm=512, sub=512, fc=512):
    L, D = x.shape
    E = ln_g.shape[0]
    return pl.pallas_call(
        functools.partial(_gmlp_body, fc=fc),
        grid=(L // tm,),
        in_specs=[
            pl.BlockSpec((tm, D), lambda i: (i, 0)),
            _resident((1, D)), _resident((1, D)), _resident((D, 2 * E)), _resident((1, 2 * E)),
            _resident((1, E)), _resident((1, E)),
            _resident(w_s.shape), _resident((GM_CHUNK, GM_HEADS)),
            _resident((E, D)), _resident((1, D)),
        ],
        out_specs=pl.BlockSpec((tm, D), lambda i: (i, 0)),
        out_shape=jax.ShapeDtypeStruct((L, D), F32),
        scratch_shapes=[pltpu.VMEM((tm // sub, sub, 2 * E), F32), pltpu.VMEM((tm // sub, sub, E), BF16)],
        compiler_params=_params("parallel"),
        name="gmlp_mixer",
    )(x, _row(gpre), _row(gpost), w_in.astype(BF16), _row(b_in), _row(ln_g), _row(ln_b),
      w_s, b_s.T, w_out.astype(BF16), _row(b_out))


def _regroup_body(x_ref, g_ref, *refs):
    *o_refs, xt_scr = refs
    xn = _rms(x_ref[...], g_ref[...])
    for c in range(xt_scr.shape[0]):
        xt_scr[c] = xn[:, c * LANES:(c + 1) * LANES]
    for o_ref in o_refs:
        d, rows, _ = o_ref.shape
        if d == 1:
            o_ref[0] = xn.astype(o_ref.dtype)
            continue
        for r in range(d):
            o_ref[r] = jnp.concatenate([xt_scr[c, pl.ds(r, rows, stride=d), :] for c in range(xt_scr.shape[0])],
                                       axis=-1).astype(o_ref.dtype)


def _regroup_rows(x, g, dilations, tm=512):
    L, D = x.shape
    return pl.pallas_call(
        _regroup_body,
        grid=(L // tm,),
        in_specs=[pl.BlockSpec((tm, D), lambda i: (i, 0)), _resident((1, D))],
        out_specs=[pl.BlockSpec((d, tm // d, D), lambda i: (0, i, 0)) for d in dilations],
        out_shape=[jax.ShapeDtypeStruct((d, L // d, D), BF16) for d in dilations],
        scratch_shapes=[pltpu.VMEM((D // LANES, tm, LANES), F32)],
        compiler_params=_params("parallel"),
        name="regroup_rows",
    )(x, _row(g))


def _t5_buckets(dilation):
    delta = (np.arange(BLOCK)[:, None] + BLOCK) - np.arange(2 * BLOCK)[None, :]
    dist = np.maximum(delta, 0) * dilation
    max_exact = NUM_BUCKETS // 2
    distf = np.maximum(dist, 1).astype(np.float32)
    large = max_exact + (np.log(distf / np.float32(max_exact)) / np.float32(math.log(MAX_DISTANCE / max_exact))
                         * np.float32(NUM_BUCKETS - max_exact)).astype(np.int32)
    large = np.minimum(large, NUM_BUCKETS - 1)
    return np.where(dist < max_exact, dist, large).astype(np.int32)


def _attn_body(tab_ref, bkt_ref, xc_ref, xn_ref, w_ref, o_ref, lse_ref,
               bias_scr, q_cur, k_win, v_win, q_nxt, k_nxt, v_nxt, s_scr, p_scr, *, scale):
    blk = bias_scr.shape[2]
    n_heads = bias_scr.shape[1]
    D = q_cur.shape[1]
    hd = D // n_heads
    qb = q_cur.shape[0] // blk
    n = pl.program_id(1)

    def project(x_ref, q_dst, k_dst, v_dst, k_off):
        xb = x_ref[...]
        q_dst[...] = (_dot(xb, w_ref[:, 0:D]) * scale).astype(BF16)
        k_dst[k_off:k_off + qb * blk, :] = _dot(xb, w_ref[:, D:2 * D]).astype(BF16)
        v_dst[k_off:k_off + qb * blk, :] = _dot(xb, w_ref[:, 2 * D:3 * D]).astype(BF16)

    @pl.when(n == 0)
    def _():
        k_win[0:blk, :] = jnp.zeros((blk, D), BF16)
        v_win[0:blk, :] = jnp.zeros((blk, D), BF16)

    @pl.when((pl.program_id(0) == 0) & (n == 0))
    def _():
        project(xc_ref, q_cur, k_win, v_win, blk)

    @pl.when((pl.program_id(0) == 0) & (n == 0))
    def _():
        bkt = bkt_ref[...]
        qi = lax.broadcasted_iota(jnp.int32, bkt.shape, 0)
        ki = lax.broadcasted_iota(jnp.int32, bkt.shape, 1)
        delta = qi + blk - ki
        in_band = (delta >= 0) & (delta <= blk)
        for h in range(n_heads):
            acc = jnp.zeros(bkt.shape, F32)
            for b in range(tab_ref.shape[0]):
                acc = jnp.where(bkt == b, tab_ref[b, h], acc)
            acc = jnp.where(in_band, acc, NEG_BIG)
            bias_scr[1, h] = acc
            bias_scr[0, h] = jnp.where(ki >= blk, acc, NEG_BIG)

    dims = (((1,), (1,)), ((), ()))
    per_tile = LANES // hd
    lane = lax.broadcasted_iota(jnp.int32, (1, LANES), 1)
    own = [(lane >= i * hd) & (lane < (i + 1) * hd) for i in range(per_tile)]
    n_tiles = n_heads // per_tile

    pending = [(dst, which * D + c * MXU_WIDTH, c) for which, dst in enumerate((q_nxt, k_nxt, v_nxt))
               for c in range(D // MXU_WIDTH)]
    n_chunks = len(pending)
    n_slots = qb * n_heads

    def project_chunks(slot_idx):
        due = n_chunks * (slot_idx + 1) // n_slots
        while n_chunks - len(pending) < due:
            dst, col, c = pending.pop(0)
            res = _dot(xn_ref[...], w_ref[:, col:col + MXU_WIDTH])
            if dst is q_nxt:
                res = res * scale
            dst[:, c * MXU_WIDTH:(c + 1) * MXU_WIDTH] = res.astype(BF16)

    for b in range(qb):
        rows = slice(b * blk, (b + 1) * blk)
        keys = slice(b * blk, (b + 2) * blk)
        slot = jnp.minimum(n, 1) if b == 0 else 1
        for t in range(n_tiles):
            cs = slice(t * LANES, (t + 1) * LANES)
            qt = q_cur[rows, cs]
            kt = k_win[keys, cs]
            for i in range(per_tile):
                h = t * per_tile + i
                qh = jnp.where(own[i], qt, jnp.zeros_like(qt))
                s_scr[b, h] = lax.dot_general(qh, kt, dims, preferred_element_type=F32) + bias_scr[slot, h]

        lse_b = jnp.zeros((blk, LANES), F32)
        for h in range(n_heads):
            s = s_scr[b, h]
            m = jnp.max(s, axis=-1, keepdims=True)
            e = jnp.exp(s - m)
            den = jnp.sum(e, axis=-1, keepdims=True)
            p_scr[b, h] = (e * (1.0 / den)).astype(BF16)
            lse_b = jnp.where(lane == h, m + jnp.log(den), lse_b)
            project_chunks(b * n_heads + h)
        lse_ref[rows, :] = lse_b

        for t in range(n_tiles):
            cs = slice(t * LANES, (t + 1) * LANES)
            vt = v_win[keys, cs]
            acc = jnp.zeros((blk, LANES), F32)
            for i in range(per_tile):
                acc = acc + _dot(p_scr[b, t * per_tile + i], jnp.where(own[i], vt, jnp.zeros_like(vt)))
            o_ref[rows, cs] = acc

    k_win[0:blk, :] = k_win[qb * blk:(qb + 1) * blk, :]
    v_win[0:blk, :] = v_win[qb * blk:(qb + 1) * blk, :]
    k_win[blk:(qb + 1) * blk, :] = k_nxt[...]
    v_win[blk:(qb + 1) * blk, :] = v_nxt[...]
    q_cur[...] = q_nxt[...]


def _dilated_attention(xn, w_qkv, pattern, table, dilation, qb=2):
    d, rows, D = xn.shape
    n_heads = D // HEAD_DIM
    tq = qb * BLOCK
    n_steps = rows // tq
    out = lambda w: pl.BlockSpec((None, tq, w), lambda r, n: (r, n, 0))

    def following(r, n):
        nxt = jnp.minimum(r * n_steps + n + 1, d * n_steps - 1)
        return nxt // n_steps, nxt % n_steps, 0

    return pl.pallas_call(
        functools.partial(_attn_body, scale=HEAD_DIM ** -0.5),
        grid=(d, n_steps),
        in_specs=[
            pl.BlockSpec(memory_space=pltpu.SMEM),
            pl.BlockSpec((BLOCK, 2 * BLOCK), lambda r, n: (0, 0), pipeline_mode=pl.Buffered(1)),
            pl.BlockSpec((None, tq, D), lambda r, n: (0, 0, 0)),
            pl.BlockSpec((None, tq, D), following),
            pl.BlockSpec((D, 3 * D), lambda r, n: (0, pattern), pipeline_mode=pl.Buffered(1)),
        ],
        out_specs=[out(D), out(LANES)],
        out_shape=[jax.ShapeDtypeStruct((d, rows, D), F32), jax.ShapeDtypeStruct((d, rows, LANES), F32)],
        scratch_shapes=[pltpu.VMEM((2, n_heads, BLOCK, 2 * BLOCK), F32),
                        pltpu.VMEM((tq, D), BF16),
                        pltpu.VMEM((tq + BLOCK, D), BF16), pltpu.VMEM((tq + BLOCK, D), BF16),
                        pltpu.VMEM((tq, D), BF16), pltpu.VMEM((tq, D), BF16), pltpu.VMEM((tq, D), BF16),
                        pltpu.VMEM((qb, n_heads, BLOCK, 2 * BLOCK), F32),
                        pltpu.VMEM((qb, n_heads, BLOCK, 2 * BLOCK), BF16)],
        compiler_params=_params("arbitrary", "arbitrary"),
        name=f"dilated_attn_d{dilation}",
    )(table, jnp.asarray(_t5_buckets(dilation)), xn, xn, w_qkv)


def _attn_out_body(x_ref, gpost_ref, o0_ref, o1_ref, o2_ref, l0_ref, l1_ref, l2_ref, wout_ref, out_ref, *scr):
    def natural(ref, s):
        d, rows, width = ref.shape
        if d == 1:
            return ref[0]
        n_lt = width // LANES
        for c in range(n_lt):
            for r in range(d):
                s[c, pl.ds(r, rows, stride=d), :] = ref[r, :, c * LANES:(c + 1) * LANES]
        return jnp.concatenate([s[c] for c in range(n_lt)], axis=-1)

    D = x_ref.shape[1]
    n_heads = D // HEAD_DIM
    l0, l1, l2 = natural(l0_ref, None), natural(l1_ref, scr[0]), natural(l2_ref, scr[1])
    m = jnp.maximum(jnp.maximum(l0, l1), l2)
    e0, e1, e2 = jnp.exp(l0 - m), jnp.exp(l1 - m), jnp.exp(l2 - m)
    inv = 1.0 / (e0 + e1 + e2)
    lane = lax.broadcasted_iota(jnp.int32, (1, LANES), 1)
    packed = jnp.zeros(e0.shape, F32)
    for g, e in enumerate((e0, e1, e2)):
        w = jnp.where(lane < n_heads, e * inv, 0.0)
        hi = w.astype(BF16).astype(F32)
        lo = w - hi
        for part, piece in enumerate((hi, lo)):
            shift = (2 * g + part) * n_heads
            packed = packed + (pltpu.roll(piece, shift, 1) if shift else piece)
    src = lax.broadcasted_iota(jnp.int32, (LANES, 3 * D), 0)
    dst = lax.broadcasted_iota(jnp.int32, (LANES, 3 * D), 1)
    spread = ((src // (2 * n_heads) == dst // D) & (src % n_heads == (dst % D) // HEAD_DIM)
              & (src < 6 * n_heads)).astype(BF16)
    wts = _dot(packed.astype(BF16), spread)
    o = (wts[:, 0:D] * natural(o0_ref, None) + wts[:, D:2 * D] * natural(o1_ref, scr[2])
         + wts[:, 2 * D:3 * D] * natural(o2_ref, scr[3]))
    h = _dot(o.astype(BF16), wout_ref[...])
    out_ref[...] = x_ref[...] + _rms(h, gpost_ref[...])


def _attention_mixer(x, gpre, gpost, w_qkv, w_out, rel_bias, tm=512):
    L, D = x.shape
    n_heads = D // HEAD_DIM
    outs, lses = [], []
    w_qkv = w_qkv.astype(BF16)
    xns = _regroup_rows(x, gpre, [d for _, d in PATTERNS])
    for g, (window, dilation) in enumerate(PATTERNS):
        assert window // dilation == BLOCK and L % (BLOCK * dilation) == 0
        o, lse = _dilated_attention(xns[g], w_qkv, g, rel_bias[:, g * n_heads:(g + 1) * n_heads], dilation)
        outs.append(o)
        lses.append(lse)
    tile = pl.BlockSpec((tm, D), lambda i: (i, 0))
    grouped = lambda w: [pl.BlockSpec((d, tm // d, w), lambda i: (0, i, 0)) for _, d in PATTERNS]
    return pl.pallas_call(
        _attn_out_body,
        grid=(L // tm,),
        in_specs=[tile, _resident((1, D))] + grouped(D) + grouped(LANES) + [_resident((D, D))],
        out_specs=tile,
        out_shape=jax.ShapeDtypeStruct((L, D), F32),
        scratch_shapes=[pltpu.VMEM((1, tm, LANES), F32)] * 2 + [pltpu.VMEM((D // LANES, tm, LANES), F32)] * 2,
        compiler_params=_params("parallel"),
        name="attn_out",
    )(x, _row(gpost), *outs, *lses, w_out.astype(BF16))


def kernel(x, norm_pre, norm_post, ffn_w1, ffn_w3, ffn_w2, rel_bias, s5_w_in, s5_a_re, s5_a_im, s5_log_dt, s5_b_re, s5_b_im, s5_c_re, s5_c_im, s5_d, s5_w_glu, s5_b_glu, s5_w_out, cv_w_in, cv_b_in, cv_dw, cv_dw_b, cv_ln_g, cv_ln_b, cv_w_out, cv_b_out, gm_w_in, gm_b_in, gm_ln_g, gm_ln_b, gm_w_s, gm_b_s, gm_w_out, gm_b_out, at_w_qkv, at_w_out):
    bsz, seq, d_model = x.shape
    assert bsz == 1, "sequence mixers carry state along the row axis of one sequence"
    depth = norm_pre.shape[0]
    n_mixers = 4
    h = x.reshape(seq, d_model)
    stacks = (ffn_w1, ffn_w3, ffn_w2)
    mixer_mats = ((s5_w_in, s5_w_glu, s5_w_out), (cv_w_in, cv_w_out), (gm_w_in, gm_w_out), (at_w_qkv, at_w_out))
    weights = tuple(w[0, 0].astype(BF16) for w in stacks)
    for i in range(depth):
        kind, j = i % n_mixers, i // n_mixers
        n_mats = len(mixer_mats[kind])
        next_casts = [(w, (i + 1, 0)) for w in stacks] if i + 1 < depth else []
        casts = [(w, (i, 1)) for w in stacks] + [(w, (j,)) for w in mixer_mats[kind]]
        if kind == 1:
            casts += next_casts
        h, cast = _ffn(h, norm_pre[i, 0], norm_post[i, 0], weights, casts)
        weights, mats = tuple(cast[:3]), cast[3:3 + n_mats]
        if kind == 1:
            h = _conv_mixer_ffn(h, norm_pre[i, 1], norm_post[i, 1], mats[0], cv_b_in[j], cv_dw[j], cv_dw_b[j],
                                cv_ln_g[j], cv_ln_b[j], mats[1], cv_b_out[j], norm_pre[i, 2], norm_post[i, 2], weights)
            weights = tuple(cast[3 + n_mats:])
            continue
        if kind == 0:
            h = _s5_mixer(h, norm_pre[i, 1], norm_post[i, 1], mats[0], s5_a_re[j], s5_a_im[j], s5_log_dt[j],
                          s5_b_re[j], s5_b_im[j], s5_c_re[j], s5_c_im[j], s5_d[j], mats[1], s5_b_glu[j], mats[2])
        elif kind == 2:
            h = _gmlp_mixer(h, norm_pre[i, 1], norm_post[i, 1], mats[0], gm_b_in[j], gm_ln_g[j], gm_ln_b[j],
                            gm_w_s[j], gm_b_s[j], mats[1], gm_b_out[j])
        else:
            h = _attention_mixer(h, norm_pre[i, 1], norm_post[i, 1], mats[0], mats[1], rel_bias)
        h, cast = _ffn(h, norm_pre[i, 2], norm_post[i, 2], weights, next_casts)
        weights = tuple(cast)
    return h.reshape(bsz, seq, d_model)
```
